```python
import math
import jax, jax.numpy as jnp
from jax import lax
import numpy as np

D_MODEL = 1024
BATCH = 16
SEQ = 2048
DEPTH = 1

D_RNN = D_MODEL
H_R = 16
DH_R = D_RNN // H_R
RG_C = 8.0
CONV_W = 4
D_M = D_MODEL
H_M = 4
DH_M = D_M // H_M
CHUNK = 128
IN_SIZES = [D_RNN, D_RNN, D_M, D_M, D_M, H_M, H_M, D_MODEL, D_MODEL]
D_IN = sum(IN_SIZES)
EPS = 1e-6

kernel_name = "hybrid_rglru_mlstm_gated_block"


def rms_norm(x, g):
    xf = x.astype(jnp.float32)
    y = xf * lax.rsqrt(jnp.mean(xf * xf, axis=-1, keepdims=True) + EPS)
    return (y * g.astype(jnp.float32)).astype(x.dtype)


def causal_dwconv(x, w, b):
    C = x.shape[-1]
    y = lax.conv_general_dilated(
        x, w[:, None, :].astype(x.dtype), window_strides=(1,), padding=[(CONV_W - 1, 0)],
        dimension_numbers=("NWC", "WIO", "NWC"), feature_group_count=C)
    return y + b.astype(x.dtype)


def headwise(x, w):
    B, S, _ = x.shape
    H, d, e = w.shape
    y = jnp.einsum("bshd,hde->bshe", x.reshape(B, S, H, d), w)
    return y.reshape(B, S, H * e)


def rg_lru(u, gate_a, gate_x, lam):
    uf = u.astype(jnp.float32)
    r = jax.nn.sigmoid(gate_a.astype(jnp.float32))
    i = jax.nn.sigmoid(gate_x.astype(jnp.float32))
    log_a = -RG_C * r * jax.nn.softplus(-lam.astype(jnp.float32))
    a = jnp.exp(log_a)
    b = jnp.sqrt(-jnp.expm1(2.0 * log_a)) * (i * uf)

    def combine(c1, c2):
        a1, b1 = c1
        a2, b2 = c2
        return a1 * a2, a2 * b1 + b2

    _, h = lax.associative_scan(combine, (a, b), axis=1)
    return h


def mlstm_chunkwise(q, k, v, i_pre, logf):
    B, H, S, d = q.shape
    nc = S // CHUNK
    to_c = lambda t: jnp.moveaxis(t.reshape(B, H, nc, CHUNK, *t.shape[3:]), 2, 0)
    qc, kc, vc, ic, fc = to_c(q), to_c(k), to_c(v), to_c(i_pre), to_c(logf)
    causal = jnp.tril(jnp.ones((CHUNK, CHUNK), dtype=bool))

    def step(carry, inp):
        C, n, m = carry
        q_, k_, v_, i_, f_ = inp
        bcum = jnp.cumsum(f_, axis=-1)
        dmat = bcum[..., :, None] - bcum[..., None, :] + i_[..., None, :]
        dmat = jnp.where(causal, dmat, -jnp.inf)
        inter = bcum + m[..., None]
        m_j = jnp.maximum(inter, jnp.max(dmat, axis=-1))
        w = jnp.exp(dmat - m_j[..., None])
        s_qk = jnp.einsum("bhjd,bhsd->bhjs", q_, k_) * w
        e_inter = jnp.exp(inter - m_j)
        num = jnp.einsum("bhjs,bhse->bhje", s_qk, v_) + \
            e_inter[..., None] * jnp.einsum("bhjd,bhde->bhje", q_, C)
        den = jnp.sum(s_qk, axis=-1) + e_inter * jnp.einsum("bhjd,bhd->bhj", q_, n)
        h = num / jnp.maximum(jnp.abs(den), jnp.exp(-m_j))[..., None]
        b_last = bcum[..., -1]
        g = b_last[..., None] - bcum + i_
        m_new = jnp.maximum(b_last + m, jnp.max(g, axis=-1))
        decay = jnp.exp(b_last + m - m_new)
        wg = jnp.exp(g - m_new[..., None])
        C_new = decay[..., None, None] * C + jnp.einsum("bhs,bhsd,bhse->bhde", wg, k_, v_)
        n_new = decay[..., None] * n + jnp.einsum("bhs,bhsd->bhd", wg, k_)
        return (C_new, n_new, m_new), h

    init = (jnp.zeros((B, H, d, d), jnp.float32), jnp.zeros((B, H, d), jnp.float32),
            jnp.zeros((B, H), jnp.float32))
    _, hs = lax.scan(step, init, (qc, kc, vc, ic, fc))
    return jnp.moveaxis(hs, 0, 2).reshape(B, H, S, d)


def hybrid_layer(x, g_pre, w_in, rg_conv_w, rg_conv_b, rg_w_a, rg_b_a, rg_w_x, rg_b_x, rg_lambda,
                 ml_conv_w, ml_conv_b, ml_w_q, ml_w_k, ml_w_v, ml_b_i, ml_b_f, ml_norm_g, ml_skip,
                 w_branch_r, w_branch_m, w_out, g_post):
    dt = x.dtype
    B, S, _ = x.shape
    xn = rms_norm(x, g_pre)
    proj = jnp.einsum("bsd,de->bse", xn, w_in)
    offs = [0]
    for sz in IN_SIZES:
        offs.append(offs[-1] + sz)
    r_x, r_z, m_x, m_z, m_o, m_i, m_f, gate_r, gate_m = [proj[..., offs[j]:offs[j + 1]] for j in range(len(IN_SIZES))]

    u = causal_dwconv(r_x, rg_conv_w, rg_conv_b)
    h_r = rg_lru(u, headwise(u, rg_w_a) + rg_b_a, headwise(u, rg_w_x) + rg_b_x, rg_lambda)
    y_r = (h_r * jax.nn.silu(r_z.astype(jnp.float32))).astype(dt)

    c = jax.nn.silu(causal_dwconv(m_x, ml_conv_w, ml_conv_b))
    heads = lambda t: t.reshape(B, S, H_M, DH_M).transpose(0, 2, 1, 3).astype(jnp.float32)
    q = heads(headwise(c, ml_w_q))
    k = heads(headwise(c, ml_w_k)) * (1.0 / math.sqrt(DH_M))
    v = heads(headwise(m_x, ml_w_v))
    i_pre = (m_i + ml_b_i).astype(jnp.float32).transpose(0, 2, 1)
    logf = jax.nn.log_sigmoid((m_f + ml_b_f).astype(jnp.float32)).transpose(0, 2, 1)
    h_m = mlstm_chunkwise(q, k, v, i_pre, logf).transpose(0, 2, 1, 3)
    h_m = jax.nn.sigmoid(m_o.astype(jnp.float32)).reshape(B, S, H_M, DH_M) * h_m
    mu = jnp.mean(h_m, axis=-1, keepdims=True)
    var = jnp.mean(jnp.square(h_m - mu), axis=-1, keepdims=True)
    h_m = (h_m - mu) * lax.rsqrt(var + EPS) * ml_norm_g.astype(jnp.float32).reshape(H_M, DH_M)
    h_m = h_m.reshape(B, S, D_M) + ml_skip.astype(jnp.float32) * c.astype(jnp.float32)
    y_m = (h_m * jax.nn.silu(m_z.astype(jnp.float32))).astype(dt)

    y = jax.nn.sigmoid(gate_r) * jnp.einsum("bsr,rd->bsd", y_r, w_branch_r) + \
        jax.nn.sigmoid(gate_m) * jnp.einsum("bsm,md->bsd", y_m, w_branch_m)
    out = jnp.einsum("bsd,de->bse", y, w_out)
    return x + rms_norm(out, g_post)


def setup_inputs(seed: int = 0) -> dict:
    key = jax.random.key(seed)
    ks = jax.random.split(key, 24)
    L = DEPTH
    nrm = lambda k, shape, scale: jax.random.normal(k, shape, jnp.float32) * scale
    u = jax.random.uniform(ks[9], (L, D_RNN), jnp.float32, minval=0.9, maxval=0.999)
    s = u ** (1.0 / RG_C)
    rg_lambda = jnp.log(s) - jnp.log1p(-s)
    ml_b_f = jnp.broadcast_to(jnp.linspace(3.0, 6.0, H_M, dtype=jnp.float32), (L, H_M)) + nrm(ks[16], (L, H_M), 0.01)
    return {
        "x": nrm(ks[0], (BATCH, SEQ, D_MODEL), 1.0),
        "g_pre": 1.0 + nrm(ks[1], (L, D_MODEL), 0.05),
        "w_in": nrm(ks[2], (L, D_MODEL, D_IN), D_MODEL ** -0.5),
        "rg_conv_w": nrm(ks[3], (L, CONV_W, D_RNN), CONV_W ** -0.5),
        "rg_conv_b": nrm(ks[4], (L, D_RNN), 0.02),
        "rg_w_a": nrm(ks[5], (L, H_R, DH_R, DH_R), DH_R ** -0.5),
        "rg_b_a": nrm(ks[6], (L, D_RNN), 0.02),
        "rg_w_x": nrm(ks[7], (L, H_R, DH_R, DH_R), DH_R ** -0.5),
        "rg_b_x": nrm(ks[8], (L, D_RNN), 0.02),
        "rg_lambda": rg_lambda,
        "ml_conv_w": nrm(ks[10], (L, CONV_W, D_M), CONV_W ** -0.5),
        "ml_conv_b": nrm(ks[11], (L, D_M), 0.02),
        "ml_w_q": nrm(ks[12], (L, H_M, DH_M, DH_M), DH_M ** -0.5),
        "ml_w_k": nrm(ks[13], (L, H_M, DH_M, DH_M), DH_M ** -0.5),
        "ml_w_v": nrm(ks[14], (L, H_M, DH_M, DH_M), DH_M ** -0.5),
        "ml_b_i": nrm(ks[15], (L, H_M), 0.1),
        "ml_b_f": ml_b_f,
        "ml_norm_g": 1.0 + nrm(ks[17], (L, D_M), 0.05),
        "ml_skip": 1.0 + nrm(ks[18], (L, D_M), 0.05),
        "w_branch_r": nrm(ks[19], (L, D_RNN, D_MODEL), D_RNN ** -0.5),
        "w_branch_m": nrm(ks[20], (L, D_M, D_MODEL), D_M ** -0.5),
        "w_out": nrm(ks[21], (L, D_MODEL, D_MODEL), D_MODEL ** -0.5),
        "g_post": 1.0 + nrm(ks[22], (L, D_MODEL), 0.05),
    }


def reference(x, g_pre, w_in, rg_conv_w, rg_conv_b, rg_w_a, rg_b_a, rg_w_x, rg_b_x, rg_lambda,
              ml_conv_w, ml_conv_b, ml_w_q, ml_w_k, ml_w_v, ml_b_i, ml_b_f, ml_norm_g, ml_skip,
              w_branch_r, w_branch_m, w_out, g_post):
    h = x
    for l in range(DEPTH):
        h = hybrid_layer(h, g_pre[l], w_in[l], rg_conv_w[l], rg_conv_b[l], rg_w_a[l], rg_b_a[l],
                         rg_w_x[l], rg_b_x[l], rg_lambda[l], ml_conv_w[l], ml_conv_b[l],
                         ml_w_q[l], ml_w_k[l], ml_w_v[l], ml_b_i[l], ml_b_f[l], ml_norm_g[l],
                         ml_skip[l], w_branch_r[l], w_branch_m[l], w_out[l], g_post[l])
    return h
```

```python
import functools
import math

import jax
import jax.numpy as jnp
from jax import lax
from jax.experimental import pallas as pl
from jax.experimental.pallas import tpu as pltpu

D_MODEL = 1024
H_R = 16
DH_R = D_MODEL // H_R
RG_C = 8.0
CONV_W = 4
H_M = 4
DH_M = D_MODEL // H_M
CHUNK = 128
EPS = 1e-6

TILE_T = 256
COL_G = 256
N_G = D_MODEL // COL_G
SUBLANES = 8
LANES = 128
VMEM_LIMIT_BYTES = 56 * 1024 * 1024

OFF_RX, OFF_RZ, OFF_MX, OFF_MZ, OFF_MO, OFF_GR, OFF_GM, OFF_IF = (
    0, 1024, 2048, 3072, 4096, 5120, 6144, 7168)
W_COLS = OFF_IF + LANES

(P_GPRE, P_RCW, P_RCB, P_RBA, P_RBX, P_LAM, P_MCW, P_MCB, P_NORMG, P_SKIP, P_GPOST, P_BIF) = (
    0, 1, 5, 6, 7, 8, 9, 13, 14, 15, 16, 17)
P_ROWS = 24

F32 = jnp.float32
BF16 = jnp.bfloat16


def _dot(a, b):
    return jnp.dot(a, b, preferred_element_type=F32)


def _silu(x):
    return x * jax.nn.sigmoid(x)


def _linear_scan(a, b, h0):
    tn, c = a.shape
    ng = tn // SUBLANES
    a3 = a.reshape(ng, SUBLANES, c)
    b3 = b.reshape(ng, SUBLANES, c)
    sub = lax.broadcasted_iota(jnp.int32, a3.shape, 1)
    s = 1
    while s < SUBLANES:
        a_sh = pltpu.roll(a3, s, 1)
        b_sh = pltpu.roll(b3, s, 1)
        keep = sub >= s
        b3 = jnp.where(keep, a3 * b_sh + b3, b3)
        a3 = jnp.where(keep, a3 * a_sh, a3)
        s *= 2
    hs = []
    h = h0
    for g in range(ng):
        hg = a3[g] * h + b3[g]
        hs.append(hg)
        h = hg[SUBLANES - 1:SUBLANES, :]
    return jnp.concatenate(hs, axis=0), h


def _causal_conv(buf_ref, cs, y, w_rows, bias):
    tn = y.shape[0]
    buf_ref[SUBLANES:SUBLANES + tn, cs] = y
    out = bias + w_rows[CONV_W - 1] * y
    for j in range(CONV_W - 1):
        shift = CONV_W - 1 - j
        out = out + w_rows[j] * buf_ref[SUBLANES - shift:SUBLANES - shift + tn, cs]
    buf_ref[0:SUBLANES, cs] = buf_ref[tn:tn + SUBLANES, cs]
    return out


def _block_kernel(x_ref, w_ref, wa_ref, wx_ref, wq_ref, wk_ref, wv_ref, wbr_ref, wbm_ref, wout_ref,
                  p_ref, o_ref,
                  xn_ref, rxb_ref, mxb_ref, yr_ref, c_ref, q_ref, k_ref, v_ref, z_ref, hm_ref,
                  ym_ref, y_ref, hr_ref, cst_ref, nst_ref, mst_ref):
    tn = x_ref.shape[0]

    @pl.when(pl.program_id(1) == 0)
    def _reset_state():
        rxb_ref[0:SUBLANES, :] = jnp.zeros((SUBLANES, D_MODEL), F32)
        mxb_ref[0:SUBLANES, :] = jnp.zeros((SUBLANES, D_MODEL), F32)
        hr_ref[...] = jnp.zeros_like(hr_ref)
        cst_ref[...] = jnp.zeros_like(cst_ref)
        nst_ref[...] = jnp.zeros_like(nst_ref)
        mst_ref[...] = jnp.zeros_like(mst_ref)

    def prow(r, cs):
        return p_ref[r:r + 1, cs]

    x = x_ref[...]
    ms = jnp.mean(x * x, axis=-1, keepdims=True)
    xn_ref[...] = (x * lax.rsqrt(ms + EPS) * p_ref[P_GPRE:P_GPRE + 1, :]).astype(BF16)
    xn = xn_ref[...]

    for g in range(N_G):
        cs = slice(g * COL_G, (g + 1) * COL_G)
        rx = _dot(xn, w_ref[:, OFF_RX + g * COL_G:OFF_RX + (g + 1) * COL_G])
        u = _causal_conv(rxb_ref, cs, rx, [prow(P_RCW + j, cs) for j in range(CONV_W)], prow(P_RCB, cs))
        ub = u.astype(BF16)
        r = jax.nn.sigmoid(_dot(ub, wa_ref[g]) + prow(P_RBA, cs))
        i = jax.nn.sigmoid(_dot(ub, wx_ref[g]) + prow(P_RBX, cs))
        log_a = (-RG_C * jax.nn.softplus(-prow(P_LAM, cs))) * r
        s_half = jnp.tanh(0.5 * log_a)
        inv = 1.0 / (1.0 - s_half)
        a = (1.0 + s_half) * inv
        b = (2.0 * jnp.sqrt(-s_half) * inv) * (i * u)
        h, h_last = _linear_scan(a, b, hr_ref[0:1, cs])
        hr_ref[0:1, cs] = h_last
        rz = _dot(xn, w_ref[:, OFF_RZ + g * COL_G:OFF_RZ + (g + 1) * COL_G])
        yr_ref[:, cs] = (h * _silu(rz)).astype(BF16)

    for hd in range(H_M):
        cs = slice(hd * DH_M, (hd + 1) * DH_M)
        mx = _dot(xn, w_ref[:, OFF_MX + hd * DH_M:OFF_MX + (hd + 1) * DH_M])
        c = _silu(_causal_conv(mxb_ref, cs, mx, [prow(P_MCW + j, cs) for j in range(CONV_W)],
                               prow(P_MCB, cs)))
        c_ref[:, cs] = c
        cb = c.astype(BF16)
        q_ref[:, cs] = _dot(cb, wq_ref[hd]).astype(BF16)
        k_ref[:, cs] = (_dot(cb, wk_ref[hd]) * (1.0 / math.sqrt(DH_M))).astype(BF16)
        v_ref[:, cs] = _dot(mx.astype(BF16), wv_ref[hd]).astype(BF16)

    pre_if = _dot(xn, w_ref[:, OFF_IF:OFF_IF + LANES]) + p_ref[P_BIF:P_BIF + 1, 0:LANES]
    lane = lax.broadcasted_iota(jnp.int32, pre_if.shape, 1)
    z_ref[...] = jnp.where(lane < H_M, pre_if, jax.nn.log_sigmoid(pre_if))

    row = lax.broadcasted_iota(jnp.int32, (CHUNK, CHUNK), 0)
    col = lax.broadcasted_iota(jnp.int32, (CHUNK, CHUNK), 1)
    causal = row >= col
    tril = causal.astype(F32)
    for ck in range(tn // CHUNK):
        rows = slice(ck * CHUNK, (ck + 1) * CHUNK)
        z = z_ref[rows, :]
        bc = jnp.dot(tril, z, precision=lax.Precision.HIGHEST, preferred_element_type=F32)
        zt = z.T
        bct = bc.T
        m_row = mst_ref[...]
        zi = pltpu.roll(z, H_M, 1)
        b_last = bc[CHUNK - 1:CHUNK, :]
        g_all = b_last - bc + zi
        m_new = jnp.maximum(b_last + m_row, jnp.max(g_all, axis=0, keepdims=True))
        decay_all = jnp.exp(b_last + m_row - m_new)
        wg_all = jnp.exp(g_all - m_new)
        inter_all = bc + m_row
        for hd in range(H_M):
            cs = slice(hd * DH_M, (hd + 1) * DH_M)
            q = q_ref[rows, cs]
            k = k_ref[rows, cs]
            v = v_ref[rows, cs]
            bcol = bc[:, H_M + hd:H_M + hd + 1]
            brow = bct[H_M + hd:H_M + hd + 1, :]
            irow = zt[hd:hd + 1, :]
            dmat = jnp.where(causal, bcol - brow + irow, -jnp.inf)
            inter = inter_all[:, H_M + hd:H_M + hd + 1]
            m_j = jnp.maximum(inter, jnp.max(dmat, axis=-1, keepdims=True))
            w = jnp.exp(dmat - m_j)
            s_qk = lax.dot_general(q, k, (((1,), (1,)), ((), ())), preferred_element_type=F32) * w
            e_inter = jnp.exp(inter - m_j)
            c_state = cst_ref[hd]
            n_state = nst_ref[hd:hd + 1, :]
            num = _dot(s_qk.astype(BF16), v) + e_inter * _dot(q, c_state.astype(BF16))
            den = jnp.sum(s_qk, axis=-1, keepdims=True) + \
                e_inter * jnp.sum(q.astype(F32) * n_state, axis=-1, keepdims=True)
            hm_ref[rows, cs] = num / jnp.maximum(jnp.abs(den), jnp.exp(-m_j))
            wg = wg_all[:, H_M + hd:H_M + hd + 1]
            decay = decay_all[:, H_M + hd:H_M + hd + 1]
            kw = k.astype(F32) * wg
            cst_ref[hd] = decay * c_state + lax.dot_general(
                kw.astype(BF16), v, (((0,), (0,)), ((), ())), preferred_element_type=F32)
            nst_ref[hd:hd + 1, :] = decay * n_state + jnp.sum(kw, axis=0, keepdims=True)
        mst_ref[...] = m_new

    for hd in range(H_M):
        cs = slice(hd * DH_M, (hd + 1) * DH_M)
        o_gate = jax.nn.sigmoid(_dot(xn, w_ref[:, OFF_MO + hd * DH_M:OFF_MO + (hd + 1) * DH_M]))
        hg = o_gate * hm_ref[:, cs]
        mu = jnp.mean(hg, axis=-1, keepdims=True)
        dlt = hg - mu
        var = jnp.mean(dlt * dlt, axis=-1, keepdims=True)
        hn = dlt * lax.rsqrt(var + EPS) * prow(P_NORMG, cs) + prow(P_SKIP, cs) * c_ref[:, cs]
        mz = _dot(xn, w_ref[:, OFF_MZ + hd * DH_M:OFF_MZ + (hd + 1) * DH_M])
        ym_ref[:, cs] = (hn * _silu(mz)).astype(BF16)

    yr = yr_ref[...]
    ym = ym_ref[...]
    for g in range(N_G):
        cs = slice(g * COL_G, (g + 1) * COL_G)
        gate_r = jax.nn.sigmoid(_dot(xn, w_ref[:, OFF_GR + g * COL_G:OFF_GR + (g + 1) * COL_G]))
        gate_m = jax.nn.sigmoid(_dot(xn, w_ref[:, OFF_GM + g * COL_G:OFF_GM + (g + 1) * COL_G]))
        y = gate_r * _dot(yr, wbr_ref[:, cs]) + gate_m * _dot(ym, wbm_ref[:, cs])
        y_ref[:, cs] = y.astype(BF16)
    out = _dot(y_ref[...], wout_ref[...])
    ms_o = jnp.mean(out * out, axis=-1, keepdims=True)
    o_ref[...] = x_ref[...] + out * lax.rsqrt(ms_o + EPS) * p_ref[P_GPOST:P_GPOST + 1, :]


def _block_diag(w, group):
    h, d, e = w.shape
    w = w.reshape(h // group, group, d, e)
    eye = jnp.eye(group, dtype=w.dtype)
    return jnp.einsum("gide,ij->gidje", w, eye).reshape(h // group, group * d, group * e)


def _layer(x, g_pre, w_in, rg_conv_w, rg_conv_b, rg_w_a, rg_b_a, rg_w_x, rg_b_x, rg_lambda,
           ml_conv_w, ml_conv_b, ml_w_q, ml_w_k, ml_w_v, ml_b_i, ml_b_f, ml_norm_g, ml_skip,
           w_branch_r, w_branch_m, w_out, g_post):
    bsz, seq, d = x.shape
    assert d == D_MODEL and seq % TILE_T == 0 and TILE_T % CHUNK == 0

    n_if = 2 * H_M
    o_i = 5 * D_MODEL
    w_if = jnp.pad(w_in[:, o_i:o_i + n_if], ((0, 0), (0, LANES - n_if)))
    w_pack = jnp.concatenate([w_in[:, :o_i], w_in[:, o_i + n_if:], w_if], axis=1).astype(BF16)
    assert w_pack.shape == (D_MODEL, W_COLS)

    group = COL_G // DH_R
    wa = _block_diag(rg_w_a, group).astype(BF16)
    wx = _block_diag(rg_w_x, group).astype(BF16)

    b_if = jnp.pad(jnp.concatenate([ml_b_i, ml_b_f]), (0, D_MODEL - n_if))
    rows = [g_pre[None], rg_conv_w, rg_conv_b[None], rg_b_a[None], rg_b_x[None], rg_lambda[None],
            ml_conv_w, ml_conv_b[None], ml_norm_g[None], ml_skip[None], g_post[None], b_if[None]]
    params = jnp.concatenate(rows, axis=0).astype(F32)
    params = jnp.pad(params, ((0, P_ROWS - params.shape[0]), (0, 0)))

    resident = pl.BlockSpec(memory_space=pltpu.VMEM)
    tile = pl.BlockSpec((None, TILE_T, D_MODEL), lambda b, t: (b, t, 0))
    slab_f32 = pltpu.VMEM((TILE_T, D_MODEL), F32)
    slab_bf16 = pltpu.VMEM((TILE_T, D_MODEL), BF16)
    conv_buf = pltpu.VMEM((TILE_T + SUBLANES, D_MODEL), F32)
    scratch = [
        slab_bf16,
        conv_buf, conv_buf,
        slab_bf16,
        slab_f32,
        slab_bf16, slab_bf16, slab_bf16,
        pltpu.VMEM((TILE_T, LANES), F32),
        slab_f32,
        slab_bf16,
        slab_bf16,
        pltpu.VMEM((SUBLANES, D_MODEL), F32),
        pltpu.VMEM((H_M, DH_M, DH_M), F32),
        pltpu.VMEM((SUBLANES, DH_M), F32),
        pltpu.VMEM((1, LANES), F32),
    ]
    return pl.pallas_call(
        _block_kernel,
        out_shape=jax.ShapeDtypeStruct(x.shape, x.dtype),
        grid=(bsz, seq // TILE_T),
        in_specs=[tile] + [resident] * 10,
        out_specs=tile,
        scratch_shapes=scratch,
        compiler_params=pltpu.CompilerParams(
            dimension_semantics=("arbitrary", "arbitrary"),
            vmem_limit_bytes=VMEM_LIMIT_BYTES),
        name="hybrid_block",
    )(x, w_pack, wa, wx, ml_w_q.astype(BF16), ml_w_k.astype(BF16), ml_w_v.astype(BF16),
      w_branch_r.astype(BF16), w_branch_m.astype(BF16), w_out.astype(BF16), params)


def kernel(x, g_pre, w_in, rg_conv_w, rg_conv_b, rg_w_a, rg_b_a, rg_w_x, rg_b_x, rg_lambda,
           ml_conv_w, ml_conv_b, ml_w_q, ml_w_k, ml_w_v, ml_b_i, ml_b_f, ml_norm_g, ml_skip,
           w_branch_r, w_branch_m, w_out, g_post):
    h = x
    for l in range(g_pre.shape[0]):
        h = _layer(h, g_pre[l], w_in[l], rg_conv_w[l], rg_conv_b[l], rg_w_a[l], rg_b_a[l],
                   rg_w_x[l], rg_b_x[l], rg_lambda[l], ml_conv_w[l], ml_conv_b[l],
                   ml_w_q[l], ml_w_k[l], ml_w_v[l], ml_b_i[l], ml_b_f[l], ml_norm_g[l],
                   ml_skip[l], w_branch_r[l], w_branch_m[l], w_out[l], g_post[l])
    return h
```

```python
import functools
import math

import jax
import jax.numpy as jnp
from jax import lax
from jax.experimental import pallas as pl
from jax.experimental.pallas import tpu as pltpu

D_MODEL = 1024
H_R = 16
DH_R = D_MODEL // H_R
RG_C = 8.0
CONV_W = 4
H_M = 4
DH_M = D_MODEL // H_M
CHUNK = 128
EPS = 1e-6

TILE_T = 256
COL_G = 256
N_G = D_MODEL // COL_G
SUBLANES = 8
LANES = 128
VMEM_LIMIT_BYTES = 56 * 1024 * 1024

OFF_RX, OFF_RZ, OFF_MX, OFF_MZ, OFF_MO, OFF_GR, OFF_GM, OFF_IF = (
    0, 1024, 2048, 3072, 4096, 5120, 6144, 7168)
W_COLS = OFF_IF + LANES

(P_GPRE, P_RCW, P_RCB, P_RBA, P_RBX, P_LAM, P_MCW, P_MCB, P_NORMG, P_SKIP, P_GPOST, P_BIF) = (
    0, 1, 5, 6, 7, 8, 9, 13, 14, 15, 16, 17)
P_ROWS = 24

F32 = jnp.float32
BF16 = jnp.bfloat16


def _dot(a, b):
    return jnp.dot(a, b, preferred_element_type=F32)


def _silu(x):
    return x * jax.nn.sigmoid(x)


def _linear_scan(a, b, h0):
    tn, c = a.shape
    ng = tn // SUBLANES
    a3 = a.reshape(ng, SUBLANES, c)
    b3 = b.reshape(ng, SUBLANES, c)
    sub = lax.broadcasted_iota(jnp.int32, a3.shape, 1)
    s = 1
    while s < SUBLANES:
        a_sh = pltpu.roll(a3, s, 1)
        b_sh = pltpu.roll(b3, s, 1)
        keep = sub >= s
        b3 = jnp.where(keep, a3 * b_sh + b3, b3)
        a3 = jnp.where(keep, a3 * a_sh, a3)
        s *= 2
    hs = []
    h = h0
    for g in range(ng):
        hg = a3[g] * h + b3[g]
        hs.append(hg)
        h = hg[SUBLANES - 1:SUBLANES, :]
    return jnp.concatenate(hs, axis=0), h


def _causal_conv(buf_ref, cs, y, w_rows, bias):
    tn = y.shape[0]
    buf_ref[SUBLANES:SUBLANES + tn, cs] = y
    out = bias + w_rows[CONV_W - 1] * y
    for j in range(CONV_W - 1):
        shift = CONV_W - 1 - j
        out = out + w_rows[j] * buf_ref[SUBLANES - shift:SUBLANES - shift + tn, cs]
    buf_ref[0:SUBLANES, cs] = buf_ref[tn:tn + SUBLANES, cs]
    return out


def _block_kernel(x_ref, w_ref, wa_ref, wx_ref, wq_ref, wk_ref, wv_ref, wbr_ref, wbm_ref, wout_ref,
                  p_ref, o_ref,
                  xn_ref, rxb_ref, mxb_ref, yr_ref, c_ref, q_ref, k_ref, v_ref, hm_ref,
                  ym_ref, y_ref, gr_ref, gm_ref, mo_ref, mz_ref,
                  hr_ref, cst_ref, nst_ref, mst_ref):
    tn = x_ref.shape[0]

    @pl.when(pl.program_id(1) == 0)
    def _reset_state():
        rxb_ref[0:SUBLANES, :] = jnp.zeros((SUBLANES, D_MODEL), F32)
        mxb_ref[0:SUBLANES, :] = jnp.zeros((SUBLANES, D_MODEL), F32)
        hr_ref[...] = jnp.zeros_like(hr_ref)
        cst_ref[...] = jnp.zeros_like(cst_ref)
        nst_ref[...] = jnp.zeros_like(nst_ref)
        mst_ref[...] = jnp.zeros_like(mst_ref)

    def prow(r, cs):
        return p_ref[r:r + 1, cs]

    x = x_ref[...]
    ms = jnp.mean(x * x, axis=-1, keepdims=True)
    xn_ref[...] = (x * lax.rsqrt(ms + EPS) * p_ref[P_GPRE:P_GPRE + 1, :]).astype(BF16)
    xn = xn_ref[...]

    fillers = []
    for g in range(N_G):
        for off, dst in ((OFF_GR, gr_ref), (OFF_GM, gm_ref), (OFF_MO, mo_ref), (OFF_MZ, mz_ref)):
            fillers.append((off + g * COL_G, dst, g * COL_G))
    fillers.reverse()

    def fill():
        if fillers:
            c0, dst, d0 = fillers.pop()
            dst[:, d0:d0 + COL_G] = _dot(xn, w_ref[:, c0:c0 + COL_G])

    rg = [dict() for _ in range(N_G)]

    def rg_conv(g):
        cs = slice(g * COL_G, (g + 1) * COL_G)
        rx = _dot(xn, w_ref[:, OFF_RX + g * COL_G:OFF_RX + (g + 1) * COL_G])
        rg[g]["u"] = _causal_conv(rxb_ref, cs, rx, [prow(P_RCW + j, cs) for j in range(CONV_W)],
                                  prow(P_RCB, cs))

    def rg_gates(g):
        cs = slice(g * COL_G, (g + 1) * COL_G)
        u = rg[g]["u"]
        ub = u.astype(BF16)
        r = jax.nn.sigmoid(_dot(ub, wa_ref[g]) + prow(P_RBA, cs))
        i = jax.nn.sigmoid(_dot(ub, wx_ref[g]) + prow(P_RBX, cs))
        log_a = (-RG_C * jax.nn.softplus(-prow(P_LAM, cs))) * r
        s_half = jnp.tanh(0.5 * log_a)
        inv = 1.0 / (1.0 - s_half)
        rg[g]["a"] = (1.0 + s_half) * inv
        rg[g]["b"] = (2.0 * jnp.sqrt(-s_half) * inv) * (i * u)

    def rg_scan(g):
        cs = slice(g * COL_G, (g + 1) * COL_G)
        h, h_last = _linear_scan(rg[g]["a"], rg[g]["b"], hr_ref[0:1, cs])
        hr_ref[0:1, cs] = h_last
        rg[g]["h"] = h

    def rg_out(g):
        cs = slice(g * COL_G, (g + 1) * COL_G)
        rz = _dot(xn, w_ref[:, OFF_RZ + g * COL_G:OFF_RZ + (g + 1) * COL_G])
        yr_ref[:, cs] = (rg[g]["h"] * _silu(rz)).astype(BF16)
        rg[g].clear()

    for hd in range(H_M):
        cs = slice(hd * DH_M, (hd + 1) * DH_M)
        mx = _dot(xn, w_ref[:, OFF_MX + hd * DH_M:OFF_MX + (hd + 1) * DH_M])
        fill()
        c = _silu(_causal_conv(mxb_ref, cs, mx, [prow(P_MCW + j, cs) for j in range(CONV_W)],
                               prow(P_MCB, cs)))
        c_ref[:, cs] = c
        cb = c.astype(BF16)
        q_ref[:, cs] = _dot(cb, wq_ref[hd]).astype(BF16)
        k_ref[:, cs] = (_dot(cb, wk_ref[hd]) * (1.0 / math.sqrt(DH_M))).astype(BF16)
        v_ref[:, cs] = _dot(mx.astype(BF16), wv_ref[hd]).astype(BF16)
        fill()

    pre_if = _dot(xn, w_ref[:, OFF_IF:OFF_IF + LANES]) + p_ref[P_BIF:P_BIF + 1, 0:LANES]
    lane = lax.broadcasted_iota(jnp.int32, pre_if.shape, 1)
    z_all = jnp.where(lane < H_M, pre_if, jax.nn.log_sigmoid(pre_if))
    row = lax.broadcasted_iota(jnp.int32, (CHUNK, CHUNK), 0)
    col = lax.broadcasted_iota(jnp.int32, (CHUNK, CHUNK), 1)
    causal = row >= col
    tril = causal.astype(F32)
    n_grp = CHUNK // SUBLANES
    sub = lax.broadcasted_iota(jnp.int32, (n_grp, SUBLANES, LANES), 1)
    m_row = mst_ref[...]
    gates = []
    for ck in range(tn // CHUNK):
        z = z_all[ck * CHUNK:(ck + 1) * CHUNK, :]
        bc = jnp.dot(tril, z, precision=lax.Precision.HIGHEST, preferred_element_type=F32)
        r = pltpu.roll(z, H_M, 1) - bc
        cm = r.reshape(n_grp, SUBLANES, LANES)
        s = 1
        while s < SUBLANES:
            cm = jnp.where(sub >= s, jnp.maximum(cm, pltpu.roll(cm, s, 1)), cm)
            s *= 2
        s = 1
        while s < n_grp:
            cm = jnp.concatenate([cm[:s], jnp.maximum(cm[s:], cm[:-s])], axis=0)
            s *= 2
        cm = cm.reshape(CHUNK, LANES)
        mm = jnp.maximum(m_row, cm)
        mm_last = mm[CHUNK - 1:CHUNK, :]
        gates.append(dict(
            rt=r.T,
            mm=mm,
            e_inter=jnp.exp(m_row - mm),
            e_negm=jnp.exp(-(bc + mm)),
            wg=jnp.exp(r - mm_last),
            decay=jnp.exp(m_row - mm_last)))
        m_row = bc[CHUNK - 1:CHUNK, :] + mm_last
    mst_ref[...] = m_row

    def chunk(hd, ck):
        cs = slice(hd * DH_M, (hd + 1) * DH_M)
        rows = slice(ck * CHUNK, (ck + 1) * CHUNK)
        ln = slice(H_M + hd, H_M + hd + 1)
        gt = gates[ck]
        q = q_ref[rows, cs]
        k = k_ref[rows, cs]
        v = v_ref[rows, cs]
        w = jnp.where(causal, jnp.exp(gt["rt"][ln, :] - gt["mm"][:, ln]), 0.0)
        s_qk = lax.dot_general(q, k, (((1,), (1,)), ((), ())), preferred_element_type=F32) * w
        e_inter = gt["e_inter"][:, ln]
        c_state = cst_ref[hd]
        n_state = nst_ref[hd:hd + 1, :]
        num = _dot(s_qk.astype(BF16), v) + e_inter * _dot(q, c_state.astype(BF16))
        den = jnp.sum(s_qk, axis=-1, keepdims=True) + \
            e_inter * jnp.sum(q.astype(F32) * n_state, axis=-1, keepdims=True)
        hm_ref[rows, cs] = num * (1.0 / jnp.maximum(jnp.abs(den), gt["e_negm"][:, ln]))
        decay = gt["decay"][:, ln]
        kw = k.astype(F32) * gt["wg"][:, ln]
        cst_ref[hd] = decay * c_state + lax.dot_general(
            kw.astype(BF16), v, (((0,), (0,)), ((), ())), preferred_element_type=F32)
        nst_ref[hd:hd + 1, :] = decay * n_state + jnp.sum(kw, axis=0, keepdims=True)

    def epilogue(hd):
        cs = slice(hd * DH_M, (hd + 1) * DH_M)
        hg = jax.nn.sigmoid(mo_ref[:, cs]) * hm_ref[:, cs]
        mu = jnp.mean(hg, axis=-1, keepdims=True)
        dlt = hg - mu
        var = jnp.mean(dlt * dlt, axis=-1, keepdims=True)
        hn = dlt * lax.rsqrt(var + EPS) * prow(P_NORMG, cs) + prow(P_SKIP, cs) * c_ref[:, cs]
        ym_ref[:, cs] = (hn * _silu(mz_ref[:, cs])).astype(BF16)

    for hd in range(H_M):
        chunk(hd, 0)
        rg_conv(hd)
        fill()
        rg_gates(hd)
        for ck in range(1, tn // CHUNK):
            chunk(hd, ck)
        fill()
        rg_scan(hd)
        rg_out(hd)
        epilogue(hd)
    while fillers:
        fill()

    yr = yr_ref[...]
    ym = ym_ref[...]
    for g in range(N_G):
        cs = slice(g * COL_G, (g + 1) * COL_G)
        y = jax.nn.sigmoid(gr_ref[:, cs]) * _dot(yr, wbr_ref[:, cs]) + \
            jax.nn.sigmoid(gm_ref[:, cs]) * _dot(ym, wbm_ref[:, cs])
        y_ref[:, cs] = y.astype(BF16)
    out = _dot(y_ref[...], wout_ref[...])
    ms_o = jnp.mean(out * out, axis=-1, keepdims=True)
    o_ref[...] = x_ref[...] + out * lax.rsqrt(ms_o + EPS) * p_ref[P_GPOST:P_GPOST + 1, :]


def _block_diag(w, group):
    h, d, e = w.shape
    w = w.reshape(h // group, group, d, e)
    eye = jnp.eye(group, dtype=w.dtype)
    return jnp.einsum("gide,ij->gidje", w, eye).reshape(h // group, group * d, group * e)


def _layer(x, g_pre, w_in, rg_conv_w, rg_conv_b, rg_w_a, rg_b_a, rg_w_x, rg_b_x, rg_lambda,
           ml_conv_w, ml_conv_b, ml_w_q, ml_w_k, ml_w_v, ml_b_i, ml_b_f, ml_norm_g, ml_skip,
           w_branch_r, w_branch_m, w_out, g_post):
    bsz, seq, d = x.shape
    assert d == D_MODEL and seq % TILE_T == 0 and TILE_T % CHUNK == 0

    n_if = 2 * H_M
    o_i = 5 * D_MODEL
    w_if = jnp.pad(w_in[:, o_i:o_i + n_if], ((0, 0), (0, LANES - n_if)))
    w_pack = jnp.concatenate([w_in[:, :o_i], w_in[:, o_i + n_if:], w_if], axis=1).astype(BF16)
    assert w_pack.shape == (D_MODEL, W_COLS)

    group = COL_G // DH_R
    wa = _block_diag(rg_w_a, group).astype(BF16)
    wx = _block_diag(rg_w_x, group).astype(BF16)

    b_if = jnp.pad(jnp.concatenate([ml_b_i, ml_b_f]), (0, D_MODEL - n_if))
    rows = [g_pre[None], rg_conv_w, rg_conv_b[None], rg_b_a[None], rg_b_x[None], rg_lambda[None],
            ml_conv_w, ml_conv_b[None], ml_norm_g[None], ml_skip[None], g_post[None], b_if[None]]
    params = jnp.concatenate(rows, axis=0).astype(F32)
    params = jnp.pad(params, ((0, P_ROWS - params.shape[0]), (0, 0)))

    resident = pl.BlockSpec(memory_space=pltpu.VMEM)
    tile = pl.BlockSpec((None, TILE_T, D_MODEL), lambda b, t: (b, t, 0))
    slab_f32 = pltpu.VMEM((TILE_T, D_MODEL), F32)
    slab_bf16 = pltpu.VMEM((TILE_T, D_MODEL), BF16)
    conv_buf = pltpu.VMEM((TILE_T + SUBLANES, D_MODEL), F32)
    scratch = [
        slab_bf16,
        conv_buf, conv_buf,
        slab_bf16,
        slab_f32,
        slab_bf16, slab_bf16, slab_bf16,
        slab_f32,
        slab_bf16,
        slab_bf16,
        slab_f32, slab_f32, slab_f32, slab_f32,
        pltpu.VMEM((SUBLANES, D_MODEL), F32),
        pltpu.VMEM((H_M, DH_M, DH_M), F32),
        pltpu.VMEM((SUBLANES, DH_M), F32),
        pltpu.VMEM((1, LANES), F32),
    ]
    return pl.pallas_call(
        _block_kernel,
        out_shape=jax.ShapeDtypeStruct(x.shape, x.dtype),
        grid=(bsz, seq // TILE_T),
        in_specs=[tile] + [resident] * 10,
        out_specs=tile,
        scratch_shapes=scratch,
        compiler_params=pltpu.CompilerParams(
            dimension_semantics=("arbitrary", "arbitrary"),
            vmem_limit_bytes=VMEM_LIMIT_BYTES),
        name="hybrid_block",
    )(x, w_pack, wa, wx, ml_w_q.astype(BF16), ml_w_k.astype(BF16), ml_w_v.astype(BF16),
      w_branch_r.astype(BF16), w_branch_m.astype(BF16), w_out.astype(BF16), params)


def kernel(x, g_pre, w_in, rg_conv_w, rg_conv_b, rg_w_a, rg_b_a, rg_w_x, rg_b_x, rg_lambda,
           ml_conv_w, ml_conv_b, ml_w_q, ml_w_k, ml_w_v, ml_b_i, ml_b_f, ml_norm_g, ml_skip,
           w_branch_r, w_branch_m, w_out, g_post):
    h = x
    for l in range(g_pre.shape[0]):
        h = _layer(h, g_pre[l], w_in[l], rg_conv_w[l], rg_conv_b[l], rg_w_a[l], rg_b_a[l],
                   rg_w_x[l], rg_b_x[l], rg_lambda[l], ml_conv_w[l], ml_conv_b[l],
                   ml_w_q[l], ml_w_k[l], ml_w_v[l], ml_b_i[l], ml_b_f[l], ml_norm_g[l],
                   ml_skip[l], w_branch_r[l], w_branch_m[l], w_out[l], g_post[l])
    return h
```

```python
import functools
import math

import jax
import jax.numpy as jnp
import numpy as np
from jax import lax
from jax.experimental import pallas as pl
from jax.experimental.pallas import tpu as pltpu

D_MODEL = 1024
H_R = 16
DH_R = D_MODEL // H_R
RG_C = 8.0
CONV_W = 4
H_M = 4
DH_M = D_MODEL // H_M
CHUNK = 128
EPS = 1e-6

TILE_T = 256
COL_G = 256
N_G = D_MODEL // COL_G
SUBLANES = 8
LANES = 128
SEG = CHUNK // SUBLANES
VMEM_LIMIT_BYTES = 56 * 1024 * 1024

PAIR = 2 * COL_G
OFF_RG, OFF_MM, OFF_FA, OFF_GM, OFF_IF = 0, 2048, 4096, 6144, 7168
W_COLS = OFF_IF + LANES

(P_GPRE, P_RCW, P_RCB, P_RBA, P_RBX, P_LAM, P_MCW, P_MCB, P_NORMG, P_SKIP, P_GPOST, P_BIF) = (
    0, 1, 5, 6, 7, 8, 9, 13, 14, 15, 16, 17)
P_ROWS = 24

F32 = jnp.float32
BF16 = jnp.bfloat16


def _dot(a, b):
    return jnp.dot(a, b, preferred_element_type=F32)


def _silu(x):
    return x * jax.nn.sigmoid(x)


def _step_deps():
    n_ck = TILE_T // CHUNK
    d = {"pre": ()}
    for g in range(N_G):
        d[f"fill{g}"] = ("pre",)
        d[f"rg_dot{g}"] = ("pre",)
        d[f"rg_conv{g}"] = (f"rg_dot{g}",)
        d[f"rg_gdot{g}"] = (f"rg_conv{g}",)
        d[f"rg_gates{g}"] = (f"rg_gdot{g}",)
        d[f"rg_scan{g}"] = (f"rg_gates{g}",)
        d[f"rg_out{g}"] = (f"rg_scan{g}",)
        d[f"mm_dot{g}"] = ("pre",)
        d[f"mm_conv{g}"] = (f"mm_dot{g}",)
        d[f"mm_qkv{g}"] = (f"mm_conv{g}",)
        for ck in range(n_ck):
            d[f"ck_a{g}{ck}"] = (f"mm_qkv{g}",) + ((f"ck_f{g}{ck - 1}",) if ck else ())
            d[f"ck_b{g}{ck}"] = (f"ck_a{g}{ck}", "gv_vec")
            d[f"ck_c{g}{ck}"] = (f"ck_b{g}{ck}",)
            d[f"ck_d{g}{ck}"] = (f"ck_c{g}{ck}",) + ((f"ck_d{g}{ck - 1}",) if ck else ())
            d[f"ck_e{g}{ck}"] = (f"ck_d{g}{ck}",)
            d[f"ck_f{g}{ck}"] = (f"ck_e{g}{ck}",)
        d[f"epi{g}"] = (f"ck_d{g}{n_ck - 1}", f"mm_dot{g}", f"mm_conv{g}", f"fill{g}")
    for j in range(N_G // 2):
        d[f"fill{N_G + j}"] = ("pre",)
        d[f"br{j}"] = tuple(f"rg_out{g}" for g in range(N_G))
        d[f"bm{j}"] = tuple(f"epi{g}" for g in range(N_G))
        d[f"merge{j}"] = (f"br{j}", f"bm{j}", f"fill{2 * j}", f"fill{2 * j + 1}", f"fill{N_G + j}")
    d["gv_dot"] = ("pre",)
    d["gv_z"] = ("gv_dot",)
    d["gv_cum"] = ("gv_z",)
    d["gv_vec"] = ("gv_cum",)
    d["unperm"] = tuple(f"merge{j}" for j in range(N_G // 2))
    d["out"] = ("unperm",)
    d["post"] = ("out",)
    return d


STEP_DEPS = _step_deps()


def _issue_order():
    n_ck = TILE_T // CHUNK
    o = ["pre", "mm_dot0", "gv_dot", "mm_dot1", "mm_conv0"]
    gate_vec = {1: "gv_z", 2: "gv_cum"}
    for hd in range(1, H_M):
        o += [f"mm_dot{hd + 1}" if hd + 1 < H_M else "rg_dot0", f"mm_qkv{hd - 1}"]
        o += [gate_vec[hd]] if hd == 1 else []
        o += [f"mm_conv{hd}"]
        o += [gate_vec[hd]] if hd == 2 else []
    o += [f"mm_qkv{H_M - 1}", "gv_vec"]
    fills = [f"fill{i}" for i in range(N_G + N_G // 2)]
    for hd in range(H_M):
        ck = lambda c, s: f"ck_{s}{hd}{c}"
        o += [ck(0, "a")] + ([f"rg_dot{hd + 1}"] if hd + 1 < H_M else [])
        o += [f"rg_conv{hd}", ck(0, "b"), f"rg_gdot{hd}", ck(0, "c"), fills.pop(0)]
        o += [ck(0, "d"), f"rg_gates{hd}", ck(0, "e"), ck(0, "f")]
        rg_tail = [f"rg_scan{hd}", f"rg_out{hd}"]
        for c in range(1, n_ck):
            o += [ck(c, "a"), ck(c, "b")] + rg_tail[:1] + [ck(c, "c"), ck(c, "d")] + rg_tail[1:]
            o += [ck(c, "e"), ck(c, "f")]
            rg_tail = []
        o += rg_tail
        if hd % 2 == 1:
            o.append(fills.pop(0))
        if hd == H_M - 1:
            o += [f"br{j}" for j in range(N_G // 2)]
        o.append(f"epi{hd}")
    o += [f"bm{j}" for j in range(N_G // 2)] + [f"merge{j}" for j in range(N_G // 2)]
    o += ["unperm", "out", "post"]
    return o


ORDER = _issue_order()


def _linear_scan(a, b, h0):
    tn, c = a.shape
    a3 = a.reshape(tn // SUBLANES, SUBLANES, c)
    b3 = b.reshape(tn // SUBLANES, SUBLANES, c)
    sub = lax.broadcasted_iota(jnp.int32, (SUBLANES, c), 0)
    hs = []
    carry = h0
    for ck in range(tn // CHUNK):
        base = ck * SEG
        h = b3[base]
        p = a3[base]
        h_loc, p_loc = [h], [p]
        for r in range(1, SEG):
            h = a3[base + r] * h + b3[base + r]
            p = a3[base + r] * p
            h_loc.append(h)
            p_loc.append(p)
        s = 1
        while s < SUBLANES:
            keep = sub >= s
            h = jnp.where(keep, p * pltpu.roll(h, s, 0) + h, h)
            p = jnp.where(keep, p * pltpu.roll(p, s, 0), p)
            s *= 2
        end_state = p * carry + h
        seg_in = jnp.where(sub >= 1, pltpu.roll(end_state, 1, 0), carry)
        for r in range(SEG):
            hs.append(h_loc[r] + p_loc[r] * seg_in)
        carry = end_state[SUBLANES - 1:SUBLANES, :]
    return jnp.stack(hs, axis=0).reshape(tn, c), carry


def _causal_conv(tail_ref, cs, y, w_rows, bias):
    tn, c = y.shape
    n_tail = CONV_W - 1
    y3 = y.reshape(tn // SUBLANES, SUBLANES, c)
    first = lax.broadcasted_iota(jnp.int32, (n_tail, SUBLANES, c), 1) == 0
    prev_last = tail_ref[:, cs].reshape(n_tail, SUBLANES, c)
    outs = []
    for ck in range(tn // CHUNK):
        cur = y3[ck * SEG:(ck + 1) * SEG]
        wrapped = jnp.where(first, pltpu.roll(prev_last, 1, 1), pltpu.roll(cur[SEG - n_tail:], 1, 1))
        acc = bias + w_rows[CONV_W - 1] * cur
        for shift in range(1, CONV_W):
            shifted = jnp.concatenate([wrapped[n_tail - shift:], cur[:SEG - shift]], axis=0)
            acc = acc + w_rows[CONV_W - 1 - shift] * shifted
        outs.append(acc)
        prev_last = cur[SEG - n_tail:]
    tail_ref[:, cs] = prev_last.reshape(n_tail * SUBLANES, c)
    return jnp.concatenate(outs, axis=0).reshape(tn, c)


def _block_kernel(x_ref, w_ref, wax_ref, wqk_ref, wv_ref, wbr_ref, wbm_ref, wout_ref,
                  p_ref, perm_ref, unperm_ref, causal_ref, o_ref,
                  xn_ref, rxb_ref, mxb_ref, yr_ref, c_ref, q_ref, k_ref, v_ref, hm_ref,
                  ym_ref, y_ref, gr_ref, gm_ref, mo_ref, mz_ref,
                  hr_ref, cst_ref, nst_ref, mst_ref):
    tn = x_ref.shape[0]

    @pl.when(pl.program_id(1) == 0)
    def _reset_state():
        rxb_ref[...] = jnp.zeros_like(rxb_ref)
        mxb_ref[...] = jnp.zeros_like(mxb_ref)
        hr_ref[...] = jnp.zeros_like(hr_ref)
        cst_ref[...] = jnp.zeros_like(cst_ref)
        nst_ref[...] = jnp.zeros_like(nst_ref)
        mst_ref[...] = jnp.zeros_like(mst_ref)

    def prow(r, cs):
        return p_ref[r:r + 1, cs]

    n_ck = tn // CHUNK
    val = {}
    steps = {}

    def step(name):
        def register(fn):
            steps[name] = fn
            return fn
        return register

    @step("pre")
    def _():
        x = x_ref[...]
        ms = jnp.mean(x * x, axis=-1, keepdims=True)
        xn_time = (x * lax.rsqrt(ms + EPS) * p_ref[P_GPRE:P_GPRE + 1, :]).astype(BF16)
        xn_ref[...] = _dot(perm_ref[...], xn_time).astype(BF16)

    def fill_mz_gr(g):
        d = _dot(xn_ref[...], w_ref[:, OFF_FA + g * PAIR:OFF_FA + (g + 1) * PAIR])
        mz_ref[:, g * COL_G:(g + 1) * COL_G] = d[:, :COL_G]
        gr_ref[:, g * COL_G:(g + 1) * COL_G] = d[:, COL_G:]

    def fill_gm(j):
        gm_ref[:, j * PAIR:(j + 1) * PAIR] = _dot(xn_ref[...], w_ref[:, OFF_GM + j * PAIR:OFF_GM + (j + 1) * PAIR])

    for g in range(N_G):
        steps[f"fill{g}"] = functools.partial(fill_mz_gr, g)
    for j in range(N_G // 2):
        steps[f"fill{N_G + j}"] = functools.partial(fill_gm, j)

    def rg_dot(g):
        d = _dot(xn_ref[...], w_ref[:, OFF_RG + g * PAIR:OFF_RG + (g + 1) * PAIR])
        val[f"rx{g}"] = d[:, :COL_G]
        val[f"rz{g}"] = d[:, COL_G:]

    def rg_conv(g):
        cs = slice(g * COL_G, (g + 1) * COL_G)
        u = _causal_conv(rxb_ref, cs, val.pop(f"rx{g}"), [prow(P_RCW + j, cs) for j in range(CONV_W)],
                         prow(P_RCB, cs))
        val[f"u{g}"] = u
        val[f"ub{g}"] = u.astype(BF16)

    def rg_gdot(g):
        val[f"gate{g}"] = _dot(val.pop(f"ub{g}"), wax_ref[g])

    def rg_gates(g):
        cs = slice(g * COL_G, (g + 1) * COL_G)
        u = val.pop(f"u{g}")
        gate = val.pop(f"gate{g}")
        r = jax.nn.sigmoid(gate[:, :COL_G] + prow(P_RBA, cs))
        i = jax.nn.sigmoid(gate[:, COL_G:] + prow(P_RBX, cs))
        log_a = (-RG_C * jax.nn.softplus(-prow(P_LAM, cs))) * r
        s_half = jnp.tanh(0.5 * log_a)
        inv = 1.0 / (1.0 - s_half)
        val[f"a{g}"] = (1.0 + s_half) * inv
        val[f"b{g}"] = (2.0 * jnp.sqrt(-s_half) * inv) * (i * u)

    def rg_scan(g):
        cs = slice(g * COL_G, (g + 1) * COL_G)
        h, h_last = _linear_scan(val.pop(f"a{g}"), val.pop(f"b{g}"), hr_ref[0:1, cs])
        hr_ref[0:1, cs] = h_last
        val[f"h{g}"] = h

    def rg_out(g):
        cs = slice(g * COL_G, (g + 1) * COL_G)
        yr_ref[:, cs] = (val.pop(f"h{g}") * _silu(val.pop(f"rz{g}"))).astype(BF16)

    def mm_dot(hd):
        cs = slice(hd * DH_M, (hd + 1) * DH_M)
        d = _dot(xn_ref[...], w_ref[:, OFF_MM + hd * PAIR:OFF_MM + (hd + 1) * PAIR])
        val[f"mx{hd}"] = d[:, :DH_M]
        mo_ref[:, cs] = d[:, DH_M:]

    def mm_conv(hd):
        cs = slice(hd * DH_M, (hd + 1) * DH_M)
        mx = val.pop(f"mx{hd}")
        c = _silu(_causal_conv(mxb_ref, cs, mx, [prow(P_MCW + j, cs) for j in range(CONV_W)],
                               prow(P_MCB, cs)))
        c_ref[:, cs] = c
        val[f"cb{hd}"] = c.astype(BF16)
        val[f"mxb{hd}"] = mx.astype(BF16)

    def mm_qkv(hd):
        cs = slice(hd * DH_M, (hd + 1) * DH_M)
        qk = _dot(val.pop(f"cb{hd}"), wqk_ref[hd])
        q_ref[:, cs] = qk[:, :DH_M].astype(BF16)
        k_ref[:, cs] = qk[:, DH_M:].astype(BF16)
        v_ref[:, cs] = _dot(val.pop(f"mxb{hd}"), wv_ref[hd]).astype(BF16)

    for g in range(N_G):
        for nm, fn in (("rg_dot", rg_dot), ("rg_conv", rg_conv), ("rg_gdot", rg_gdot),
                       ("rg_gates", rg_gates), ("rg_scan", rg_scan), ("rg_out", rg_out),
                       ("mm_dot", mm_dot), ("mm_conv", mm_conv), ("mm_qkv", mm_qkv)):
            steps[f"{nm}{g}"] = functools.partial(fn, g)

    @step("gv_dot")
    def _():
        val["pre_if"] = _dot(xn_ref[...], w_ref[:, OFF_IF:OFF_IF + LANES])

    @step("gv_z")
    def _():
        pre_if = val.pop("pre_if") + p_ref[P_BIF:P_BIF + 1, 0:LANES]
        lane = lax.broadcasted_iota(jnp.int32, pre_if.shape, 1)
        val["z"] = jnp.where(lane < H_M, pre_if, jax.nn.log_sigmoid(pre_if))

    @step("gv_cum")
    def _():
        z = val["z"]
        sub = lax.broadcasted_iota(jnp.int32, (SUBLANES, LANES), 0)
        bcs = []
        for ck in range(n_ck):
            bc = z[ck * CHUNK:(ck + 1) * CHUNK, :].reshape(SEG, SUBLANES, LANES)
            s = 1
            while s < SEG:
                bc = jnp.concatenate([bc[:s], bc[s:] + bc[:-s]], axis=0)
                s *= 2
            seg_sum = bc[SEG - 1]
            s = 1
            while s < SUBLANES:
                seg_sum = jnp.where(sub >= s, seg_sum + pltpu.roll(seg_sum, s, 0), seg_sum)
                s *= 2
            before = jnp.where(sub >= 1, pltpu.roll(seg_sum, 1, 0), 0.0)
            bcs.append((bc + before).reshape(CHUNK, LANES))
        val["bc"] = bcs

    @step("gv_vec")
    def _():
        z_all = val.pop("z")
        bcs = val.pop("bc")
        sub = lax.broadcasted_iota(jnp.int32, (SUBLANES, LANES), 0)
        m_row = mst_ref[...]
        for ck in range(n_ck):
            z = z_all[ck * CHUNK:(ck + 1) * CHUNK, :]
            bc = bcs[ck]
            r = pltpu.roll(z, H_M, 1) - bc
            cm = r.reshape(SEG, SUBLANES, LANES)
            s = 1
            while s < SEG:
                cm = jnp.concatenate([cm[:s], jnp.maximum(cm[s:], cm[:-s])], axis=0)
                s *= 2
            seg_max = cm[SEG - 1]
            s = 1
            while s < SUBLANES:
                seg_max = jnp.where(sub >= s, jnp.maximum(seg_max, pltpu.roll(seg_max, s, 0)), seg_max)
                s *= 2
            before = jnp.where(sub >= 1, pltpu.roll(seg_max, 1, 0), -jnp.inf)
            cm = jnp.maximum(cm, before).reshape(CHUNK, LANES)
            mj = jnp.maximum(m_row, cm)
            mj_last = mj[CHUNK - 1:CHUNK, :]
            val[f"gt{ck}"] = dict(
                rt=r.T,
                mj=mj,
                e_inter=jnp.exp(m_row - mj),
                e_negm=jnp.exp(-(bc + mj)),
                wg=jnp.exp(r - mj_last),
                decay=jnp.exp(m_row - mj_last))
            m_row = bc[CHUNK - 1:CHUNK, :] + mj_last
        mst_ref[...] = m_row

    def ck_a(hd, ck):
        cs = slice(hd * DH_M, (hd + 1) * DH_M)
        rows = slice(ck * CHUNK, (ck + 1) * CHUNK)
        if ck == 0:
            val[f"C{hd}"] = cst_ref[hd]
            val[f"n{hd}"] = nst_ref[hd:hd + 1, :]
        q = q_ref[rows, cs]
        val[f"qk{hd}{ck}"] = lax.dot_general(q, k_ref[rows, cs], (((1,), (1,)), ((), ())),
                                            preferred_element_type=F32)
        val[f"qc{hd}{ck}"] = _dot(q, val[f"C{hd}"].astype(BF16))

    def ck_b(hd, ck):
        ln = slice(H_M + hd, H_M + hd + 1)
        gt = val[f"gt{ck}"]
        causal = causal_ref[...] > 0.5
        w = jnp.where(causal, jnp.exp(gt["rt"][ln, :] - gt["mj"][:, ln]), 0.0)
        s_qk = val.pop(f"qk{hd}{ck}") * w
        val[f"rowsum{hd}{ck}"] = jnp.sum(s_qk, axis=-1, keepdims=True)
        val[f"s{hd}{ck}"] = s_qk.astype(BF16)

    def ck_c(hd, ck):
        cs = slice(hd * DH_M, (hd + 1) * DH_M)
        rows = slice(ck * CHUNK, (ck + 1) * CHUNK)
        val[f"sv{hd}{ck}"] = _dot(val.pop(f"s{hd}{ck}"), v_ref[rows, cs])

    def ck_d(hd, ck):
        cs = slice(hd * DH_M, (hd + 1) * DH_M)
        rows = slice(ck * CHUNK, (ck + 1) * CHUNK)
        ln = slice(H_M + hd, H_M + hd + 1)
        gt = val[f"gt{ck}"]
        e_inter = gt["e_inter"][:, ln]
        num = val.pop(f"sv{hd}{ck}") + e_inter * val.pop(f"qc{hd}{ck}")
        den = val.pop(f"rowsum{hd}{ck}") + \
            e_inter * jnp.sum(q_ref[rows, cs].astype(F32) * val[f"n{hd}"], axis=-1, keepdims=True)
        hm_ref[rows, cs] = num * (1.0 / jnp.maximum(jnp.abs(den), gt["e_negm"][:, ln]))
        kw = k_ref[rows, cs].astype(F32) * gt["wg"][:, ln]
        val[f"n{hd}"] = gt["decay"][:, ln] * val[f"n{hd}"] + jnp.sum(kw, axis=0, keepdims=True)
        val[f"kw{hd}{ck}"] = kw.astype(BF16)

    def ck_e(hd, ck):
        cs = slice(hd * DH_M, (hd + 1) * DH_M)
        rows = slice(ck * CHUNK, (ck + 1) * CHUNK)
        val[f"kv{hd}{ck}"] = lax.dot_general(val.pop(f"kw{hd}{ck}"), v_ref[rows, cs],
                                            (((0,), (0,)), ((), ())), preferred_element_type=F32)

    def ck_f(hd, ck):
        ln = slice(H_M + hd, H_M + hd + 1)
        c_new = val[f"gt{ck}"]["decay"][:, ln] * val.pop(f"C{hd}") + val.pop(f"kv{hd}{ck}")
        if ck + 1 < n_ck:
            val[f"C{hd}"] = c_new
        else:
            cst_ref[hd] = c_new
            nst_ref[hd:hd + 1, :] = val.pop(f"n{hd}")

    for hd in range(H_M):
        for ck in range(n_ck):
            for nm, fn in (("ck_a", ck_a), ("ck_b", ck_b), ("ck_c", ck_c), ("ck_d", ck_d),
                           ("ck_e", ck_e), ("ck_f", ck_f)):
                steps[f"{nm}{hd}{ck}"] = functools.partial(fn, hd, ck)

    def epilogue(hd):
        cs = slice(hd * DH_M, (hd + 1) * DH_M)
        hg = jax.nn.sigmoid(mo_ref[:, cs]) * hm_ref[:, cs]
        mu = jnp.mean(hg, axis=-1, keepdims=True)
        dlt = hg - mu
        var = jnp.mean(dlt * dlt, axis=-1, keepdims=True)
        hn = dlt * lax.rsqrt(var + EPS) * prow(P_NORMG, cs) + prow(P_SKIP, cs) * c_ref[:, cs]
        ym_ref[:, cs] = (hn * _silu(mz_ref[:, cs])).astype(BF16)

    for hd in range(H_M):
        steps[f"epi{hd}"] = functools.partial(epilogue, hd)

    def br(j):
        val[f"br{j}"] = _dot(yr_ref[...], wbr_ref[:, j * PAIR:(j + 1) * PAIR])

    def bm(j):
        val[f"bm{j}"] = _dot(ym_ref[...], wbm_ref[:, j * PAIR:(j + 1) * PAIR])

    def merge(j):
        cs = slice(j * PAIR, (j + 1) * PAIR)
        y = jax.nn.sigmoid(gr_ref[:, cs]) * val.pop(f"br{j}") + \
            jax.nn.sigmoid(gm_ref[:, cs]) * val.pop(f"bm{j}")
        y_ref[:, cs] = y.astype(BF16)

    for j in range(N_G // 2):
        steps[f"br{j}"] = functools.partial(br, j)
        steps[f"bm{j}"] = functools.partial(bm, j)
        steps[f"merge{j}"] = functools.partial(merge, j)

    @step("unperm")
    def _():
        val["y_time"] = _dot(unperm_ref[...], y_ref[...]).astype(BF16)

    @step("out")
    def _():
        val["out"] = _dot(val.pop("y_time"), wout_ref[...])

    @step("post")
    def _():
        out = val.pop("out")
        ms_o = jnp.mean(out * out, axis=-1, keepdims=True)
        o_ref[...] = x_ref[...] + out * lax.rsqrt(ms_o + EPS) * p_ref[P_GPOST:P_GPOST + 1, :]

    done = set()
    for name in ORDER:
        assert all(d in done for d in STEP_DEPS[name]), (name, STEP_DEPS[name])
        steps[name]()
        done.add(name)
    assert done == set(steps), set(steps) - done


def _block_diag(w, group):
    h, d, e = w.shape
    w = w.reshape(h // group, group, d, e)
    eye = jnp.eye(group, dtype=w.dtype)
    return jnp.einsum("gide,ij->gidje", w, eye).reshape(h // group, group * d, group * e)


def _row_order_constants():
    p = np.arange(TILE_T)
    q = p % CHUNK
    time_of = (p // CHUNK) * CHUNK + (q % SUBLANES) * SEG + q // SUBLANES
    perm = np.zeros((TILE_T, TILE_T), np.float32)
    perm[p, time_of] = 1.0
    t = time_of[:CHUNK]
    causal_p = (t[None, :] <= t[:, None]).astype(np.float32)
    return jnp.asarray(perm, BF16), jnp.asarray(causal_p, F32)


def _layer(x, g_pre, w_in, rg_conv_w, rg_conv_b, rg_w_a, rg_b_a, rg_w_x, rg_b_x, rg_lambda,
           ml_conv_w, ml_conv_b, ml_w_q, ml_w_k, ml_w_v, ml_b_i, ml_b_f, ml_norm_g, ml_skip,
           w_branch_r, w_branch_m, w_out, g_post):
    bsz, seq, d = x.shape
    assert d == D_MODEL and seq % TILE_T == 0 and TILE_T % CHUNK == 0

    n_if = 2 * H_M
    o_i = 5 * D_MODEL
    w_if = jnp.pad(w_in[:, o_i:o_i + n_if], ((0, 0), (0, LANES - n_if)))
    sec = lambda j: w_in[:, j * D_MODEL + (n_if if j >= 5 else 0):][:, :D_MODEL]
    r_x, r_z, m_x, m_z, m_o, gate_r, gate_m = (sec(j) for j in range(7))

    def side_by_side(a, b):
        both = jnp.stack([a.reshape(D_MODEL, N_G, COL_G), b.reshape(D_MODEL, N_G, COL_G)], axis=2)
        return both.reshape(D_MODEL, 2 * D_MODEL)

    w_pack = jnp.concatenate([side_by_side(r_x, r_z), side_by_side(m_x, m_o), side_by_side(m_z, gate_r),
                              gate_m, w_if], axis=1).astype(BF16)
    assert w_pack.shape == (D_MODEL, W_COLS)

    group = COL_G // DH_R
    wax = jnp.concatenate([_block_diag(rg_w_a, group), _block_diag(rg_w_x, group)], axis=2).astype(BF16)
    wqk = jnp.concatenate([ml_w_q, ml_w_k * (1.0 / math.sqrt(DH_M))], axis=2).astype(BF16)

    b_if = jnp.pad(jnp.concatenate([ml_b_i, ml_b_f]), (0, D_MODEL - n_if))
    rows = [g_pre[None], rg_conv_w, rg_conv_b[None], rg_b_a[None], rg_b_x[None], rg_lambda[None],
            ml_conv_w, ml_conv_b[None], ml_norm_g[None], ml_skip[None], g_post[None], b_if[None]]
    params = jnp.concatenate(rows, axis=0).astype(F32)
    params = jnp.pad(params, ((0, P_ROWS - params.shape[0]), (0, 0)))

    perm, causal_p = _row_order_constants()

    resident = pl.BlockSpec(memory_space=pltpu.VMEM)
    tile = pl.BlockSpec((None, TILE_T, D_MODEL), lambda b, t: (b, t, 0))
    slab_f32 = pltpu.VMEM((TILE_T, D_MODEL), F32)
    slab_bf16 = pltpu.VMEM((TILE_T, D_MODEL), BF16)
    conv_tail = pltpu.VMEM(((CONV_W - 1) * SUBLANES, D_MODEL), F32)
    scratch = [
        slab_bf16,
        conv_tail, conv_tail,
        slab_bf16,
        slab_f32,
        slab_bf16, slab_bf16, slab_bf16,
        slab_f32,
        slab_bf16,
        slab_bf16,
        slab_f32, slab_f32, slab_f32, slab_f32,
        pltpu.VMEM((SUBLANES, D_MODEL), F32),
        pltpu.VMEM((H_M, DH_M, DH_M), F32),
        pltpu.VMEM((SUBLANES, DH_M), F32),
        pltpu.VMEM((1, LANES), F32),
    ]
    return pl.pallas_call(
        _block_kernel,
        out_shape=jax.ShapeDtypeStruct(x.shape, x.dtype),
        grid=(bsz, seq // TILE_T),
        in_specs=[tile] + [resident] * 11,
        out_specs=tile,
        scratch_shapes=scratch,
        compiler_params=pltpu.CompilerParams(
            dimension_semantics=("arbitrary", "arbitrary"),
            vmem_limit_bytes=VMEM_LIMIT_BYTES),
        name="hybrid_block",
    )(x, w_pack, wax, wqk, ml_w_v.astype(BF16),
      w_branch_r.astype(BF16), w_branch_m.astype(BF16), w_out.astype(BF16), params,
      perm, perm.T, causal_p)


def kernel(x, g_pre, w_in, rg_conv_w, rg_conv_b, rg_w_a, rg_b_a, rg_w_x, rg_b_x, rg_lambda,
           ml_conv_w, ml_conv_b, ml_w_q, ml_w_k, ml_w_v, ml_b_i, ml_b_f, ml_norm_g, ml_skip,
           w_branch_r, w_branch_m, w_out, g_post):
    h = x
    for l in range(g_pre.shape[0]):
        h = _layer(h, g_pre[l], w_in[l], rg_conv_w[l], rg_conv_b[l], rg_w_a[l], rg_b_a[l],
                   rg_w_x[l], rg_b_x[l], rg_lambda[l], ml_conv_w[l], ml_conv_b[l],
                   ml_w_q[l], ml_w_k[l], ml_w_v[l], ml_b_i[l], ml_b_f[l], ml_norm_g[l],
                   ml_skip[l], w_branch_r[l], w_branch_m[l], w_out[l], g_post[l])
    return h
```

```python
import functools
import math

import jax
import jax.numpy as jnp
import numpy as np
from jax import lax
from jax.experimental import pallas as pl
from jax.experimental.pallas import tpu as pltpu

D_MODEL = 1024
H_R = 16
DH_R = D_MODEL // H_R
RG_C = 8.0
CONV_W = 4
H_M = 4
DH_M = D_MODEL // H_M
CHUNK = 128
EPS = 1e-6

TILE_T = 256
COL_G = 256
N_G = D_MODEL // COL_G
SUBLANES = 8
LANES = 128
SEG = CHUNK // SUBLANES
VMEM_LIMIT_BYTES = 56 * 1024 * 1024

PAIR = 2 * COL_G
OFF_RX, OFF_RZ, OFF_MX, OFF_MZ, OFF_MO = (j * D_MODEL for j in range(5))
OFF_GR, OFF_GM = 0, D_MODEL

(P_GPRE, P_RCW, P_RCB, P_RBA, P_RBX, P_LAM, P_MCW, P_MCB, P_NORMG, P_SKIP, P_GPOST, P_BIF) = (
    0, 1, 5, 6, 7, 8, 9, 13, 14, 15, 16, 17)
P_ROWS = 24

F32 = jnp.float32
BF16 = jnp.bfloat16


def _dot(a, b):
    return jnp.dot(a, b, preferred_element_type=F32)


def _silu(x):
    return x * jax.nn.sigmoid(x)


def _step_deps():
    n_ck = TILE_T // CHUNK
    d = {"pre": ()}
    for g in range(N_G):
        d[f"fill{g}"] = ("pre",)
        d[f"rg_dot{g}"] = ("pre",)
        d[f"rg_conv{g}"] = (f"rg_dot{g}",)
        d[f"rg_gdot{g}"] = (f"rg_conv{g}",)
        d[f"rg_gates{g}"] = (f"rg_gdot{g}",)
        d[f"rg_scan{g}"] = (f"rg_gates{g}",)
        d[f"rg_out{g}"] = (f"rg_scan{g}",)
        d[f"mm_dot{g}"] = ("pre",)
        d[f"mm_conv{g}"] = (f"mm_dot{g}",)
        d[f"mm_qkv{g}"] = (f"mm_conv{g}",)
        for ck in range(n_ck):
            d[f"ck_a{g}{ck}"] = (f"mm_qkv{g}",) + ((f"ck_f{g}{ck - 1}",) if ck else ())
            d[f"ck_b{g}{ck}"] = (f"ck_a{g}{ck}", "gv_vec")
            d[f"ck_c{g}{ck}"] = (f"ck_b{g}{ck}",)
            d[f"ck_d{g}{ck}"] = (f"ck_c{g}{ck}",) + ((f"ck_d{g}{ck - 1}",) if ck else ())
            d[f"ck_e{g}{ck}"] = (f"ck_d{g}{ck}",)
            d[f"ck_f{g}{ck}"] = (f"ck_e{g}{ck}",)
        d[f"epi{g}"] = (f"ck_d{g}{n_ck - 1}", f"mm_dot{g}", f"mm_conv{g}", f"fill{g}")
    for j in range(N_G // 2):
        d[f"fill{N_G + j}"] = ("pre",)
        d[f"br{j}"] = tuple(f"rg_out{g}" for g in range(N_G))
        d[f"bm{j}"] = tuple(f"epi{g}" for g in range(N_G))
        d[f"merge{j}"] = (f"br{j}", f"bm{j}", f"fill{2 * j}", f"fill{2 * j + 1}", f"fill{N_G + j}")
    d["gv_dot"] = ("pre",)
    d["gv_z"] = ("gv_dot",)
    d["gv_cum"] = ("gv_z",)
    d["gv_vec"] = ("gv_cum",)
    d["unperm"] = tuple(f"merge{j}" for j in range(N_G // 2))
    d["out"] = ("unperm",)
    d["post"] = ("out",)
    return d


STEP_DEPS = _step_deps()


def _issue_order():
    n_ck = TILE_T // CHUNK
    o = ["pre", "mm_dot0", "gv_dot", "mm_dot1", "mm_conv0"]
    gate_vec = {1: "gv_z", 2: "gv_cum"}
    for hd in range(1, H_M):
        o += [f"mm_dot{hd + 1}" if hd + 1 < H_M else "rg_dot0", f"mm_qkv{hd - 1}"]
        o += [gate_vec[hd]] if hd == 1 else []
        o += [f"mm_conv{hd}"]
        o += [gate_vec[hd]] if hd == 2 else []
    o += [f"mm_qkv{H_M - 1}", "gv_vec"]
    fills = [f"fill{i}" for i in range(N_G + N_G // 2)]
    for hd in range(H_M):
        ck = lambda c, s: f"ck_{s}{hd}{c}"
        o += [ck(0, "a")] + ([f"rg_dot{hd + 1}"] if hd + 1 < H_M else [])
        o += [f"rg_conv{hd}", ck(0, "b"), f"rg_gdot{hd}", ck(0, "c"), fills.pop(0)]
        o += [ck(0, "d"), f"rg_gates{hd}", ck(0, "e"), ck(0, "f")]
        rg_tail = [f"rg_scan{hd}", f"rg_out{hd}"]
        for c in range(1, n_ck):
            o += [ck(c, "a"), ck(c, "b")] + rg_tail[:1] + [ck(c, "c"), ck(c, "d")] + rg_tail[1:]
            o += [ck(c, "e"), ck(c, "f")]
            rg_tail = []
        o += rg_tail
        if hd % 2 == 1:
            o.append(fills.pop(0))
        if hd == H_M - 1:
            o += [f"br{j}" for j in range(N_G // 2)]
        o.append(f"epi{hd}")
    o += [f"bm{j}" for j in range(N_G // 2)] + [f"merge{j}" for j in range(N_G // 2)]
    o += ["unperm", "out", "post"]
    return o


ORDER = _issue_order()


def _linear_scan(a, b, h0):
    tn, c = a.shape
    a3 = a.reshape(tn // SUBLANES, SUBLANES, c)
    b3 = b.reshape(tn // SUBLANES, SUBLANES, c)
    sub = lax.broadcasted_iota(jnp.int32, (SUBLANES, c), 0)
    hs = []
    carry = h0
    for ck in range(tn // CHUNK):
        base = ck * SEG
        h = b3[base]
        p = a3[base]
        h_loc, p_loc = [h], [p]
        for r in range(1, SEG):
            h = a3[base + r] * h + b3[base + r]
            p = a3[base + r] * p
            h_loc.append(h)
            p_loc.append(p)
        s = 1
        while s < SUBLANES:
            keep = sub >= s
            h = jnp.where(keep, p * pltpu.roll(h, s, 0) + h, h)
            p = jnp.where(keep, p * pltpu.roll(p, s, 0), p)
            s *= 2
        end_state = p * carry + h
        seg_in = jnp.where(sub >= 1, pltpu.roll(end_state, 1, 0), carry)
        for r in range(SEG):
            hs.append(h_loc[r] + p_loc[r] * seg_in)
        carry = end_state[SUBLANES - 1:SUBLANES, :]
    return jnp.stack(hs, axis=0).reshape(tn, c), carry


def _causal_conv(tail_ref, cs, y, w_rows, bias):
    tn, c = y.shape
    n_tail = CONV_W - 1
    y3 = y.reshape(tn // SUBLANES, SUBLANES, c)
    first = lax.broadcasted_iota(jnp.int32, (n_tail, SUBLANES, c), 1) == 0
    prev_last = tail_ref[:, cs].reshape(n_tail, SUBLANES, c)
    outs = []
    for ck in range(tn // CHUNK):
        cur = y3[ck * SEG:(ck + 1) * SEG]
        wrapped = jnp.where(first, pltpu.roll(prev_last, 1, 1), pltpu.roll(cur[SEG - n_tail:], 1, 1))
        acc = bias + w_rows[CONV_W - 1] * cur
        for shift in range(1, CONV_W):
            shifted = jnp.concatenate([wrapped[n_tail - shift:], cur[:SEG - shift]], axis=0)
            acc = acc + w_rows[CONV_W - 1 - shift] * shifted
        outs.append(acc)
        prev_last = cur[SEG - n_tail:]
    tail_ref[:, cs] = prev_last.reshape(n_tail * SUBLANES, c)
    return jnp.concatenate(outs, axis=0).reshape(tn, c)


def _block_kernel(x_ref, w_ref, wg_ref, wif_ref, wax_ref, wqk_ref, wv_ref, wbr_ref, wbm_ref, wout_ref,
                  p_ref, perm_ref, unperm_ref, causal_ref, o_ref,
                  xn_ref, rxb_ref, mxb_ref, yr_ref, c_ref, q_ref, k_ref, v_ref, hm_ref,
                  ym_ref, y_ref, gr_ref, gm_ref, mo_ref, mz_ref,
                  hr_ref, cst_ref, nst_ref, mst_ref):
    tn = x_ref.shape[0]

    @pl.when(pl.program_id(1) == 0)
    def _reset_state():
        rxb_ref[...] = jnp.zeros_like(rxb_ref)
        mxb_ref[...] = jnp.zeros_like(mxb_ref)
        hr_ref[...] = jnp.zeros_like(hr_ref)
        cst_ref[...] = jnp.zeros_like(cst_ref)
        nst_ref[...] = jnp.zeros_like(nst_ref)
        mst_ref[...] = jnp.zeros_like(mst_ref)

    def prow(r, cs):
        return p_ref[r:r + 1, cs]

    n_ck = tn // CHUNK
    val = {}
    steps = {}

    def step(name):
        def register(fn):
            steps[name] = fn
            return fn
        return register

    @step("pre")
    def _():
        x = x_ref[...]
        ms = jnp.mean(x * x, axis=-1, keepdims=True)
        xn_time = (x * lax.rsqrt(ms + EPS) * p_ref[P_GPRE:P_GPRE + 1, :]).astype(BF16)
        xn_ref[...] = _dot(perm_ref[...], xn_time).astype(BF16)

    def fill_mz_gr(g):
        cs = slice(g * COL_G, (g + 1) * COL_G)
        xn = xn_ref[...]
        mz_ref[:, cs] = _dot(xn, w_ref[:, OFF_MZ + g * COL_G:OFF_MZ + (g + 1) * COL_G])
        gr_ref[:, cs] = _dot(xn, wg_ref[:, OFF_GR + g * COL_G:OFF_GR + (g + 1) * COL_G])

    def fill_gm(j):
        gm_ref[:, j * PAIR:(j + 1) * PAIR] = _dot(xn_ref[...], wg_ref[:, OFF_GM + j * PAIR:OFF_GM + (j + 1) * PAIR])

    for g in range(N_G):
        steps[f"fill{g}"] = functools.partial(fill_mz_gr, g)
    for j in range(N_G // 2):
        steps[f"fill{N_G + j}"] = functools.partial(fill_gm, j)

    def rg_dot(g):
        xn = xn_ref[...]
        val[f"rx{g}"] = _dot(xn, w_ref[:, OFF_RX + g * COL_G:OFF_RX + (g + 1) * COL_G])
        val[f"rz{g}"] = _dot(xn, w_ref[:, OFF_RZ + g * COL_G:OFF_RZ + (g + 1) * COL_G])

    def rg_conv(g):
        cs = slice(g * COL_G, (g + 1) * COL_G)
        u = _causal_conv(rxb_ref, cs, val.pop(f"rx{g}"), [prow(P_RCW + j, cs) for j in range(CONV_W)],
                         prow(P_RCB, cs))
        val[f"u{g}"] = u
        val[f"ub{g}"] = u.astype(BF16)

    def rg_gdot(g):
        val[f"gate{g}"] = _dot(val.pop(f"ub{g}"), wax_ref[g])

    def rg_gates(g):
        cs = slice(g * COL_G, (g + 1) * COL_G)
        u = val.pop(f"u{g}")
        gate = val.pop(f"gate{g}")
        r = jax.nn.sigmoid(gate[:, :COL_G] + prow(P_RBA, cs))
        i = jax.nn.sigmoid(gate[:, COL_G:] + prow(P_RBX, cs))
        log_a = (-RG_C * jax.nn.softplus(-prow(P_LAM, cs))) * r
        s_half = jnp.tanh(0.5 * log_a)
        inv = 1.0 / (1.0 - s_half)
        val[f"a{g}"] = (1.0 + s_half) * inv
        val[f"b{g}"] = (2.0 * jnp.sqrt(-s_half) * inv) * (i * u)

    def rg_scan(g):
        cs = slice(g * COL_G, (g + 1) * COL_G)
        h, h_last = _linear_scan(val.pop(f"a{g}"), val.pop(f"b{g}"), hr_ref[0:1, cs])
        hr_ref[0:1, cs] = h_last
        val[f"h{g}"] = h

    def rg_out(g):
        cs = slice(g * COL_G, (g + 1) * COL_G)
        yr_ref[:, cs] = (val.pop(f"h{g}") * _silu(val.pop(f"rz{g}"))).astype(BF16)

    def mm_dot(hd):
        cs = slice(hd * DH_M, (hd + 1) * DH_M)
        xn = xn_ref[...]
        val[f"mx{hd}"] = _dot(xn, w_ref[:, OFF_MX + hd * DH_M:OFF_MX + (hd + 1) * DH_M])
        mo_ref[:, cs] = _dot(xn, w_ref[:, OFF_MO + hd * DH_M:OFF_MO + (hd + 1) * DH_M])

    def mm_conv(hd):
        cs = slice(hd * DH_M, (hd + 1) * DH_M)
        mx = val.pop(f"mx{hd}")
        c = _silu(_causal_conv(mxb_ref, cs, mx, [prow(P_MCW + j, cs) for j in range(CONV_W)],
                               prow(P_MCB, cs)))
        c_ref[:, cs] = c
        val[f"cb{hd}"] = c.astype(BF16)
        val[f"mxb{hd}"] = mx.astype(BF16)

    def mm_qkv(hd):
        cs = slice(hd * DH_M, (hd + 1) * DH_M)
        qk = _dot(val.pop(f"cb{hd}"), wqk_ref[hd])
        q_ref[:, cs] = qk[:, :DH_M].astype(BF16)
        k_ref[:, cs] = qk[:, DH_M:].astype(BF16)
        v_ref[:, cs] = _dot(val.pop(f"mxb{hd}"), wv_ref[hd]).astype(BF16)

    for g in range(N_G):
        for nm, fn in (("rg_dot", rg_dot), ("rg_conv", rg_conv), ("rg_gdot", rg_gdot),
                       ("rg_gates", rg_gates), ("rg_scan", rg_scan), ("rg_out", rg_out),
                       ("mm_dot", mm_dot), ("mm_conv", mm_conv), ("mm_qkv", mm_qkv)):
            steps[f"{nm}{g}"] = functools.partial(fn, g)

    @step("gv_dot")
    def _():
        val["pre_if"] = _dot(xn_ref[...], wif_ref[...])

    @step("gv_z")
    def _():
        pre_if = val.pop("pre_if") + p_ref[P_BIF:P_BIF + 1, 0:LANES]
        lane = lax.broadcasted_iota(jnp.int32, pre_if.shape, 1)
        val["z"] = jnp.where(lane < H_M, pre_if, jax.nn.log_sigmoid(pre_if))

    @step("gv_cum")
    def _():
        z = val["z"]
        sub = lax.broadcasted_iota(jnp.int32, (SUBLANES, LANES), 0)
        bcs = []
        for ck in range(n_ck):
            bc = z[ck * CHUNK:(ck + 1) * CHUNK, :].reshape(SEG, SUBLANES, LANES)
            s = 1
            while s < SEG:
                bc = jnp.concatenate([bc[:s], bc[s:] + bc[:-s]], axis=0)
                s *= 2
            seg_sum = bc[SEG - 1]
            s = 1
            while s < SUBLANES:
                seg_sum = jnp.where(sub >= s, seg_sum + pltpu.roll(seg_sum, s, 0), seg_sum)
                s *= 2
            before = jnp.where(sub >= 1, pltpu.roll(seg_sum, 1, 0), 0.0)
            bcs.append((bc + before).reshape(CHUNK, LANES))
        val["bc"] = bcs

    @step("gv_vec")
    def _():
        z_all = val.pop("z")
        bcs = val.pop("bc")
        sub = lax.broadcasted_iota(jnp.int32, (SUBLANES, LANES), 0)
        m_row = mst_ref[...]
        for ck in range(n_ck):
            z = z_all[ck * CHUNK:(ck + 1) * CHUNK, :]
            bc = bcs[ck]
            r = pltpu.roll(z, H_M, 1) - bc
            cm = r.reshape(SEG, SUBLANES, LANES)
            s = 1
            while s < SEG:
                cm = jnp.concatenate([cm[:s], jnp.maximum(cm[s:], cm[:-s])], axis=0)
                s *= 2
            seg_max = cm[SEG - 1]
            s = 1
            while s < SUBLANES:
                seg_max = jnp.where(sub >= s, jnp.maximum(seg_max, pltpu.roll(seg_max, s, 0)), seg_max)
                s *= 2
            before = jnp.where(sub >= 1, pltpu.roll(seg_max, 1, 0), -jnp.inf)
            cm = jnp.maximum(cm, before).reshape(CHUNK, LANES)
            mj = jnp.maximum(m_row, cm)
            mj_last = mj[CHUNK - 1:CHUNK, :]
            val[f"gt{ck}"] = dict(
                rt=r.T,
                mj=mj,
                e_inter=jnp.exp(m_row - mj),
                e_negm=jnp.exp(-(bc + mj)),
                wg=jnp.exp(r - mj_last),
                decay=jnp.exp(m_row - mj_last))
            m_row = bc[CHUNK - 1:CHUNK, :] + mj_last
        mst_ref[...] = m_row

    def ck_a(hd, ck):
        cs = slice(hd * DH_M, (hd + 1) * DH_M)
        rows = slice(ck * CHUNK, (ck + 1) * CHUNK)
        if ck == 0:
            val[f"C{hd}"] = cst_ref[hd]
            val[f"n{hd}"] = nst_ref[hd:hd + 1, :]
        q = q_ref[rows, cs]
        val[f"qk{hd}{ck}"] = lax.dot_general(q, k_ref[rows, cs], (((1,), (1,)), ((), ())),
                                            preferred_element_type=F32)
        val[f"qc{hd}{ck}"] = _dot(q, val[f"C{hd}"].astype(BF16))

    def ck_b(hd, ck):
        ln = slice(H_M + hd, H_M + hd + 1)
        gt = val[f"gt{ck}"]
        causal = causal_ref[...] > 0.5
        w = jnp.where(causal, jnp.exp(gt["rt"][ln, :] - gt["mj"][:, ln]), 0.0)
        s_qk = val.pop(f"qk{hd}{ck}") * w
        val[f"rowsum{hd}{ck}"] = jnp.sum(s_qk, axis=-1, keepdims=True)
        val[f"s{hd}{ck}"] = s_qk.astype(BF16)

    def ck_c(hd, ck):
        cs = slice(hd * DH_M, (hd + 1) * DH_M)
        rows = slice(ck * CHUNK, (ck + 1) * CHUNK)
        val[f"sv{hd}{ck}"] = _dot(val.pop(f"s{hd}{ck}"), v_ref[rows, cs])

    def ck_d(hd, ck):
        cs = slice(hd * DH_M, (hd + 1) * DH_M)
        rows = slice(ck * CHUNK, (ck + 1) * CHUNK)
        ln = slice(H_M + hd, H_M + hd + 1)
        gt = val[f"gt{ck}"]
        e_inter = gt["e_inter"][:, ln]
        num = val.pop(f"sv{hd}{ck}") + e_inter * val.pop(f"qc{hd}{ck}")
        den = val.pop(f"rowsum{hd}{ck}") + \
            e_inter * jnp.sum(q_ref[rows, cs].astype(F32) * val[f"n{hd}"], axis=-1, keepdims=True)
        hm_ref[rows, cs] = num * (1.0 / jnp.maximum(jnp.abs(den), gt["e_negm"][:, ln]))
        kw = k_ref[rows, cs].astype(F32) * gt["wg"][:, ln]
        val[f"n{hd}"] = gt["decay"][:, ln] * val[f"n{hd}"] + jnp.sum(kw, axis=0, keepdims=True)
        val[f"kw{hd}{ck}"] = kw.astype(BF16)

    def ck_e(hd, ck):
        cs = slice(hd * DH_M, (hd + 1) * DH_M)
        rows = slice(ck * CHUNK, (ck + 1) * CHUNK)
        val[f"kv{hd}{ck}"] = lax.dot_general(val.pop(f"kw{hd}{ck}"), v_ref[rows, cs],
                                            (((0,), (0,)), ((), ())), preferred_element_type=F32)

    def ck_f(hd, ck):
        ln = slice(H_M + hd, H_M + hd + 1)
        c_new = val[f"gt{ck}"]["decay"][:, ln] * val.pop(f"C{hd}") + val.pop(f"kv{hd}{ck}")
        if ck + 1 < n_ck:
            val[f"C{hd}"] = c_new
        else:
            cst_ref[hd] = c_new
            nst_ref[hd:hd + 1, :] = val.pop(f"n{hd}")

    for hd in range(H_M):
        for ck in range(n_ck):
            for nm, fn in (("ck_a", ck_a), ("ck_b", ck_b), ("ck_c", ck_c), ("ck_d", ck_d),
                           ("ck_e", ck_e), ("ck_f", ck_f)):
                steps[f"{nm}{hd}{ck}"] = functools.partial(fn, hd, ck)

    def epilogue(hd):
        cs = slice(hd * DH_M, (hd + 1) * DH_M)
        hg = jax.nn.sigmoid(mo_ref[:, cs]) * hm_ref[:, cs]
        mu = jnp.mean(hg, axis=-1, keepdims=True)
        dlt = hg - mu
        var = jnp.mean(dlt * dlt, axis=-1, keepdims=True)
        hn = dlt * lax.rsqrt(var + EPS) * prow(P_NORMG, cs) + prow(P_SKIP, cs) * c_ref[:, cs]
        ym_ref[:, cs] = (hn * _silu(mz_ref[:, cs])).astype(BF16)

    for hd in range(H_M):
        steps[f"epi{hd}"] = functools.partial(epilogue, hd)

    def br(j):
        val[f"br{j}"] = _dot(yr_ref[...], wbr_ref[:, j * PAIR:(j + 1) * PAIR])

    def bm(j):
        val[f"bm{j}"] = _dot(ym_ref[...], wbm_ref[:, j * PAIR:(j + 1) * PAIR])

    def merge(j):
        cs = slice(j * PAIR, (j + 1) * PAIR)
        y = jax.nn.sigmoid(gr_ref[:, cs]) * val.pop(f"br{j}") + \
            jax.nn.sigmoid(gm_ref[:, cs]) * val.pop(f"bm{j}")
        y_ref[:, cs] = y.astype(BF16)

    for j in range(N_G // 2):
        steps[f"br{j}"] = functools.partial(br, j)
        steps[f"bm{j}"] = functools.partial(bm, j)
        steps[f"merge{j}"] = functools.partial(merge, j)

    @step("unperm")
    def _():
        val["y_time"] = _dot(unperm_ref[...], y_ref[...]).astype(BF16)

    @step("out")
    def _():
        val["out"] = _dot(val.pop("y_time"), wout_ref[...])

    @step("post")
    def _():
        out = val.pop("out")
        ms_o = jnp.mean(out * out, axis=-1, keepdims=True)
        o_ref[...] = x_ref[...] + out * lax.rsqrt(ms_o + EPS) * p_ref[P_GPOST:P_GPOST + 1, :]

    done = set()
    for name in ORDER:
        assert all(d in done for d in STEP_DEPS[name]), (name, STEP_DEPS[name])
        steps[name]()
        done.add(name)
    assert done == set(steps), set(steps) - done


def _block_diag(w, group):
    h, d, e = w.shape
    w = w.reshape(h // group, group, d, e)
    eye = jnp.eye(group, dtype=w.dtype)
    return jnp.einsum("gide,ij->gidje", w, eye).reshape(h // group, group * d, group * e)


def _row_order_constants():
    p = np.arange(TILE_T)
    q = p % CHUNK
    time_of = (p // CHUNK) * CHUNK + (q % SUBLANES) * SEG + q // SUBLANES
    perm = np.zeros((TILE_T, TILE_T), np.float32)
    perm[p, time_of] = 1.0
    t = time_of[:CHUNK]
    causal_p = (t[None, :] <= t[:, None]).astype(np.float32)
    return jnp.asarray(perm, BF16), jnp.asarray(causal_p, F32)


def _layer(x, g_pre, w_in, rg_conv_w, rg_conv_b, rg_w_a, rg_b_a, rg_w_x, rg_b_x, rg_lambda,
           ml_conv_w, ml_conv_b, ml_w_q, ml_w_k, ml_w_v, ml_b_i, ml_b_f, ml_norm_g, ml_skip,
           w_branch_r, w_branch_m, w_out, g_post):
    bsz, seq, d = x.shape
    assert d == D_MODEL and seq % TILE_T == 0 and TILE_T % CHUNK == 0

    n_if = 2 * H_M
    o_i = 5 * D_MODEL
    w_main = w_in[:, :o_i].astype(BF16)
    w_gate = w_in[:, o_i + n_if:].astype(BF16)
    w_if = jnp.pad(w_in[:, o_i:o_i + n_if], ((0, 0), (0, LANES - n_if))).astype(BF16)

    group = COL_G // DH_R
    wax = jnp.concatenate([_block_diag(rg_w_a, group), _block_diag(rg_w_x, group)], axis=2).astype(BF16)
    wqk = jnp.concatenate([ml_w_q, ml_w_k * (1.0 / math.sqrt(DH_M))], axis=2).astype(BF16)

    b_if = jnp.pad(jnp.concatenate([ml_b_i, ml_b_f]), (0, D_MODEL - n_if))
    rows = [g_pre[None], rg_conv_w, rg_conv_b[None], rg_b_a[None], rg_b_x[None], rg_lambda[None],
            ml_conv_w, ml_conv_b[None], ml_norm_g[None], ml_skip[None], g_post[None], b_if[None]]
    params = jnp.concatenate(rows, axis=0).astype(F32)
    params = jnp.pad(params, ((0, P_ROWS - params.shape[0]), (0, 0)))

    perm, causal_p = _row_order_constants()

    resident = pl.BlockSpec(memory_space=pltpu.VMEM)
    tile = pl.BlockSpec((None, TILE_T, D_MODEL), lambda b, t: (b, t, 0))
    slab_f32 = pltpu.VMEM((TILE_T, D_MODEL), F32)
    slab_bf16 = pltpu.VMEM((TILE_T, D_MODEL), BF16)
    conv_tail = pltpu.VMEM(((CONV_W - 1) * SUBLANES, D_MODEL), F32)
    scratch = [
        slab_bf16,
        conv_tail, conv_tail,
        slab_bf16,
        slab_f32,
        slab_bf16, slab_bf16, slab_bf16,
        slab_f32,
        slab_bf16,
        slab_bf16,
        slab_f32, slab_f32, slab_f32, slab_f32,
        pltpu.VMEM((SUBLANES, D_MODEL), F32),
        pltpu.VMEM((H_M, DH_M, DH_M), F32),
        pltpu.VMEM((SUBLANES, DH_M), F32),
        pltpu.VMEM((1, LANES), F32),
    ]
    return pl.pallas_call(
        _block_kernel,
        out_shape=jax.ShapeDtypeStruct(x.shape, x.dtype),
        grid=(bsz, seq // TILE_T),
        in_specs=[tile] + [resident] * 13,
        out_specs=tile,
        scratch_shapes=scratch,
        compiler_params=pltpu.CompilerParams(
            dimension_semantics=("arbitrary", "arbitrary"),
            vmem_limit_bytes=VMEM_LIMIT_BYTES),
        name="hybrid_block",
    )(x, w_main, w_gate, w_if, wax, wqk, ml_w_v.astype(BF16),
      w_branch_r.astype(BF16), w_branch_m.astype(BF16), w_out.astype(BF16), params,
      perm, perm.T, causal_p)


def kernel(x, g_pre, w_in, rg_conv_w, rg_conv_b, rg_w_a, rg_b_a, rg_w_x, rg_b_x, rg_lambda,
           ml_conv_w, ml_conv_b, ml_w_q, ml_w_k, ml_w_v, ml_b_i, ml_b_f, ml_norm_g, ml_skip,
           w_branch_r, w_branch_m, w_out, g_post):
    h = x
    for l in range(g_pre.shape[0]):
        h = _layer(h, g_pre[l], w_in[l], rg_conv_w[l], rg_conv_b[l], rg_w_a[l], rg_b_a[l],
                   rg_w_x[l], rg_b_x[l], rg_lambda[l], ml_conv_w[l], ml_conv_b[l],
                   ml_w_q[l], ml_w_k[l], ml_w_v[l], ml_b_i[l], ml_b_f[l], ml_norm_g[l],
                   ml_skip[l], w_branch_r[l], w_branch_m[l], w_out[l], g_post[l])
    return h
```

```python
import functools
import math

import jax
import jax.numpy as jnp
import numpy as np
from jax import lax
from jax.experimental import pallas as pl
from jax.experimental.pallas import tpu as pltpu

D_MODEL = 1024
H_R = 16
DH_R = D_MODEL // H_R
RG_C = 8.0
CONV_W = 4
H_M = 4
DH_M = D_MODEL // H_M
CHUNK = 128
EPS = 1e-6

TILE_T = 256
COL_G = 256
N_G = D_MODEL // COL_G
SUBLANES = 8
LANES = 128
SEG = CHUNK // SUBLANES
VMEM_LIMIT_BYTES = 56 * 1024 * 1024

PAIR = 2 * COL_G
OFF_RX, OFF_RZ, OFF_MX, OFF_MZ, OFF_MO = (j * D_MODEL for j in range(5))
OFF_GR, OFF_GM = 0, D_MODEL

(P_GPRE, P_RCW, P_RCB, P_RBA, P_RBX, P_LAM, P_MCW, P_MCB, P_NORMG, P_SKIP, P_GPOST, P_BIF) = (
    0, 1, 5, 6, 7, 8, 9, 13, 14, 15, 16, 17)
P_ROWS = 24

F32 = jnp.float32
BF16 = jnp.bfloat16


def _dot(a, b):
    return jnp.dot(a, b, preferred_element_type=F32)


def _silu(x):
    return x * jax.nn.sigmoid(x)


def _step_deps():
    n_ck = TILE_T // CHUNK
    d = {"pre": ()}
    for g in range(N_G):
        d[f"fill{g}"] = ("pre",)
        d[f"rg_dot{g}"] = ("pre",)
        d[f"rg_conv{g}"] = (f"rg_dot{g}",)
        d[f"rg_gdot{g}"] = (f"rg_conv{g}",)
        d[f"rg_gates{g}"] = (f"rg_gdot{g}",)
        d[f"rg_scan{g}"] = (f"rg_gates{g}",)
        d[f"rg_out{g}"] = (f"rg_scan{g}",)
        d[f"mm_dot{g}"] = ("pre",)
        d[f"mm_conv{g}"] = (f"mm_dot{g}",)
        d[f"mm_qkv{g}"] = (f"mm_conv{g}",)
        for ck in range(n_ck):
            d[f"ck_a{g}{ck}"] = (f"mm_qkv{g}",) + ((f"ck_f{g}{ck - 1}",) if ck else ())
            d[f"ck_b{g}{ck}"] = (f"ck_a{g}{ck}", "gv_vec")
            d[f"ck_c{g}{ck}"] = (f"ck_b{g}{ck}",)
            d[f"ck_d{g}{ck}"] = (f"ck_c{g}{ck}",) + ((f"ck_d{g}{ck - 1}",) if ck else ())
            d[f"ck_e{g}{ck}"] = (f"ck_d{g}{ck}",)
            d[f"ck_f{g}{ck}"] = (f"ck_e{g}{ck}",)
        d[f"epi{g}"] = (f"ck_d{g}{n_ck - 1}", f"mm_dot{g}", f"mm_conv{g}", f"fill{g}")
    for j in range(N_G // 2):
        d[f"fill{N_G + j}"] = ("pre",)
        d[f"br{j}"] = tuple(f"rg_out{g}" for g in range(N_G))
        d[f"bm{j}"] = tuple(f"epi{g}" for g in range(N_G))
        d[f"merge{j}"] = (f"br{j}", f"bm{j}", f"fill{2 * j}", f"fill{2 * j + 1}", f"fill{N_G + j}")
    d["gv_dot"] = ("pre",)
    d["gv_z"] = ("gv_dot",)
    d["gv_cum"] = ("gv_z",)
    d["gv_vec"] = ("gv_cum",)
    d["unperm"] = tuple(f"merge{j}" for j in range(N_G // 2))
    d["out"] = ("unperm",)
    d["post"] = ("out",)
    return d


STEP_DEPS = _step_deps()


def _issue_order():
    n_ck = TILE_T // CHUNK
    o = ["pre", "mm_dot0", "gv_dot", "mm_dot1", "mm_conv0"]
    gate_vec = {1: "gv_z", 2: "gv_cum"}
    for hd in range(1, H_M):
        o += [f"mm_dot{hd + 1}" if hd + 1 < H_M else "rg_dot0", f"mm_qkv{hd - 1}"]
        o += [gate_vec[hd]] if hd == 1 else []
        o += [f"mm_conv{hd}"]
        o += [gate_vec[hd]] if hd == 2 else []
    o += [f"mm_qkv{H_M - 1}", "gv_vec"]
    fills = [f"fill{i}" for i in range(N_G + N_G // 2)]
    for hd in range(H_M):
        ck = lambda c, s: f"ck_{s}{hd}{c}"
        o += [ck(0, "a")] + ([f"rg_dot{hd + 1}"] if hd + 1 < H_M else [])
        o += [f"rg_conv{hd}", ck(0, "b"), f"rg_gdot{hd}", ck(0, "c"), fills.pop(0)]
        o += [ck(0, "d"), f"rg_gates{hd}", ck(0, "e"), ck(0, "f")]
        rg_tail = [f"rg_scan{hd}", f"rg_out{hd}"]
        for c in range(1, n_ck):
            o += [ck(c, "a"), ck(c, "b")] + rg_tail[:1] + [ck(c, "c"), ck(c, "d")] + rg_tail[1:]
            o += [ck(c, "e"), ck(c, "f")]
            rg_tail = []
        o += rg_tail
        if hd % 2 == 1:
            o.append(fills.pop(0))
        if hd == H_M - 1:
            o += [f"br{j}" for j in range(N_G // 2)]
        o.append(f"epi{hd}")
    o += [f"bm{j}" for j in range(N_G // 2)] + [f"merge{j}" for j in range(N_G // 2)]
    o += ["unperm", "out", "post"]
    return o


ORDER = _issue_order()


def _linear_scan(a, b, h0):
    tn, c = a.shape
    a3 = a.reshape(tn // SUBLANES, SUBLANES, c)
    b3 = b.reshape(tn // SUBLANES, SUBLANES, c)
    sub = lax.broadcasted_iota(jnp.int32, (SUBLANES, c), 0)
    hs = []
    carry = h0
    for ck in range(tn // CHUNK):
        base = ck * SEG
        h = b3[base]
        p = a3[base]
        h_loc, p_loc = [h], [p]
        for r in range(1, SEG):
            h = a3[base + r] * h + b3[base + r]
            p = a3[base + r] * p
            h_loc.append(h)
            p_loc.append(p)
        s = 1
        while s < SUBLANES:
            keep = sub >= s
            h = jnp.where(keep, p * pltpu.roll(h, s, 0) + h, h)
            p = jnp.where(keep, p * pltpu.roll(p, s, 0), p)
            s *= 2
        end_state = p * carry + h
        seg_in = jnp.where(sub >= 1, pltpu.roll(end_state, 1, 0), carry)
        for r in range(SEG):
            hs.append(h_loc[r] + p_loc[r] * seg_in)
        carry = end_state[SUBLANES - 1:SUBLANES, :]
    return jnp.stack(hs, axis=0).reshape(tn, c), carry


def _causal_conv(tail_ref, cs, y, w_rows, bias):
    tn, c = y.shape
    n_tail = CONV_W - 1
    y3 = y.reshape(tn // SUBLANES, SUBLANES, c)
    first = lax.broadcasted_iota(jnp.int32, (n_tail, SUBLANES, c), 1) == 0
    prev_last = tail_ref[:, cs].reshape(n_tail, SUBLANES, c)
    outs = []
    for ck in range(tn // CHUNK):
        cur = y3[ck * SEG:(ck + 1) * SEG]
        wrapped = jnp.where(first, pltpu.roll(prev_last, 1, 1), pltpu.roll(cur[SEG - n_tail:], 1, 1))
        acc = bias + w_rows[CONV_W - 1] * cur
        for shift in range(1, CONV_W):
            shifted = jnp.concatenate([wrapped[n_tail - shift:], cur[:SEG - shift]], axis=0)
            acc = acc + w_rows[CONV_W - 1 - shift] * shifted
        outs.append(acc)
        prev_last = cur[SEG - n_tail:]
    tail_ref[:, cs] = prev_last.reshape(n_tail * SUBLANES, c)
    return jnp.concatenate(outs, axis=0).reshape(tn, c)


def _block_kernel(x_ref, w_ref, wg_ref, wif_ref, wax_ref, wqk_ref, wv_ref, wbr_ref, wbm_ref, wout_ref,
                  p_ref, perm_ref, unperm_ref, causal_ref, o_ref,
                  xn_ref, rxb_ref, mxb_ref, yr_ref, c_ref, q_ref, k_ref, v_ref, hm_ref,
                  ym_ref, y_ref, gr_ref, gm_ref, mo_ref, mz_ref,
                  hr_ref, cst_ref, nst_ref, mst_ref):
    tn = x_ref.shape[0]

    @pl.when(pl.program_id(1) == 0)
    def _reset_state():
        rxb_ref[...] = jnp.zeros_like(rxb_ref)
        mxb_ref[...] = jnp.zeros_like(mxb_ref)
        hr_ref[...] = jnp.zeros_like(hr_ref)
        cst_ref[...] = jnp.zeros_like(cst_ref)
        nst_ref[...] = jnp.zeros_like(nst_ref)
        mst_ref[...] = jnp.zeros_like(mst_ref)

    def prow(r, cs):
        return p_ref[r:r + 1, cs]

    n_ck = tn // CHUNK
    val = {}
    steps = {}

    def step(name):
        def register(fn):
            steps[name] = fn
            return fn
        return register

    @step("pre")
    def _():
        x = x_ref[...]
        ms = jnp.mean(x * x, axis=-1, keepdims=True)
        xn_time = (x * lax.rsqrt(ms + EPS) * p_ref[P_GPRE:P_GPRE + 1, :]).astype(BF16)
        xn_ref[...] = _dot(perm_ref[...], xn_time).astype(BF16)

    def fill_mz_gr(g):
        cs = slice(g * COL_G, (g + 1) * COL_G)
        xn = xn_ref[...]
        mz_ref[:, cs] = _dot(xn, w_ref[:, OFF_MZ + g * COL_G:OFF_MZ + (g + 1) * COL_G])
        gr_ref[:, cs] = _dot(xn, wg_ref[:, OFF_GR + g * COL_G:OFF_GR + (g + 1) * COL_G])

    def fill_gm(j):
        gm_ref[:, j * PAIR:(j + 1) * PAIR] = _dot(xn_ref[...], wg_ref[:, OFF_GM + j * PAIR:OFF_GM + (j + 1) * PAIR])

    for g in range(N_G):
        steps[f"fill{g}"] = functools.partial(fill_mz_gr, g)
    for j in range(N_G // 2):
        steps[f"fill{N_G + j}"] = functools.partial(fill_gm, j)

    def rg_dot(g):
        xn = xn_ref[...]
        val[f"rx{g}"] = _dot(xn, w_ref[:, OFF_RX + g * COL_G:OFF_RX + (g + 1) * COL_G])
        val[f"rz{g}"] = _dot(xn, w_ref[:, OFF_RZ + g * COL_G:OFF_RZ + (g + 1) * COL_G])

    def rg_conv(g):
        cs = slice(g * COL_G, (g + 1) * COL_G)
        u = _causal_conv(rxb_ref, cs, val.pop(f"rx{g}"), [prow(P_RCW + j, cs) for j in range(CONV_W)],
                         prow(P_RCB, cs))
        val[f"u{g}"] = u
        val[f"ub{g}"] = u.astype(BF16)

    def rg_gdot(g):
        val[f"gate{g}"] = _dot(val.pop(f"ub{g}"), wax_ref[g])

    def rg_gates(g):
        cs = slice(g * COL_G, (g + 1) * COL_G)
        u = val.pop(f"u{g}")
        gate = val.pop(f"gate{g}")
        r = jax.nn.sigmoid(gate[:, :COL_G] + prow(P_RBA, cs))
        i = jax.nn.sigmoid(gate[:, COL_G:] + prow(P_RBX, cs))
        log_a = (-RG_C * jax.nn.softplus(-prow(P_LAM, cs))) * r
        s_half = jnp.tanh(0.5 * log_a)
        inv = 1.0 / (1.0 - s_half)
        val[f"a{g}"] = (1.0 + s_half) * inv
        val[f"b{g}"] = (2.0 * jnp.sqrt(-s_half) * inv) * (i * u)

    def rg_scan(g):
        cs = slice(g * COL_G, (g + 1) * COL_G)
        h, h_last = _linear_scan(val.pop(f"a{g}"), val.pop(f"b{g}"), hr_ref[0:1, cs])
        hr_ref[0:1, cs] = h_last
        val[f"h{g}"] = h

    def rg_out(g):
        cs = slice(g * COL_G, (g + 1) * COL_G)
        yr_ref[:, cs] = (val.pop(f"h{g}") * _silu(val.pop(f"rz{g}"))).astype(BF16)

    def mm_dot(hd):
        cs = slice(hd * DH_M, (hd + 1) * DH_M)
        xn = xn_ref[...]
        val[f"mx{hd}"] = _dot(xn, w_ref[:, OFF_MX + hd * DH_M:OFF_MX + (hd + 1) * DH_M])
        mo_ref[:, cs] = _dot(xn, w_ref[:, OFF_MO + hd * DH_M:OFF_MO + (hd + 1) * DH_M])

    def mm_conv(hd):
        cs = slice(hd * DH_M, (hd + 1) * DH_M)
        mx = val.pop(f"mx{hd}")
        c = _silu(_causal_conv(mxb_ref, cs, mx, [prow(P_MCW + j, cs) for j in range(CONV_W)],
                               prow(P_MCB, cs)))
        c_ref[:, cs] = c
        val[f"cb{hd}"] = c.astype(BF16)
        val[f"mxb{hd}"] = mx.astype(BF16)

    def mm_qkv(hd):
        cs = slice(hd * DH_M, (hd + 1) * DH_M)
        qk = _dot(val.pop(f"cb{hd}"), wqk_ref[hd])
        q_ref[:, cs] = qk[:, :DH_M].astype(BF16)
        k_ref[:, cs] = qk[:, DH_M:].astype(BF16)
        v_ref[:, cs] = _dot(val.pop(f"mxb{hd}"), wv_ref[hd]).astype(BF16)

    for g in range(N_G):
        for nm, fn in (("rg_dot", rg_dot), ("rg_conv", rg_conv), ("rg_gdot", rg_gdot),
                       ("rg_gates", rg_gates), ("rg_scan", rg_scan), ("rg_out", rg_out),
                       ("mm_dot", mm_dot), ("mm_conv", mm_conv), ("mm_qkv", mm_qkv)):
            steps[f"{nm}{g}"] = functools.partial(fn, g)

    @step("gv_dot")
    def _():
        val["pre_if"] = _dot(xn_ref[...], wif_ref[...])

    @step("gv_z")
    def _():
        pre_if = val.pop("pre_if") + p_ref[P_BIF:P_BIF + 1, 0:LANES]
        lane = lax.broadcasted_iota(jnp.int32, pre_if.shape, 1)
        val["z"] = jnp.where(lane < H_M, pre_if, jax.nn.log_sigmoid(pre_if))

    @step("gv_cum")
    def _():
        z = val["z"]
        sub = lax.broadcasted_iota(jnp.int32, (SUBLANES, LANES), 0)
        bcs = []
        for ck in range(n_ck):
            bc = z[ck * CHUNK:(ck + 1) * CHUNK, :].reshape(SEG, SUBLANES, LANES)
            s = 1
            while s < SEG:
                bc = jnp.concatenate([bc[:s], bc[s:] + bc[:-s]], axis=0)
                s *= 2
            seg_sum = bc[SEG - 1]
            s = 1
            while s < SUBLANES:
                seg_sum = jnp.where(sub >= s, seg_sum + pltpu.roll(seg_sum, s, 0), seg_sum)
                s *= 2
            before = jnp.where(sub >= 1, pltpu.roll(seg_sum, 1, 0), 0.0)
            bcs.append((bc + before).reshape(CHUNK, LANES))
        val["bc"] = bcs

    @step("gv_vec")
    def _():
        z_all = val.pop("z")
        bcs = val.pop("bc")
        sub = lax.broadcasted_iota(jnp.int32, (SUBLANES, LANES), 0)
        m_row = mst_ref[...]
        for ck in range(n_ck):
            z = z_all[ck * CHUNK:(ck + 1) * CHUNK, :]
            bc = bcs[ck]
            r = pltpu.roll(z, H_M, 1) - bc
            cm = r.reshape(SEG, SUBLANES, LANES)
            s = 1
            while s < SEG:
                cm = jnp.concatenate([cm[:s], jnp.maximum(cm[s:], cm[:-s])], axis=0)
                s *= 2
            seg_max = cm[SEG - 1]
            s = 1
            while s < SUBLANES:
                seg_max = jnp.where(sub >= s, jnp.maximum(seg_max, pltpu.roll(seg_max, s, 0)), seg_max)
                s *= 2
            before = jnp.where(sub >= 1, pltpu.roll(seg_max, 1, 0), -jnp.inf)
            cm = jnp.maximum(cm, before).reshape(CHUNK, LANES)
            mj = jnp.maximum(m_row, cm)
            mj_last = mj[CHUNK - 1:CHUNK, :]
            val[f"gt{ck}"] = dict(
                rt=r.T,
                mj=mj,
                e_inter=jnp.exp(m_row - mj),
                e_negm=jnp.exp(-(bc + mj)),
                wg=jnp.exp(r - mj_last),
                decay=jnp.exp(m_row - mj_last))
            m_row = bc[CHUNK - 1:CHUNK, :] + mj_last
        mst_ref[...] = m_row

    def ck_a(hd, ck):
        cs = slice(hd * DH_M, (hd + 1) * DH_M)
        rows = slice(ck * CHUNK, (ck + 1) * CHUNK)
        if ck == 0:
            val[f"C{hd}"] = cst_ref[hd]
            val[f"n{hd}"] = nst_ref[hd:hd + 1, :]
        q = q_ref[rows, cs]
        val[f"qk{hd}{ck}"] = lax.dot_general(q, k_ref[rows, cs], (((1,), (1,)), ((), ())),
                                            preferred_element_type=F32)
        val[f"qc{hd}{ck}"] = _dot(q, val[f"C{hd}"].astype(BF16))

    def ck_b(hd, ck):
        ln = slice(H_M + hd, H_M + hd + 1)
        gt = val[f"gt{ck}"]
        causal = causal_ref[...] > 0.5
        w = jnp.where(causal, jnp.exp(gt["rt"][ln, :] - gt["mj"][:, ln]), 0.0)
        s_qk = val.pop(f"qk{hd}{ck}") * w
        val[f"rowsum{hd}{ck}"] = jnp.sum(s_qk, axis=-1, keepdims=True)
        val[f"s{hd}{ck}"] = s_qk.astype(BF16)

    def ck_c(hd, ck):
        cs = slice(hd * DH_M, (hd + 1) * DH_M)
        rows = slice(ck * CHUNK, (ck + 1) * CHUNK)
        val[f"sv{hd}{ck}"] = _dot(val.pop(f"s{hd}{ck}"), v_ref[rows, cs])

    def ck_d(hd, ck):
        cs = slice(hd * DH_M, (hd + 1) * DH_M)
        rows = slice(ck * CHUNK, (ck + 1) * CHUNK)
        ln = slice(H_M + hd, H_M + hd + 1)
        gt = val[f"gt{ck}"]
        e_inter = gt["e_inter"][:, ln]
        num = val.pop(f"sv{hd}{ck}") + e_inter * val.pop(f"qc{hd}{ck}")
        den = val.pop(f"rowsum{hd}{ck}") + \
            e_inter * jnp.sum(q_ref[rows, cs].astype(F32) * val[f"n{hd}"], axis=-1, keepdims=True)
        hm_ref[rows, cs] = num * (1.0 / jnp.maximum(jnp.abs(den), gt["e_negm"][:, ln]))
        kw = k_ref[rows, cs].astype(F32) * gt["wg"][:, ln]
        val[f"n{hd}"] = gt["decay"][:, ln] * val[f"n{hd}"] + jnp.sum(kw, axis=0, keepdims=True)
        val[f"kw{hd}{ck}"] = kw.astype(BF16)

    def ck_e(hd, ck):
        cs = slice(hd * DH_M, (hd + 1) * DH_M)
        rows = slice(ck * CHUNK, (ck + 1) * CHUNK)
        val[f"kv{hd}{ck}"] = lax.dot_general(val.pop(f"kw{hd}{ck}"), v_ref[rows, cs],
                                            (((0,), (0,)), ((), ())), preferred_element_type=F32)

    def ck_f(hd, ck):
        ln = slice(H_M + hd, H_M + hd + 1)
        c_new = val[f"gt{ck}"]["decay"][:, ln] * val.pop(f"C{hd}") + val.pop(f"kv{hd}{ck}")
        if ck + 1 < n_ck:
            val[f"C{hd}"] = c_new
        else:
            cst_ref[hd] = c_new
            nst_ref[hd:hd + 1, :] = val.pop(f"n{hd}")

    for hd in range(H_M):
        for ck in range(n_ck):
            for nm, fn in (("ck_a", ck_a), ("ck_b", ck_b), ("ck_c", ck_c), ("ck_d", ck_d),
                           ("ck_e", ck_e), ("ck_f", ck_f)):
                steps[f"{nm}{hd}{ck}"] = functools.partial(fn, hd, ck)

    def epilogue(hd):
        cs = slice(hd * DH_M, (hd + 1) * DH_M)
        hg = jax.nn.sigmoid(mo_ref[:, cs]) * hm_ref[:, cs]
        mu = jnp.mean(hg, axis=-1, keepdims=True)
        dlt = hg - mu
        var = jnp.mean(dlt * dlt, axis=-1, keepdims=True)
        hn = dlt * lax.rsqrt(var + EPS) * prow(P_NORMG, cs) + prow(P_SKIP, cs) * c_ref[:, cs]
        ym_ref[:, cs] = (hn * _silu(mz_ref[:, cs])).astype(BF16)

    for hd in range(H_M):
        steps[f"epi{hd}"] = functools.partial(epilogue, hd)

    def br(j):
        val[f"br{j}"] = _dot(yr_ref[...], wbr_ref[:, j * PAIR:(j + 1) * PAIR])

    def bm(j):
        val[f"bm{j}"] = _dot(ym_ref[...], wbm_ref[:, j * PAIR:(j + 1) * PAIR])

    def merge(j):
        cs = slice(j * PAIR, (j + 1) * PAIR)
        y = jax.nn.sigmoid(gr_ref[:, cs]) * val.pop(f"br{j}") + \
            jax.nn.sigmoid(gm_ref[:, cs]) * val.pop(f"bm{j}")
        y_ref[:, cs] = y.astype(BF16)

    for j in range(N_G // 2):
        steps[f"br{j}"] = functools.partial(br, j)
        steps[f"bm{j}"] = functools.partial(bm, j)
        steps[f"merge{j}"] = functools.partial(merge, j)

    @step("unperm")
    def _():
        val["y_time"] = _dot(unperm_ref[...], y_ref[...]).astype(BF16)

    @step("out")
    def _():
        val["out"] = _dot(val.pop("y_time"), wout_ref[:, :D_MODEL])

    @step("post")
    def _():
        out = val.pop("out")
        ms_o = jnp.mean(out * out, axis=-1, keepdims=True)
        o_ref[...] = x_ref[...] + out * lax.rsqrt(ms_o + EPS) * p_ref[P_GPOST:P_GPOST + 1, :]

    done = set()
    for name in ORDER:
        assert all(d in done for d in STEP_DEPS[name]), (name, STEP_DEPS[name])
        steps[name]()
        done.add(name)
    assert done == set(steps), set(steps) - done


def _block_diag(w, group):
    h, d, e = w.shape
    w = w.reshape(h // group, group, d, e)
    eye = jnp.eye(group, dtype=w.dtype)
    return jnp.einsum("gide,ij->gidje", w, eye).reshape(h // group, group * d, group * e)


def _row_order_constants():
    p = np.arange(TILE_T)
    q = p % CHUNK
    time_of = (p // CHUNK) * CHUNK + (q % SUBLANES) * SEG + q // SUBLANES
    perm = np.zeros((TILE_T, TILE_T), np.float32)
    perm[p, time_of] = 1.0
    t = time_of[:CHUNK]
    causal_p = (t[None, :] <= t[:, None]).astype(np.float32)
    return jnp.asarray(perm, BF16), jnp.asarray(causal_p, F32)


def _layer(x, g_pre, w_in, rg_conv_w, rg_conv_b, rg_w_a, rg_b_a, rg_w_x, rg_b_x, rg_lambda,
           ml_conv_w, ml_conv_b, ml_w_q, ml_w_k, ml_w_v, ml_b_i, ml_b_f, ml_norm_g, ml_skip,
           w_branch_r, w_branch_m, w_out, g_post):
    bsz, seq, d = x.shape
    assert d == D_MODEL and seq % TILE_T == 0 and TILE_T % CHUNK == 0

    n_if = 2 * H_M
    o_i = 5 * D_MODEL
    def odd_width(w):
        tiles = -(-w.shape[-1] // LANES)
        return jnp.pad(w, ((0, 0), (0, (tiles + 1 - tiles % 2) * LANES - w.shape[-1])))

    w_main = w_in.astype(BF16)
    assert -(-w_main.shape[-1] // LANES) % 2 == 1
    w_gate = odd_width(w_in[:, o_i + n_if:].astype(BF16))
    w_if = jnp.pad(w_in[:, o_i:o_i + n_if], ((0, 0), (0, LANES - n_if))).astype(BF16)

    group = COL_G // DH_R
    wax = jnp.concatenate([_block_diag(rg_w_a, group), _block_diag(rg_w_x, group)], axis=2).astype(BF16)
    wqk = jnp.concatenate([ml_w_q, ml_w_k * (1.0 / math.sqrt(DH_M))], axis=2).astype(BF16)

    b_if = jnp.pad(jnp.concatenate([ml_b_i, ml_b_f]), (0, D_MODEL - n_if))
    rows = [g_pre[None], rg_conv_w, rg_conv_b[None], rg_b_a[None], rg_b_x[None], rg_lambda[None],
            ml_conv_w, ml_conv_b[None], ml_norm_g[None], ml_skip[None], g_post[None], b_if[None]]
    params = jnp.concatenate(rows, axis=0).astype(F32)
    params = jnp.pad(params, ((0, P_ROWS - params.shape[0]), (0, 0)))

    perm, causal_p = _row_order_constants()

    resident = pl.BlockSpec(memory_space=pltpu.VMEM)
    tile = pl.BlockSpec((None, TILE_T, D_MODEL), lambda b, t: (b, t, 0))
    slab_f32 = pltpu.VMEM((TILE_T, D_MODEL), F32)
    slab_bf16 = pltpu.VMEM((TILE_T, D_MODEL), BF16)
    conv_tail = pltpu.VMEM(((CONV_W - 1) * SUBLANES, D_MODEL), F32)
    scratch = [
        slab_bf16,
        conv_tail, conv_tail,
        slab_bf16,
        slab_f32,
        slab_bf16, slab_bf16, slab_bf16,
        slab_f32,
        slab_bf16,
        slab_bf16,
        slab_f32, slab_f32, slab_f32, slab_f32,
        pltpu.VMEM((SUBLANES, D_MODEL), F32),
        pltpu.VMEM((H_M, DH_M, DH_M), F32),
        pltpu.VMEM((SUBLANES, DH_M), F32),
        pltpu.VMEM((1, LANES), F32),
    ]
    return pl.pallas_call(
        _block_kernel,
        out_shape=jax.ShapeDtypeStruct(x.shape, x.dtype),
        grid=(bsz, seq // TILE_T),
        in_specs=[tile] + [resident] * 13,
        out_specs=tile,
        scratch_shapes=scratch,
        compiler_params=pltpu.CompilerParams(
            dimension_semantics=("arbitrary", "arbitrary"),
            vmem_limit_bytes=VMEM_LIMIT_BYTES),
        name="hybrid_block",
    )(x, w_main, w_gate, w_if, wax, wqk, ml_w_v.astype(BF16),
      odd_width(w_branch_r.astype(BF16)), odd_width(w_branch_m.astype(BF16)),
      odd_width(w_out.astype(BF16)), params,
      perm, perm.T, causal_p)


def kernel(x, g_pre, w_in, rg_conv_w, rg_conv_b, rg_w_a, rg_b_a, rg_w_x, rg_b_x, rg_lambda,
           ml_conv_w, ml_conv_b, ml_w_q, ml_w_k, ml_w_v, ml_b_i, ml_b_f, ml_norm_g, ml_skip,
           w_branch_r, w_branch_m, w_out, g_post):
    h = x
    for l in range(g_pre.shape[0]):
        h = _layer(h, g_pre[l], w_in[l], rg_conv_w[l], rg_conv_b[l], rg_w_a[l], rg_b_a[l],
                   rg_w_x[l], rg_b_x[l], rg_lambda[l], ml_conv_w[l], ml_conv_b[l],
                   ml_w_q[l], ml_w_k[l], ml_w_v[l], ml_b_i[l], ml_b_f[l], ml_norm_g[l],
                   ml_skip[l], w_branch_r[l], w_branch_m[l], w_out[l], g_post[l])
    return h
```

```python
import functools
import math

import jax
import jax.numpy as jnp
import numpy as np
from jax import lax
from jax.experimental import pallas as pl
from jax.experimental.pallas import tpu as pltpu

D_MODEL = 1024
H_R = 16
DH_R = D_MODEL // H_R
RG_C = 8.0
CONV_W = 4
H_M = 4
DH_M = D_MODEL // H_M
CHUNK = 256
EPS = 1e-6

TILE_T = 256
COL_G = 256
N_G = D_MODEL // COL_G
SUBLANES = 8
LANES = 128
SEG = CHUNK // SUBLANES
VMEM_LIMIT_BYTES = 56 * 1024 * 1024

PAIR = 2 * COL_G
OFF_RX, OFF_RZ, OFF_MX, OFF_MZ, OFF_MO = (j * D_MODEL for j in range(5))
OFF_GR, OFF_GM = 0, D_MODEL

(P_GPRE, P_RCW, P_RCB, P_RBA, P_RBX, P_LAM, P_MCW, P_MCB, P_NORMG, P_SKIP, P_GPOST, P_BIF) = (
    0, 1, 5, 6, 7, 8, 9, 13, 14, 15, 16, 17)
P_ROWS = 24

F32 = jnp.float32
BF16 = jnp.bfloat16


def _dot(a, b):
    return jnp.dot(a, b, preferred_element_type=F32)


def _silu(x):
    return x * jax.nn.sigmoid(x)


def _step_deps():
    n_ck = TILE_T // CHUNK
    d = {"pre": ()}
    for g in range(N_G):
        d[f"fill{g}"] = ("pre",)
        d[f"rg_dot{g}"] = ("pre",)
        d[f"rg_conv{g}"] = (f"rg_dot{g}",)
        d[f"rg_gdot{g}"] = (f"rg_conv{g}",)
        d[f"rg_gates{g}"] = (f"rg_gdot{g}",)
        d[f"rg_scan{g}"] = (f"rg_gates{g}",)
        d[f"rg_out{g}"] = (f"rg_scan{g}",)
        d[f"mm_dot{g}"] = ("pre",)
        d[f"mm_conv{g}"] = (f"mm_dot{g}",)
        d[f"mm_qkv{g}"] = (f"mm_conv{g}",)
        for ck in range(n_ck):
            d[f"ck_a{g}{ck}"] = (f"mm_qkv{g}",) + ((f"ck_f{g}{ck - 1}",) if ck else ())
            d[f"ck_b{g}{ck}"] = (f"ck_a{g}{ck}", "gv_vec")
            d[f"ck_c{g}{ck}"] = (f"ck_b{g}{ck}",)
            d[f"ck_d{g}{ck}"] = (f"ck_c{g}{ck}",) + ((f"ck_d{g}{ck - 1}",) if ck else ())
            d[f"ck_e{g}{ck}"] = (f"ck_d{g}{ck}",)
            d[f"ck_f{g}{ck}"] = (f"ck_e{g}{ck}",)
        d[f"epi{g}"] = (f"ck_d{g}{n_ck - 1}", f"mm_dot{g}", f"mm_conv{g}", f"fill{g}")
    for j in range(N_G // 2):
        d[f"fill{N_G + j}"] = ("pre",)
        d[f"br{j}"] = tuple(f"rg_out{g}" for g in range(N_G))
        d[f"bm{j}"] = tuple(f"epi{g}" for g in range(N_G))
        d[f"merge{j}"] = (f"br{j}", f"bm{j}", f"fill{2 * j}", f"fill{2 * j + 1}", f"fill{N_G + j}")
    d["gv_dot"] = ("pre",)
    d["gv_z"] = ("gv_dot",)
    d["gv_cum"] = ("gv_z",)
    d["gv_vec"] = ("gv_cum",)
    d["unperm"] = tuple(f"merge{j}" for j in range(N_G // 2))
    d["out"] = ("unperm",)
    d["post"] = ("out",)
    return d


STEP_DEPS = _step_deps()


def _issue_order():
    n_ck = TILE_T // CHUNK
    o = ["pre", "mm_dot0", "gv_dot", "mm_dot1", "mm_conv0"]
    gate_vec = {1: "gv_z", 2: "gv_cum"}
    for hd in range(1, H_M):
        o += [f"mm_dot{hd + 1}" if hd + 1 < H_M else "rg_dot0", f"mm_qkv{hd - 1}"]
        o += [gate_vec[hd]] if hd == 1 else []
        o += [f"mm_conv{hd}"]
        o += [gate_vec[hd]] if hd == 2 else []
    o += [f"mm_qkv{H_M - 1}", "gv_vec"]
    fills = [f"fill{i}" for i in range(N_G + N_G // 2)]
    for hd in range(H_M):
        ck = lambda c, s: f"ck_{s}{hd}{c}"
        o += [ck(0, "a")] + ([f"rg_dot{hd + 1}"] if hd + 1 < H_M else [])
        o += [f"rg_conv{hd}", ck(0, "b"), f"rg_gdot{hd}", ck(0, "c"), fills.pop(0)]
        o += [ck(0, "d"), f"rg_gates{hd}", ck(0, "e"), ck(0, "f")]
        rg_tail = [f"rg_scan{hd}", f"rg_out{hd}"]
        for c in range(1, n_ck):
            o += [ck(c, "a"), ck(c, "b")] + rg_tail[:1] + [ck(c, "c"), ck(c, "d")] + rg_tail[1:]
            o += [ck(c, "e"), ck(c, "f")]
            rg_tail = []
        o += rg_tail
        if 0 < hd < H_M - 1:
            o.append(fills.pop(0))
        if hd == H_M - 1:
            o += [f"br{j}" for j in range(N_G // 2)]
        o.append(f"epi{hd}")
    o += [f"bm{j}" for j in range(N_G // 2)] + [f"merge{j}" for j in range(N_G // 2)]
    o += ["unperm", "out", "post"]
    return o


ORDER = _issue_order()


def _linear_scan(a, b, h0):
    tn, c = a.shape
    a3 = a.reshape(tn // SUBLANES, SUBLANES, c)
    b3 = b.reshape(tn // SUBLANES, SUBLANES, c)
    sub = lax.broadcasted_iota(jnp.int32, (SUBLANES, c), 0)
    hs = []
    carry = h0
    for ck in range(tn // CHUNK):
        base = ck * SEG
        h = b3[base]
        p = a3[base]
        h_loc, p_loc = [h], [p]
        for r in range(1, SEG):
            h = a3[base + r] * h + b3[base + r]
            p = a3[base + r] * p
            h_loc.append(h)
            p_loc.append(p)
        s = 1
        while s < SUBLANES:
            keep = sub >= s
            h = jnp.where(keep, p * pltpu.roll(h, s, 0) + h, h)
            p = jnp.where(keep, p * pltpu.roll(p, s, 0), p)
            s *= 2
        end_state = p * carry + h
        seg_in = jnp.where(sub >= 1, pltpu.roll(end_state, 1, 0), carry)
        for r in range(SEG):
            hs.append(h_loc[r] + p_loc[r] * seg_in)
        carry = end_state[SUBLANES - 1:SUBLANES, :]
    return jnp.stack(hs, axis=0).reshape(tn, c), carry


def _causal_conv(tail_ref, cs, y, w_rows, bias):
    tn, c = y.shape
    n_tail = CONV_W - 1
    y3 = y.reshape(tn // SUBLANES, SUBLANES, c)
    first = lax.broadcasted_iota(jnp.int32, (n_tail, SUBLANES, c), 1) == 0
    prev_last = tail_ref[:, cs].reshape(n_tail, SUBLANES, c)
    outs = []
    for ck in range(tn // CHUNK):
        cur = y3[ck * SEG:(ck + 1) * SEG]
        wrapped = jnp.where(first, pltpu.roll(prev_last, 1, 1), pltpu.roll(cur[SEG - n_tail:], 1, 1))
        acc = bias + w_rows[CONV_W - 1] * cur
        for shift in range(1, CONV_W):
            shifted = jnp.concatenate([wrapped[n_tail - shift:], cur[:SEG - shift]], axis=0)
            acc = acc + w_rows[CONV_W - 1 - shift] * shifted
        outs.append(acc)
        prev_last = cur[SEG - n_tail:]
    tail_ref[:, cs] = prev_last.reshape(n_tail * SUBLANES, c)
    return jnp.concatenate(outs, axis=0).reshape(tn, c)


def _block_kernel(x_ref, w_ref, wg_ref, wif_ref, wax_ref, wqk_ref, wv_ref, wbr_ref, wbm_ref, wout_ref,
                  p_ref, perm_ref, unperm_ref, causal_ref, o_ref,
                  xn_ref, rxb_ref, mxb_ref, yr_ref, c_ref, q_ref, k_ref, v_ref, hm_ref,
                  ym_ref, y_ref, gr_ref, gm_ref, mo_ref, mz_ref,
                  hr_ref, cst_ref, nst_ref, mst_ref):
    tn = x_ref.shape[0]

    @pl.when(pl.program_id(1) == 0)
    def _reset_state():
        rxb_ref[...] = jnp.zeros_like(rxb_ref)
        mxb_ref[...] = jnp.zeros_like(mxb_ref)
        hr_ref[...] = jnp.zeros_like(hr_ref)
        cst_ref[...] = jnp.zeros_like(cst_ref)
        nst_ref[...] = jnp.zeros_like(nst_ref)
        mst_ref[...] = jnp.zeros_like(mst_ref)

    def prow(r, cs):
        return p_ref[r:r + 1, cs]

    n_ck = tn // CHUNK
    val = {}
    steps = {}

    def step(name):
        def register(fn):
            steps[name] = fn
            return fn
        return register

    @step("pre")
    def _():
        x = x_ref[...]
        ms = jnp.mean(x * x, axis=-1, keepdims=True)
        xn_time = (x * lax.rsqrt(ms + EPS) * p_ref[P_GPRE:P_GPRE + 1, :]).astype(BF16)
        xn_ref[...] = _dot(perm_ref[...], xn_time).astype(BF16)

    def fill_mz_gr(g):
        cs = slice(g * COL_G, (g + 1) * COL_G)
        xn = xn_ref[...]
        mz_ref[:, cs] = _dot(xn, w_ref[:, OFF_MZ + g * COL_G:OFF_MZ + (g + 1) * COL_G])
        gr_ref[:, cs] = _dot(xn, wg_ref[:, OFF_GR + g * COL_G:OFF_GR + (g + 1) * COL_G])

    def fill_gm(j):
        gm_ref[:, j * PAIR:(j + 1) * PAIR] = _dot(xn_ref[...], wg_ref[:, OFF_GM + j * PAIR:OFF_GM + (j + 1) * PAIR])

    for g in range(N_G):
        steps[f"fill{g}"] = functools.partial(fill_mz_gr, g)
    for j in range(N_G // 2):
        steps[f"fill{N_G + j}"] = functools.partial(fill_gm, j)

    def rg_dot(g):
        xn = xn_ref[...]
        val[f"rx{g}"] = _dot(xn, w_ref[:, OFF_RX + g * COL_G:OFF_RX + (g + 1) * COL_G])
        val[f"rz{g}"] = _dot(xn, w_ref[:, OFF_RZ + g * COL_G:OFF_RZ + (g + 1) * COL_G])

    def rg_conv(g):
        cs = slice(g * COL_G, (g + 1) * COL_G)
        u = _causal_conv(rxb_ref, cs, val.pop(f"rx{g}"), [prow(P_RCW + j, cs) for j in range(CONV_W)],
                         prow(P_RCB, cs))
        val[f"u{g}"] = u
        val[f"ub{g}"] = u.astype(BF16)

    def rg_gdot(g):
        val[f"gate{g}"] = _dot(val.pop(f"ub{g}"), wax_ref[g])

    def rg_gates(g):
        cs = slice(g * COL_G, (g + 1) * COL_G)
        u = val.pop(f"u{g}")
        gate = val.pop(f"gate{g}")
        r = jax.nn.sigmoid(gate[:, :COL_G] + prow(P_RBA, cs))
        i = jax.nn.sigmoid(gate[:, COL_G:] + prow(P_RBX, cs))
        log_a = (-RG_C * jax.nn.softplus(-prow(P_LAM, cs))) * r
        s_half = jnp.tanh(0.5 * log_a)
        inv = 1.0 / (1.0 - s_half)
        val[f"a{g}"] = (1.0 + s_half) * inv
        val[f"b{g}"] = (2.0 * jnp.sqrt(-s_half) * inv) * (i * u)

    def rg_scan(g):
        cs = slice(g * COL_G, (g + 1) * COL_G)
        h, h_last = _linear_scan(val.pop(f"a{g}"), val.pop(f"b{g}"), hr_ref[0:1, cs])
        hr_ref[0:1, cs] = h_last
        val[f"h{g}"] = h

    def rg_out(g):
        cs = slice(g * COL_G, (g + 1) * COL_G)
        yr_ref[:, cs] = (val.pop(f"h{g}") * _silu(val.pop(f"rz{g}"))).astype(BF16)

    def mm_dot(hd):
        cs = slice(hd * DH_M, (hd + 1) * DH_M)
        xn = xn_ref[...]
        val[f"mx{hd}"] = _dot(xn, w_ref[:, OFF_MX + hd * DH_M:OFF_MX + (hd + 1) * DH_M])
        mo_ref[:, cs] = _dot(xn, w_ref[:, OFF_MO + hd * DH_M:OFF_MO + (hd + 1) * DH_M])

    def mm_conv(hd):
        cs = slice(hd * DH_M, (hd + 1) * DH_M)
        mx = val.pop(f"mx{hd}")
        c = _silu(_causal_conv(mxb_ref, cs, mx, [prow(P_MCW + j, cs) for j in range(CONV_W)],
                               prow(P_MCB, cs)))
        c_ref[:, cs] = c
        val[f"cb{hd}"] = c.astype(BF16)
        val[f"mxb{hd}"] = mx.astype(BF16)

    def mm_qkv(hd):
        cs = slice(hd * DH_M, (hd + 1) * DH_M)
        qk = _dot(val.pop(f"cb{hd}"), wqk_ref[hd])
        q_ref[:, cs] = qk[:, :DH_M].astype(BF16)
        k_ref[:, cs] = qk[:, DH_M:].astype(BF16)
        v_ref[:, cs] = _dot(val.pop(f"mxb{hd}"), wv_ref[hd]).astype(BF16)

    for g in range(N_G):
        for nm, fn in (("rg_dot", rg_dot), ("rg_conv", rg_conv), ("rg_gdot", rg_gdot),
                       ("rg_gates", rg_gates), ("rg_scan", rg_scan), ("rg_out", rg_out),
                       ("mm_dot", mm_dot), ("mm_conv", mm_conv), ("mm_qkv", mm_qkv)):
            steps[f"{nm}{g}"] = functools.partial(fn, g)

    @step("gv_dot")
    def _():
        val["pre_if"] = _dot(xn_ref[...], wif_ref[...])

    @step("gv_z")
    def _():
        pre_if = val.pop("pre_if") + p_ref[P_BIF:P_BIF + 1, 0:LANES]
        lane = lax.broadcasted_iota(jnp.int32, pre_if.shape, 1)
        val["z"] = jnp.where(lane < H_M, pre_if, jax.nn.log_sigmoid(pre_if))

    @step("gv_cum")
    def _():
        z = val["z"]
        sub = lax.broadcasted_iota(jnp.int32, (SUBLANES, LANES), 0)
        bcs = []
        for ck in range(n_ck):
            bc = z[ck * CHUNK:(ck + 1) * CHUNK, :].reshape(SEG, SUBLANES, LANES)
            s = 1
            while s < SEG:
                bc = jnp.concatenate([bc[:s], bc[s:] + bc[:-s]], axis=0)
                s *= 2
            seg_sum = bc[SEG - 1]
            s = 1
            while s < SUBLANES:
                seg_sum = jnp.where(sub >= s, seg_sum + pltpu.roll(seg_sum, s, 0), seg_sum)
                s *= 2
            before = jnp.where(sub >= 1, pltpu.roll(seg_sum, 1, 0), 0.0)
            bcs.append((bc + before).reshape(CHUNK, LANES))
        val["bc"] = bcs

    @step("gv_vec")
    def _():
        z_all = val.pop("z")
        bcs = val.pop("bc")
        sub = lax.broadcasted_iota(jnp.int32, (SUBLANES, LANES), 0)
        m_row = mst_ref[...]
        for ck in range(n_ck):
            z = z_all[ck * CHUNK:(ck + 1) * CHUNK, :]
            bc = bcs[ck]
            r = pltpu.roll(z, H_M, 1) - bc
            cm = r.reshape(SEG, SUBLANES, LANES)
            s = 1
            while s < SEG:
                cm = jnp.concatenate([cm[:s], jnp.maximum(cm[s:], cm[:-s])], axis=0)
                s *= 2
            seg_max = cm[SEG - 1]
            s = 1
            while s < SUBLANES:
                seg_max = jnp.where(sub >= s, jnp.maximum(seg_max, pltpu.roll(seg_max, s, 0)), seg_max)
                s *= 2
            before = jnp.where(sub >= 1, pltpu.roll(seg_max, 1, 0), -jnp.inf)
            cm = jnp.maximum(cm, before).reshape(CHUNK, LANES)
            mj = jnp.maximum(m_row, cm)
            mj_last = mj[CHUNK - 1:CHUNK, :]
            val[f"gt{ck}"] = dict(
                rt=r.T,
                mj=mj,
                e_inter=jnp.exp(m_row - mj),
                e_negm=jnp.exp(-(bc + mj)),
                wg=jnp.exp(r - mj_last),
                decay=jnp.exp(m_row - mj_last))
            m_row = bc[CHUNK - 1:CHUNK, :] + mj_last
        mst_ref[...] = m_row

    def ck_a(hd, ck):
        cs = slice(hd * DH_M, (hd + 1) * DH_M)
        rows = slice(ck * CHUNK, (ck + 1) * CHUNK)
        if ck == 0:
            val[f"C{hd}"] = cst_ref[hd]
            val[f"n{hd}"] = nst_ref[hd:hd + 1, :]
        q = q_ref[rows, cs]
        val[f"qk{hd}{ck}"] = lax.dot_general(q, k_ref[rows, cs], (((1,), (1,)), ((), ())),
                                            preferred_element_type=F32)
        val[f"qc{hd}{ck}"] = _dot(q, val[f"C{hd}"].astype(BF16))

    def ck_b(hd, ck):
        ln = slice(H_M + hd, H_M + hd + 1)
        gt = val[f"gt{ck}"]
        causal = causal_ref[...] > 0.5
        w = jnp.where(causal, jnp.exp(gt["rt"][ln, :] - gt["mj"][:, ln]), 0.0)
        s_qk = val.pop(f"qk{hd}{ck}") * w
        val[f"rowsum{hd}{ck}"] = jnp.sum(s_qk, axis=-1, keepdims=True)
        val[f"s{hd}{ck}"] = s_qk.astype(BF16)

    def ck_c(hd, ck):
        cs = slice(hd * DH_M, (hd + 1) * DH_M)
        rows = slice(ck * CHUNK, (ck + 1) * CHUNK)
        val[f"sv{hd}{ck}"] = _dot(val.pop(f"s{hd}{ck}"), v_ref[rows, cs])

    def ck_d(hd, ck):
        cs = slice(hd * DH_M, (hd + 1) * DH_M)
        rows = slice(ck * CHUNK, (ck + 1) * CHUNK)
        ln = slice(H_M + hd, H_M + hd + 1)
        gt = val[f"gt{ck}"]
        e_inter = gt["e_inter"][:, ln]
        num = val.pop(f"sv{hd}{ck}") + e_inter * val.pop(f"qc{hd}{ck}")
        den = val.pop(f"rowsum{hd}{ck}") + \
            e_inter * jnp.sum(q_ref[rows, cs].astype(F32) * val[f"n{hd}"], axis=-1, keepdims=True)
        hm_ref[rows, cs] = num * (1.0 / jnp.maximum(jnp.abs(den), gt["e_negm"][:, ln]))
        kw = k_ref[rows, cs].astype(F32) * gt["wg"][:, ln]
        val[f"n{hd}"] = gt["decay"][:, ln] * val[f"n{hd}"] + jnp.sum(kw, axis=0, keepdims=True)
        val[f"kw{hd}{ck}"] = kw.astype(BF16)

    def ck_e(hd, ck):
        cs = slice(hd * DH_M, (hd + 1) * DH_M)
        rows = slice(ck * CHUNK, (ck + 1) * CHUNK)
        val[f"kv{hd}{ck}"] = lax.dot_general(val.pop(f"kw{hd}{ck}"), v_ref[rows, cs],
                                            (((0,), (0,)), ((), ())), preferred_element_type=F32)

    def ck_f(hd, ck):
        ln = slice(H_M + hd, H_M + hd + 1)
        c_new = val[f"gt{ck}"]["decay"][:, ln] * val.pop(f"C{hd}") + val.pop(f"kv{hd}{ck}")
        if ck + 1 < n_ck:
            val[f"C{hd}"] = c_new
        else:
            cst_ref[hd] = c_new
            nst_ref[hd:hd + 1, :] = val.pop(f"n{hd}")

    for hd in range(H_M):
        for ck in range(n_ck):
            for nm, fn in (("ck_a", ck_a), ("ck_b", ck_b), ("ck_c", ck_c), ("ck_d", ck_d),
                           ("ck_e", ck_e), ("ck_f", ck_f)):
                steps[f"{nm}{hd}{ck}"] = functools.partial(fn, hd, ck)

    def epilogue(hd):
        cs = slice(hd * DH_M, (hd + 1) * DH_M)
        hg = jax.nn.sigmoid(mo_ref[:, cs]) * hm_ref[:, cs]
        mu = jnp.mean(hg, axis=-1, keepdims=True)
        dlt = hg - mu
        var = jnp.mean(dlt * dlt, axis=-1, keepdims=True)
        hn = dlt * lax.rsqrt(var + EPS) * prow(P_NORMG, cs) + prow(P_SKIP, cs) * c_ref[:, cs]
        ym_ref[:, cs] = (hn * _silu(mz_ref[:, cs])).astype(BF16)

    for hd in range(H_M):
        steps[f"epi{hd}"] = functools.partial(epilogue, hd)

    def br(j):
        val[f"br{j}"] = _dot(yr_ref[...], wbr_ref[:, j * PAIR:(j + 1) * PAIR])

    def bm(j):
        val[f"bm{j}"] = _dot(ym_ref[...], wbm_ref[:, j * PAIR:(j + 1) * PAIR])

    def merge(j):
        cs = slice(j * PAIR, (j + 1) * PAIR)
        y = jax.nn.sigmoid(gr_ref[:, cs]) * val.pop(f"br{j}") + \
            jax.nn.sigmoid(gm_ref[:, cs]) * val.pop(f"bm{j}")
        y_ref[:, cs] = y.astype(BF16)

    for j in range(N_G // 2):
        steps[f"br{j}"] = functools.partial(br, j)
        steps[f"bm{j}"] = functools.partial(bm, j)
        steps[f"merge{j}"] = functools.partial(merge, j)

    @step("unperm")
    def _():
        val["y_time"] = _dot(unperm_ref[...], y_ref[...]).astype(BF16)

    @step("out")
    def _():
        val["out"] = _dot(val.pop("y_time"), wout_ref[:, :D_MODEL])

    @step("post")
    def _():
        out = val.pop("out")
        ms_o = jnp.mean(out * out, axis=-1, keepdims=True)
        o_ref[...] = x_ref[...] + out * lax.rsqrt(ms_o + EPS) * p_ref[P_GPOST:P_GPOST + 1, :]

    done = set()
    for name in ORDER:
        assert all(d in done for d in STEP_DEPS[name]), (name, STEP_DEPS[name])
        steps[name]()
        done.add(name)
    assert done == set(steps), set(steps) - done


def _block_diag(w, group):
    h, d, e = w.shape
    w = w.reshape(h // group, group, d, e)
    eye = jnp.eye(group, dtype=w.dtype)
    return jnp.einsum("gide,ij->gidje", w, eye).reshape(h // group, group * d, group * e)


def _row_order_constants():
    p = np.arange(TILE_T)
    q = p % CHUNK
    time_of = (p // CHUNK) * CHUNK + (q % SUBLANES) * SEG + q // SUBLANES
    perm = np.zeros((TILE_T, TILE_T), np.float32)
    perm[p, time_of] = 1.0
    t = time_of[:CHUNK]
    causal_p = (t[None, :] <= t[:, None]).astype(np.float32)
    return jnp.asarray(perm, BF16), jnp.asarray(causal_p, F32)


def _layer(x, g_pre, w_in, rg_conv_w, rg_conv_b, rg_w_a, rg_b_a, rg_w_x, rg_b_x, rg_lambda,
           ml_conv_w, ml_conv_b, ml_w_q, ml_w_k, ml_w_v, ml_b_i, ml_b_f, ml_norm_g, ml_skip,
           w_branch_r, w_branch_m, w_out, g_post):
    bsz, seq, d = x.shape
    assert d == D_MODEL and seq % TILE_T == 0 and TILE_T % CHUNK == 0

    n_if = 2 * H_M
    o_i = 5 * D_MODEL
    def odd_width(w):
        tiles = -(-w.shape[-1] // LANES)
        return jnp.pad(w, ((0, 0), (0, (tiles + 1 - tiles % 2) * LANES - w.shape[-1])))

    w_main = w_in.astype(BF16)
    assert -(-w_main.shape[-1] // LANES) % 2 == 1
    w_gate = odd_width(w_in[:, o_i + n_if:].astype(BF16))
    w_if = jnp.pad(w_in[:, o_i:o_i + n_if], ((0, 0), (0, LANES - n_if))).astype(BF16)

    group = COL_G // DH_R
    wax = jnp.concatenate([_block_diag(rg_w_a, group), _block_diag(rg_w_x, group)], axis=2).astype(BF16)
    wqk = jnp.concatenate([ml_w_q, ml_w_k * (1.0 / math.sqrt(DH_M))], axis=2).astype(BF16)

    b_if = jnp.pad(jnp.concatenate([ml_b_i, ml_b_f]), (0, D_MODEL - n_if))
    rows = [g_pre[None], rg_conv_w, rg_conv_b[None], rg_b_a[None], rg_b_x[None], rg_lambda[None],
            ml_conv_w, ml_conv_b[None], ml_norm_g[None], ml_skip[None], g_post[None], b_if[None]]
    params = jnp.concatenate(rows, axis=0).astype(F32)
    params = jnp.pad(params, ((0, P_ROWS - params.shape[0]), (0, 0)))

    perm, causal_p = _row_order_constants()

    resident = pl.BlockSpec(memory_space=pltpu.VMEM)
    tile = pl.BlockSpec((None, TILE_T, D_MODEL), lambda b, t: (b, t, 0))
    slab_f32 = pltpu.VMEM((TILE_T, D_MODEL), F32)
    slab_bf16 = pltpu.VMEM((TILE_T, D_MODEL), BF16)
    conv_tail = pltpu.VMEM(((CONV_W - 1) * SUBLANES, D_MODEL), F32)
    scratch = [
        slab_bf16,
        conv_tail, conv_tail,
        slab_bf16,
        slab_f32,
        slab_bf16, slab_bf16, slab_bf16,
        slab_f32,
        slab_bf16,
        slab_bf16,
        slab_f32, slab_f32, slab_f32, slab_f32,
        pltpu.VMEM((SUBLANES, D_MODEL), F32),
        pltpu.VMEM((H_M, DH_M, DH_M), F32),
        pltpu.VMEM((SUBLANES, DH_M), F32),
        pltpu.VMEM((1, LANES), F32),
    ]
    return pl.pallas_call(
        _block_kernel,
        out_shape=jax.ShapeDtypeStruct(x.shape, x.dtype),
        grid=(bsz, seq // TILE_T),
        in_specs=[tile] + [resident] * 13,
        out_specs=tile,
        scratch_shapes=scratch,
        compiler_params=pltpu.CompilerParams(
            dimension_semantics=("arbitrary", "arbitrary"),
            vmem_limit_bytes=VMEM_LIMIT_BYTES),
        name="hybrid_block",
    )(x, w_main, w_gate, w_if, wax, wqk, ml_w_v.astype(BF16),
      odd_width(w_branch_r.astype(BF16)), odd_width(w_branch_m.astype(BF16)),
      odd_width(w_out.astype(BF16)), params,
      perm, perm.T, causal_p)


def kernel(x, g_pre, w_in, rg_conv_w, rg_conv_b, rg_w_a, rg_b_a, rg_w_x, rg_b_x, rg_lambda,
           ml_conv_w, ml_conv_b, ml_w_q, ml_w_k, ml_w_v, ml_b_i, ml_b_f, ml_norm_g, ml_skip,
           w_branch_r, w_branch_m, w_out, g_post):
    h = x
    for l in range(g_pre.shape[0]):
        h = _layer(h, g_pre[l], w_in[l], rg_conv_w[l], rg_conv_b[l], rg_w_a[l], rg_b_a[l],
                   rg_w_x[l], rg_b_x[l], rg_lambda[l], ml_conv_w[l], ml_conv_b[l],
                   ml_w_q[l], ml_w_k[l], ml_w_v[l], ml_b_i[l], ml_b_f[l], ml_norm_g[l],
                   ml_skip[l], w_branch_r[l], w_branch_m[l], w_out[l], g_post[l])
    return h
```

```python
import functools
import math

import jax
import jax.numpy as jnp
import numpy as np
from jax import lax
from jax.experimental import pallas as pl
from jax.experimental.pallas import tpu as pltpu

D_MODEL = 1024
H_R = 16
DH_R = D_MODEL // H_R
RG_C = 8.0
CONV_W = 4
H_M = 4
DH_M = D_MODEL // H_M
EPS = 1e-6

TILE_T = 256
CHUNK = TILE_T
COL_G = 256
N_G = D_MODEL // COL_G
SUBLANES = 8
LANES = 128
SEG = CHUNK // SUBLANES
VMEM_LIMIT_BYTES = 56 * 1024 * 1024

PAIR = 2 * COL_G
OFF_RX, OFF_RZ, OFF_MX, OFF_MZ, OFF_MO = (j * D_MODEL for j in range(5))
OFF_GR, OFF_GM = 0, D_MODEL

(P_GPRE, P_RCW, P_RCB, P_RBA, P_RBX, P_LAM, P_MCW, P_MCB, P_NORMG, P_SKIP, P_GPOST, P_BIF) = (
    0, 1, 5, 6, 7, 8, 9, 13, 14, 15, 16, 17)
P_ROWS = 24

F32 = jnp.float32
BF16 = jnp.bfloat16


def _dot(a, b):
    return jnp.dot(a, b, preferred_element_type=F32)


def _silu(x):
    return x * jax.nn.sigmoid(x)


def _step_deps():
    d = {"pre": ()}
    for g in range(N_G):
        d[f"fill{g}"] = ("pre",)
        d[f"rg_dot{g}"] = ("pre",)
        d[f"rg_conv{g}"] = (f"rg_dot{g}",)
        d[f"rg_gdot{g}"] = (f"rg_conv{g}",)
        d[f"rg_gates{g}"] = (f"rg_gdot{g}",)
        d[f"rg_scan{g}"] = (f"rg_gates{g}",)
        d[f"rg_out{g}"] = (f"rg_scan{g}",)
        d[f"mm_dot{g}"] = ("pre",)
        d[f"mm_conv{g}"] = (f"mm_dot{g}",)
        d[f"mm_qkv{g}"] = (f"mm_conv{g}",)
        d[f"ck_k{g}"] = (f"mm_qkv{g}", "gv_vec")
        d[f"ck_a{g}"] = (f"ck_k{g}",)
        d[f"ck_b{g}"] = (f"ck_a{g}",)
        d[f"ck_c{g}"] = (f"ck_b{g}",)
        d[f"ck_d{g}"] = (f"ck_c{g}",)
        d[f"ck_f{g}"] = (f"ck_a{g}",)
        d[f"epi{g}"] = (f"ck_d{g}", f"mm_dot{g}", f"mm_conv{g}", f"fill{g}")
    for j in range(N_G // 2):
        d[f"fill{N_G + j}"] = ("pre",)
        d[f"br{j}"] = tuple(f"rg_out{g}" for g in range(N_G))
        d[f"bm{j}"] = tuple(f"epi{g}" for g in range(N_G))
        d[f"merge{j}"] = (f"br{j}", f"bm{j}", f"fill{2 * j}", f"fill{2 * j + 1}", f"fill{N_G + j}")
    d["gv_dot"] = ("pre",)
    d["gv_z"] = ("gv_dot",)
    d["gv_cum"] = ("gv_z",)
    d["gv_vec"] = ("gv_cum",)
    d["unperm"] = tuple(f"merge{j}" for j in range(N_G // 2))
    d["out"] = ("unperm",)
    d["post"] = ("out",)
    return d


STEP_DEPS = _step_deps()


def _issue_order():
    o = ["pre", "mm_dot0", "gv_dot", "mm_dot1", "mm_conv0"]
    gate_vec = {1: "gv_z", 2: "gv_cum"}
    for hd in range(1, H_M):
        o += [f"mm_dot{hd + 1}" if hd + 1 < H_M else "rg_dot0", f"mm_qkv{hd - 1}"]
        o += [gate_vec[hd]] if hd == 1 else []
        o += [f"mm_conv{hd}"]
        o += [gate_vec[hd]] if hd == 2 else []
    o += [f"mm_qkv{H_M - 1}", "gv_vec"]
    fills = [f"fill{i}" for i in range(N_G + N_G // 2)]
    for hd in range(H_M):
        o += [f"ck_k{hd}", f"ck_a{hd}"] + ([f"rg_dot{hd + 1}"] if hd + 1 < H_M else [])
        o += [f"rg_conv{hd}", f"ck_b{hd}", f"rg_gdot{hd}", f"ck_c{hd}", fills.pop(0)]
        o += [f"ck_d{hd}", f"rg_gates{hd}", f"ck_f{hd}", f"rg_scan{hd}", f"rg_out{hd}"]
        if 0 < hd < H_M - 1:
            o.append(fills.pop(0))
        if hd == H_M - 1:
            o += [f"br{j}" for j in range(N_G // 2)]
        o.append(f"epi{hd}")
    o += [f"bm{j}" for j in range(N_G // 2)] + [f"merge{j}" for j in range(N_G // 2)]
    o += ["unperm", "out", "post"]
    return o


ORDER = _issue_order()


def _linear_scan(a, b, h0):
    tn, c = a.shape
    a3 = a.reshape(tn // SUBLANES, SUBLANES, c)
    b3 = b.reshape(tn // SUBLANES, SUBLANES, c)
    sub = lax.broadcasted_iota(jnp.int32, (SUBLANES, c), 0)
    hs = []
    carry = h0
    for ck in range(tn // CHUNK):
        base = ck * SEG
        h = b3[base]
        p = a3[base]
        h_loc, p_loc = [h], [p]
        for r in range(1, SEG):
            h = a3[base + r] * h + b3[base + r]
            p = a3[base + r] * p
            h_loc.append(h)
            p_loc.append(p)
        s = 1
        while s < SUBLANES:
            keep = sub >= s
            h = jnp.where(keep, p * pltpu.roll(h, s, 0) + h, h)
            p = jnp.where(keep, p * pltpu.roll(p, s, 0), p)
            s *= 2
        end_state = p * carry + h
        seg_in = jnp.where(sub >= 1, pltpu.roll(end_state, 1, 0), carry)
        for r in range(SEG):
            hs.append(h_loc[r] + p_loc[r] * seg_in)
        carry = end_state[SUBLANES - 1:SUBLANES, :]
    return jnp.stack(hs, axis=0).reshape(tn, c), carry


def _causal_conv(tail_ref, cs, y, w_rows, bias):
    tn, c = y.shape
    n_tail = CONV_W - 1
    y3 = y.reshape(tn // SUBLANES, SUBLANES, c)
    first = lax.broadcasted_iota(jnp.int32, (n_tail, SUBLANES, c), 1) == 0
    prev_last = tail_ref[:, cs].reshape(n_tail, SUBLANES, c)
    outs = []
    for ck in range(tn // CHUNK):
        cur = y3[ck * SEG:(ck + 1) * SEG]
        wrapped = jnp.where(first, pltpu.roll(prev_last, 1, 1), pltpu.roll(cur[SEG - n_tail:], 1, 1))
        acc = bias + w_rows[CONV_W - 1] * cur
        for shift in range(1, CONV_W):
            shifted = jnp.concatenate([wrapped[n_tail - shift:], cur[:SEG - shift]], axis=0)
            acc = acc + w_rows[CONV_W - 1 - shift] * shifted
        outs.append(acc)
        prev_last = cur[SEG - n_tail:]
    tail_ref[:, cs] = prev_last.reshape(n_tail * SUBLANES, c)
    return jnp.concatenate(outs, axis=0).reshape(tn, c)


def _block_kernel(x_ref, w_ref, wg_ref, wif_ref, wax_ref, wqk_ref, wv_ref, wbr_ref, wbm_ref, wout_ref,
                  p_ref, perm_ref, unperm_ref, causal_ref, o_ref,
                  xn_ref, rxb_ref, mxb_ref, yr_ref, c_ref, q_ref, k_ref, v_ref, hm_ref,
                  ym_ref, y_ref, gr_ref, gm_ref, mo_ref, mz_ref,
                  hr_ref, cst_ref, nst_ref, mst_ref):
    tn = x_ref.shape[0]

    @pl.when(pl.program_id(1) == 0)
    def _reset_state():
        rxb_ref[...] = jnp.zeros_like(rxb_ref)
        mxb_ref[...] = jnp.zeros_like(mxb_ref)
        hr_ref[...] = jnp.zeros_like(hr_ref)
        cst_ref[...] = jnp.zeros_like(cst_ref)
        nst_ref[...] = jnp.zeros_like(nst_ref)
        mst_ref[...] = jnp.zeros_like(mst_ref)

    def prow(r, cs):
        return p_ref[r:r + 1, cs]

    val = {}
    steps = {}

    def step(name):
        def register(fn):
            steps[name] = fn
            return fn
        return register

    @step("pre")
    def _():
        x = x_ref[...]
        ms = jnp.mean(x * x, axis=-1, keepdims=True)
        xn_time = (x * lax.rsqrt(ms + EPS) * p_ref[P_GPRE:P_GPRE + 1, :]).astype(BF16)
        xn_ref[...] = _dot(perm_ref[...], xn_time).astype(BF16)

    def fill_mz_gr(g):
        cs = slice(g * COL_G, (g + 1) * COL_G)
        xn = xn_ref[...]
        mz_ref[:, cs] = _dot(xn, w_ref[:, OFF_MZ + g * COL_G:OFF_MZ + (g + 1) * COL_G])
        gr_ref[:, cs] = _dot(xn, wg_ref[:, OFF_GR + g * COL_G:OFF_GR + (g + 1) * COL_G])

    def fill_gm(j):
        gm_ref[:, j * PAIR:(j + 1) * PAIR] = _dot(xn_ref[...], wg_ref[:, OFF_GM + j * PAIR:OFF_GM + (j + 1) * PAIR])

    for g in range(N_G):
        steps[f"fill{g}"] = functools.partial(fill_mz_gr, g)
    for j in range(N_G // 2):
        steps[f"fill{N_G + j}"] = functools.partial(fill_gm, j)

    def rg_dot(g):
        xn = xn_ref[...]
        val[f"rx{g}"] = _dot(xn, w_ref[:, OFF_RX + g * COL_G:OFF_RX + (g + 1) * COL_G])
        val[f"rz{g}"] = _dot(xn, w_ref[:, OFF_RZ + g * COL_G:OFF_RZ + (g + 1) * COL_G])

    def rg_conv(g):
        cs = slice(g * COL_G, (g + 1) * COL_G)
        u = _causal_conv(rxb_ref, cs, val.pop(f"rx{g}"), [prow(P_RCW + j, cs) for j in range(CONV_W)],
                         prow(P_RCB, cs))
        val[f"u{g}"] = u
        val[f"ub{g}"] = u.astype(BF16)

    def rg_gdot(g):
        val[f"gate{g}"] = _dot(val.pop(f"ub{g}"), wax_ref[g])

    def rg_gates(g):
        cs = slice(g * COL_G, (g + 1) * COL_G)
        u = val.pop(f"u{g}")
        gate = val.pop(f"gate{g}")
        r = jax.nn.sigmoid(gate[:, :COL_G] + prow(P_RBA, cs))
        i = jax.nn.sigmoid(gate[:, COL_G:] + prow(P_RBX, cs))
        log_a = (-RG_C * jax.nn.softplus(-prow(P_LAM, cs))) * r
        s_half = jnp.tanh(0.5 * log_a)
        inv = 1.0 / (1.0 - s_half)
        val[f"a{g}"] = (1.0 + s_half) * inv
        val[f"b{g}"] = (2.0 * jnp.sqrt(-s_half) * inv) * (i * u)

    def rg_scan(g):
        cs = slice(g * COL_G, (g + 1) * COL_G)
        h, h_last = _linear_scan(val.pop(f"a{g}"), val.pop(f"b{g}"), hr_ref[0:1, cs])
        hr_ref[0:1, cs] = h_last
        val[f"h{g}"] = h

    def rg_out(g):
        cs = slice(g * COL_G, (g + 1) * COL_G)
        yr_ref[:, cs] = (val.pop(f"h{g}") * _silu(val.pop(f"rz{g}"))).astype(BF16)

    def mm_dot(hd):
        cs = slice(hd * DH_M, (hd + 1) * DH_M)
        xn = xn_ref[...]
        val[f"mx{hd}"] = _dot(xn, w_ref[:, OFF_MX + hd * DH_M:OFF_MX + (hd + 1) * DH_M])
        mo_ref[:, cs] = _dot(xn, w_ref[:, OFF_MO + hd * DH_M:OFF_MO + (hd + 1) * DH_M])

    def mm_conv(hd):
        cs = slice(hd * DH_M, (hd + 1) * DH_M)
        mx = val.pop(f"mx{hd}")
        c = _silu(_causal_conv(mxb_ref, cs, mx, [prow(P_MCW + j, cs) for j in range(CONV_W)],
                               prow(P_MCB, cs)))
        c_ref[:, cs] = c
        val[f"cb{hd}"] = c.astype(BF16)
        val[f"mxb{hd}"] = mx.astype(BF16)

    def mm_qkv(hd):
        cs = slice(hd * DH_M, (hd + 1) * DH_M)
        qk = _dot(val.pop(f"cb{hd}"), wqk_ref[hd])
        q_ref[:, cs] = qk[:, :DH_M].astype(BF16)
        k_ref[:, cs] = qk[:, DH_M:].astype(BF16)
        v_ref[:, cs] = _dot(val.pop(f"mxb{hd}"), wv_ref[hd]).astype(BF16)

    for g in range(N_G):
        for nm, fn in (("rg_dot", rg_dot), ("rg_conv", rg_conv), ("rg_gdot", rg_gdot),
                       ("rg_gates", rg_gates), ("rg_scan", rg_scan), ("rg_out", rg_out),
                       ("mm_dot", mm_dot), ("mm_conv", mm_conv), ("mm_qkv", mm_qkv)):
            steps[f"{nm}{g}"] = functools.partial(fn, g)

    @step("gv_dot")
    def _():
        val["pre_if"] = _dot(xn_ref[...], wif_ref[...])

    @step("gv_z")
    def _():
        pre_if = val.pop("pre_if") + p_ref[P_BIF:P_BIF + 1, 0:LANES]
        lane = lax.broadcasted_iota(jnp.int32, pre_if.shape, 1)
        val["z"] = jnp.where(lane < H_M, pre_if, jax.nn.log_sigmoid(pre_if))

    @step("gv_cum")
    def _():
        sub = lax.broadcasted_iota(jnp.int32, (SUBLANES, LANES), 0)
        bc = val["z"].reshape(SEG, SUBLANES, LANES)
        s = 1
        while s < SEG:
            bc = jnp.concatenate([bc[:s], bc[s:] + bc[:-s]], axis=0)
            s *= 2
        seg_sum = bc[SEG - 1]
        s = 1
        while s < SUBLANES:
            seg_sum = jnp.where(sub >= s, seg_sum + pltpu.roll(seg_sum, s, 0), seg_sum)
            s *= 2
        before = jnp.where(sub >= 1, pltpu.roll(seg_sum, 1, 0), 0.0)
        val["bc"] = (bc + before).reshape(CHUNK, LANES)

    @step("gv_vec")
    def _():
        z = val.pop("z")
        bc = val.pop("bc")
        sub = lax.broadcasted_iota(jnp.int32, (SUBLANES, LANES), 0)
        m_row = mst_ref[...]
        r = pltpu.roll(z, H_M, 1) - bc
        cm = r.reshape(SEG, SUBLANES, LANES)
        s = 1
        while s < SEG:
            cm = jnp.concatenate([cm[:s], jnp.maximum(cm[s:], cm[:-s])], axis=0)
            s *= 2
        seg_max = cm[SEG - 1]
        s = 1
        while s < SUBLANES:
            seg_max = jnp.where(sub >= s, jnp.maximum(seg_max, pltpu.roll(seg_max, s, 0)), seg_max)
            s *= 2
        before = jnp.where(sub >= 1, pltpu.roll(seg_max, 1, 0), -jnp.inf)
        cm = jnp.maximum(cm, before).reshape(CHUNK, LANES)
        mj = jnp.maximum(m_row, cm)
        mj_last = mj[CHUNK - 1:CHUNK, :]
        val["gt"] = dict(
            rt=r.T,
            mj=mj,
            e_inter=jnp.exp(m_row - mj),
            e_negm=jnp.exp(-(bc + mj)),
            wg=jnp.exp(r - mj_last),
            decay=jnp.exp(m_row - mj_last))
        mst_ref[...] = bc[CHUNK - 1:CHUNK, :] + mj_last

    def ck_k(hd):
        cs = slice(hd * DH_M, (hd + 1) * DH_M)
        ln = slice(H_M + hd, H_M + hd + 1)
        gt = val["gt"]
        kw = k_ref[:, cs].astype(F32) * gt["wg"][:, ln]
        n_old = nst_ref[hd:hd + 1, :]
        val[f"n{hd}"] = n_old
        nst_ref[hd:hd + 1, :] = gt["decay"][:, ln] * n_old + jnp.sum(kw, axis=0, keepdims=True)
        val[f"kw{hd}"] = kw.astype(BF16)

    def ck_a(hd):
        cs = slice(hd * DH_M, (hd + 1) * DH_M)
        q = q_ref[:, cs]
        val[f"qk{hd}"] = lax.dot_general(q, k_ref[:, cs], (((1,), (1,)), ((), ())),
                                        preferred_element_type=F32)
        val[f"qc{hd}"] = _dot(q, cst_ref[hd].astype(BF16))
        val[f"kv{hd}"] = lax.dot_general(val.pop(f"kw{hd}"), v_ref[:, cs], (((0,), (0,)), ((), ())),
                                        preferred_element_type=F32)

    def ck_b(hd):
        ln = slice(H_M + hd, H_M + hd + 1)
        gt = val["gt"]
        causal = causal_ref[...] > 0.5
        w = jnp.where(causal, jnp.exp(gt["rt"][ln, :] - gt["mj"][:, ln]), 0.0)
        s_qk = val.pop(f"qk{hd}") * w
        val[f"rowsum{hd}"] = jnp.sum(s_qk, axis=-1, keepdims=True)
        val[f"s{hd}"] = s_qk.astype(BF16)

    def ck_c(hd):
        val[f"sv{hd}"] = _dot(val.pop(f"s{hd}"), v_ref[:, hd * DH_M:(hd + 1) * DH_M])

    def ck_d(hd):
        cs = slice(hd * DH_M, (hd + 1) * DH_M)
        ln = slice(H_M + hd, H_M + hd + 1)
        gt = val["gt"]
        e_inter = gt["e_inter"][:, ln]
        num = val.pop(f"sv{hd}") + e_inter * val.pop(f"qc{hd}")
        den = val.pop(f"rowsum{hd}") + \
            e_inter * jnp.sum(q_ref[:, cs].astype(F32) * val.pop(f"n{hd}"), axis=-1, keepdims=True)
        hm_ref[:, cs] = num * (1.0 / jnp.maximum(jnp.abs(den), gt["e_negm"][:, ln]))

    def ck_f(hd):
        ln = slice(H_M + hd, H_M + hd + 1)
        cst_ref[hd] = val["gt"]["decay"][:, ln] * cst_ref[hd] + val.pop(f"kv{hd}")

    for hd in range(H_M):
        for nm, fn in (("ck_k", ck_k), ("ck_a", ck_a), ("ck_b", ck_b), ("ck_c", ck_c), ("ck_d", ck_d),
                       ("ck_f", ck_f)):
            steps[f"{nm}{hd}"] = functools.partial(fn, hd)

    def epilogue(hd):
        cs = slice(hd * DH_M, (hd + 1) * DH_M)
        hg = jax.nn.sigmoid(mo_ref[:, cs]) * hm_ref[:, cs]
        mu = jnp.mean(hg, axis=-1, keepdims=True)
        dlt = hg - mu
        var = jnp.mean(dlt * dlt, axis=-1, keepdims=True)
        hn = dlt * lax.rsqrt(var + EPS) * prow(P_NORMG, cs) + prow(P_SKIP, cs) * c_ref[:, cs]
        ym_ref[:, cs] = (hn * _silu(mz_ref[:, cs])).astype(BF16)

    for hd in range(H_M):
        steps[f"epi{hd}"] = functools.partial(epilogue, hd)

    def br(j):
        val[f"br{j}"] = _dot(yr_ref[...], wbr_ref[:, j * PAIR:(j + 1) * PAIR])

    def bm(j):
        val[f"bm{j}"] = _dot(ym_ref[...], wbm_ref[:, j * PAIR:(j + 1) * PAIR])

    def merge(j):
        cs = slice(j * PAIR, (j + 1) * PAIR)
        y = jax.nn.sigmoid(gr_ref[:, cs]) * val.pop(f"br{j}") + \
            jax.nn.sigmoid(gm_ref[:, cs]) * val.pop(f"bm{j}")
        y_ref[:, cs] = y.astype(BF16)

    for j in range(N_G // 2):
        steps[f"br{j}"] = functools.partial(br, j)
        steps[f"bm{j}"] = functools.partial(bm, j)
        steps[f"merge{j}"] = functools.partial(merge, j)

    @step("unperm")
    def _():
        val["y_time"] = _dot(unperm_ref[...], y_ref[...]).astype(BF16)

    @step("out")
    def _():
        val["out"] = _dot(val.pop("y_time"), wout_ref[:, :D_MODEL])

    @step("post")
    def _():
        out = val.pop("out")
        ms_o = jnp.mean(out * out, axis=-1, keepdims=True)
        o_ref[...] = x_ref[...] + out * lax.rsqrt(ms_o + EPS) * p_ref[P_GPOST:P_GPOST + 1, :]

    done = set()
    for name in ORDER:
        assert all(d in done for d in STEP_DEPS[name]), (name, STEP_DEPS[name])
        steps[name]()
        done.add(name)
    assert done == set(steps), set(steps) - done


def _block_diag(w, group):
    h, d, e = w.shape
    w = w.reshape(h // group, group, d, e)
    eye = jnp.eye(group, dtype=w.dtype)
    return jnp.einsum("gide,ij->gidje", w, eye).reshape(h // group, group * d, group * e)


def _row_order_constants():
    p = np.arange(TILE_T)
    q = p % CHUNK
    time_of = (p // CHUNK) * CHUNK + (q % SUBLANES) * SEG + q // SUBLANES
    perm = np.zeros((TILE_T, TILE_T), np.float32)
    perm[p, time_of] = 1.0
    t = time_of[:CHUNK]
    causal_p = (t[None, :] <= t[:, None]).astype(np.float32)
    return jnp.asarray(perm, BF16), jnp.asarray(causal_p, F32)


def _layer(x, g_pre, w_in, rg_conv_w, rg_conv_b, rg_w_a, rg_b_a, rg_w_x, rg_b_x, rg_lambda,
           ml_conv_w, ml_conv_b, ml_w_q, ml_w_k, ml_w_v, ml_b_i, ml_b_f, ml_norm_g, ml_skip,
           w_branch_r, w_branch_m, w_out, g_post):
    bsz, seq, d = x.shape
    assert d == D_MODEL and seq % TILE_T == 0

    n_if = 2 * H_M
    o_i = 5 * D_MODEL
    def odd_width(w):
        tiles = -(-w.shape[-1] // LANES)
        return jnp.pad(w, ((0, 0), (0, (tiles + 1 - tiles % 2) * LANES - w.shape[-1])))

    w_main = w_in.astype(BF16)
    assert -(-w_main.shape[-1] // LANES) % 2 == 1
    w_gate = odd_width(w_in[:, o_i + n_if:].astype(BF16))
    w_if = jnp.pad(w_in[:, o_i:o_i + n_if], ((0, 0), (0, LANES - n_if))).astype(BF16)

    group = COL_G // DH_R
    wax = jnp.concatenate([_block_diag(rg_w_a, group), _block_diag(rg_w_x, group)], axis=2).astype(BF16)
    wqk = jnp.concatenate([ml_w_q, ml_w_k * (1.0 / math.sqrt(DH_M))], axis=2).astype(BF16)

    b_if = jnp.pad(jnp.concatenate([ml_b_i, ml_b_f]), (0, D_MODEL - n_if))
    rows = [g_pre[None], rg_conv_w, rg_conv_b[None], rg_b_a[None], rg_b_x[None], rg_lambda[None],
            ml_conv_w, ml_conv_b[None], ml_norm_g[None], ml_skip[None], g_post[None], b_if[None]]
    params = jnp.concatenate(rows, axis=0).astype(F32)
    params = jnp.pad(params, ((0, P_ROWS - params.shape[0]), (0, 0)))

    perm, causal_p = _row_order_constants()

    resident = pl.BlockSpec(memory_space=pltpu.VMEM)
    tile = pl.BlockSpec((None, TILE_T, D_MODEL), lambda b, t: (b, t, 0))
    slab_f32 = pltpu.VMEM((TILE_T, D_MODEL), F32)
    slab_bf16 = pltpu.VMEM((TILE_T, D_MODEL), BF16)
    conv_tail = pltpu.VMEM(((CONV_W - 1) * SUBLANES, D_MODEL), F32)
    scratch = [
        slab_bf16,
        conv_tail, conv_tail,
        slab_bf16,
        slab_f32,
        slab_bf16, slab_bf16, slab_bf16,
        slab_f32,
        slab_bf16,
        slab_bf16,
        slab_f32, slab_f32, slab_f32, slab_f32,
        pltpu.VMEM((SUBLANES, D_MODEL), F32),
        pltpu.VMEM((H_M, DH_M, DH_M), F32),
        pltpu.VMEM((SUBLANES, DH_M), F32),
        pltpu.VMEM((1, LANES), F32),
    ]
    return pl.pallas_call(
        _block_kernel,
        out_shape=jax.ShapeDtypeStruct(x.shape, x.dtype),
        grid=(bsz, seq // TILE_T),
        in_specs=[tile] + [resident] * 13,
        out_specs=tile,
        scratch_shapes=scratch,
        compiler_params=pltpu.CompilerParams(
            dimension_semantics=("arbitrary", "arbitrary"),
            vmem_limit_bytes=VMEM_LIMIT_BYTES),
        name="hybrid_block",
    )(x, w_main, w_gate, w_if, wax, wqk, ml_w_v.astype(BF16),
      odd_width(w_branch_r.astype(BF16)), odd_width(w_branch_m.astype(BF16)),
      odd_width(w_out.astype(BF16)), params,
      perm, perm.T, causal_p)


def kernel(x, g_pre, w_in, rg_conv_w, rg_conv_b, rg_w_a, rg_b_a, rg_w_x, rg_b_x, rg_lambda,
           ml_conv_w, ml_conv_b, ml_w_q, ml_w_k, ml_w_v, ml_b_i, ml_b_f, ml_norm_g, ml_skip,
           w_branch_r, w_branch_m, w_out, g_post):
    h = x
    for l in range(g_pre.shape[0]):
        h = _layer(h, g_pre[l], w_in[l], rg_conv_w[l], rg_conv_b[l], rg_w_a[l], rg_b_a[l],
                   rg_w_x[l], rg_b_x[l], rg_lambda[l], ml_conv_w[l], ml_conv_b[l],
                   ml_w_q[l], ml_w_k[l], ml_w_v[l], ml_b_i[l], ml_b_f[l], ml_norm_g[l],
                   ml_skip[l], w_branch_r[l], w_branch_m[l], w_out[l], g_post[l])
    return h
```

```python
import functools
import math

import jax
import jax.numpy as jnp
import numpy as np
from jax import lax
from jax.experimental import pallas as pl
from jax.experimental.pallas import tpu as pltpu

D_MODEL = 1024
H_R = 16
DH_R = D_MODEL // H_R
RG_C = 8.0
CONV_W = 4
H_M = 4
DH_M = D_MODEL // H_M
EPS = 1e-6

TILE_T = 256
CHUNK = TILE_T
COL_G = 256
N_G = D_MODEL // COL_G
SUBLANES = 8
LANES = 128
SEG = CHUNK // SUBLANES
VMEM_LIMIT_BYTES = 56 * 1024 * 1024

PAIR = 2 * COL_G
OFF_RX, OFF_RZ, OFF_MX, OFF_MZ, OFF_MO = (j * D_MODEL for j in range(5))
OFF_GR, OFF_GM = 0, D_MODEL
OFF_WBR, OFF_WBM, OFF_WOUT = 0, D_MODEL, 2 * D_MODEL

(P_GPRE, P_RCW, P_RCB, P_RBA, P_RBX, P_LAM, P_MCW, P_MCB, P_NORMG, P_SKIP, P_GPOST, P_BIF) = (
    0, 1, 5, 6, 7, 8, 9, 13, 14, 15, 16, 17)
P_ROWS = 24

F32 = jnp.float32
BF16 = jnp.bfloat16


def _dot(a, b):
    return jnp.dot(a, b, preferred_element_type=F32)


def _silu(x):
    return x * jax.nn.sigmoid(x)


def _step_deps():
    d = {"pre": ()}
    for g in range(N_G):
        d[f"fill{g}"] = ("pre",)
        d[f"rg_dot{g}"] = ("pre",)
        d[f"rg_conv{g}"] = (f"rg_dot{g}",)
        d[f"rg_gdot{g}"] = (f"rg_conv{g}",)
        d[f"rg_gates{g}"] = (f"rg_gdot{g}",)
        d[f"rg_scan{g}"] = (f"rg_gates{g}",)
        d[f"rg_out{g}"] = (f"rg_scan{g}",)
        d[f"mm_dot{g}"] = ("pre",)
        d[f"mm_conv{g}"] = (f"mm_dot{g}",)
        d[f"mm_qkv{g}"] = (f"mm_conv{g}",)
        d[f"ck_k{g}"] = (f"mm_qkv{g}", "gv_vec")
        d[f"ck_a{g}"] = (f"ck_k{g}",)
        d[f"ck_b{g}"] = (f"ck_a{g}",)
        d[f"ck_c{g}"] = (f"ck_b{g}",)
        d[f"ck_d{g}"] = (f"ck_c{g}",)
        d[f"ck_f{g}"] = (f"ck_a{g}",)
        d[f"epi{g}"] = (f"ck_d{g}", f"mm_dot{g}", f"mm_conv{g}", f"fill{g}")
    for j in range(N_G // 2):
        d[f"fill{N_G + j}"] = ("pre",)
        d[f"br{j}"] = tuple(f"rg_out{g}" for g in range(N_G))
        d[f"bm{j}"] = tuple(f"epi{g}" for g in range(N_G))
        d[f"merge{j}"] = (f"br{j}", f"bm{j}", f"fill{2 * j}", f"fill{2 * j + 1}", f"fill{N_G + j}")
    d["gv_dot"] = ("pre",)
    d["gv_z"] = ("gv_dot",)
    d["gv_cum"] = ("gv_z",)
    d["gv_vec"] = ("gv_cum",)
    d["unperm"] = tuple(f"merge{j}" for j in range(N_G // 2))
    d["out"] = ("unperm",)
    d["post"] = ("out",)
    return d


STEP_DEPS = _step_deps()


def _issue_order():
    o = ["pre", "mm_dot0", "gv_dot", "mm_dot1", "mm_conv0"]
    gate_vec = {1: "gv_z", 2: "gv_cum"}
    for hd in range(1, H_M):
        o += [f"mm_dot{hd + 1}" if hd + 1 < H_M else "rg_dot0", f"mm_qkv{hd - 1}"]
        o += [gate_vec[hd]] if hd == 1 else []
        o += [f"mm_conv{hd}"]
        o += [gate_vec[hd]] if hd == 2 else []
    o += [f"mm_qkv{H_M - 1}", "gv_vec"]
    fills = [f"fill{i}" for i in range(N_G + N_G // 2)]
    for hd in range(H_M):
        o += [f"ck_k{hd}", f"ck_a{hd}"] + ([f"rg_dot{hd + 1}"] if hd + 1 < H_M else [])
        o += [f"rg_conv{hd}", f"ck_b{hd}", f"rg_gdot{hd}", f"ck_c{hd}", fills.pop(0)]
        o += [f"ck_d{hd}", f"rg_gates{hd}", f"ck_f{hd}", f"rg_scan{hd}", f"rg_out{hd}"]
        if 0 < hd < H_M - 1:
            o.append(fills.pop(0))
        if hd == H_M - 1:
            o += [f"br{j}" for j in range(N_G // 2)]
        o.append(f"epi{hd}")
    o += [f"bm{j}" for j in range(N_G // 2)] + [f"merge{j}" for j in range(N_G // 2)]
    o += ["unperm", "out", "post"]
    return o


ORDER = _issue_order()


def _linear_scan(a, b, h0):
    tn, c = a.shape
    a3 = a.reshape(tn // SUBLANES, SUBLANES, c)
    b3 = b.reshape(tn // SUBLANES, SUBLANES, c)
    sub = lax.broadcasted_iota(jnp.int32, (SUBLANES, c), 0)
    hs = []
    carry = h0
    for ck in range(tn // CHUNK):
        base = ck * SEG
        h = b3[base]
        p = a3[base]
        h_loc, p_loc = [h], [p]
        for r in range(1, SEG):
            h = a3[base + r] * h + b3[base + r]
            p = a3[base + r] * p
            h_loc.append(h)
            p_loc.append(p)
        s = 1
        while s < SUBLANES:
            keep = sub >= s
            h = jnp.where(keep, p * pltpu.roll(h, s, 0) + h, h)
            p = jnp.where(keep, p * pltpu.roll(p, s, 0), p)
            s *= 2
        end_state = p * carry + h
        seg_in = jnp.where(sub >= 1, pltpu.roll(end_state, 1, 0), carry)
        for r in range(SEG):
            hs.append(h_loc[r] + p_loc[r] * seg_in)
        carry = end_state[SUBLANES - 1:SUBLANES, :]
    return jnp.stack(hs, axis=0).reshape(tn, c), carry


def _causal_conv(tail_ref, cs, y, w_rows, bias):
    tn, c = y.shape
    n_tail = CONV_W - 1
    y3 = y.reshape(tn // SUBLANES, SUBLANES, c)
    first = lax.broadcasted_iota(jnp.int32, (n_tail, SUBLANES, c), 1) == 0
    prev_last = tail_ref[:, cs].reshape(n_tail, SUBLANES, c)
    outs = []
    for ck in range(tn // CHUNK):
        cur = y3[ck * SEG:(ck + 1) * SEG]
        wrapped = jnp.where(first, pltpu.roll(prev_last, 1, 1), pltpu.roll(cur[SEG - n_tail:], 1, 1))
        acc = bias + w_rows[CONV_W - 1] * cur
        for shift in range(1, CONV_W):
            shifted = jnp.concatenate([wrapped[n_tail - shift:], cur[:SEG - shift]], axis=0)
            acc = acc + w_rows[CONV_W - 1 - shift] * shifted
        outs.append(acc)
        prev_last = cur[SEG - n_tail:]
    tail_ref[:, cs] = prev_last.reshape(n_tail * SUBLANES, c)
    return jnp.concatenate(outs, axis=0).reshape(tn, c)


def _block_kernel(x_ref, w_ref, wg_ref, wif_ref, wax_ref, wqk_ref, wv_ref, wt_ref,
                  p_ref, perm_ref, unperm_ref, causal_ref, o_ref,
                  xn_ref, rxb_ref, mxb_ref, yr_ref, c_ref, q_ref, k_ref, v_ref, hm_ref,
                  ym_ref, y_ref, gr_ref, gm_ref, mo_ref, mz_ref,
                  hr_ref, cst_ref, nst_ref, mst_ref):
    tn = x_ref.shape[0]

    @pl.when(pl.program_id(1) == 0)
    def _reset_state():
        rxb_ref[...] = jnp.zeros_like(rxb_ref)
        mxb_ref[...] = jnp.zeros_like(mxb_ref)
        hr_ref[...] = jnp.zeros_like(hr_ref)
        cst_ref[...] = jnp.zeros_like(cst_ref)
        nst_ref[...] = jnp.zeros_like(nst_ref)
        mst_ref[...] = jnp.zeros_like(mst_ref)

    def prow(r, cs):
        return p_ref[r:r + 1, cs]

    val = {}
    steps = {}

    def step(name):
        def register(fn):
            steps[name] = fn
            return fn
        return register

    @step("pre")
    def _():
        x = x_ref[...]
        ms = jnp.mean(x * x, axis=-1, keepdims=True)
        xn_time = (x * lax.rsqrt(ms + EPS) * p_ref[P_GPRE:P_GPRE + 1, :]).astype(BF16)
        xn_ref[...] = _dot(perm_ref[...], xn_time).astype(BF16)

    def fill_mz_gr(g):
        cs = slice(g * COL_G, (g + 1) * COL_G)
        xn = xn_ref[...]
        mz_ref[:, cs] = _dot(xn, w_ref[:, OFF_MZ + g * COL_G:OFF_MZ + (g + 1) * COL_G])
        gr_ref[:, cs] = _dot(xn, wg_ref[:, OFF_GR + g * COL_G:OFF_GR + (g + 1) * COL_G])

    def fill_gm(j):
        gm_ref[:, j * PAIR:(j + 1) * PAIR] = _dot(xn_ref[...], wg_ref[:, OFF_GM + j * PAIR:OFF_GM + (j + 1) * PAIR])

    for g in range(N_G):
        steps[f"fill{g}"] = functools.partial(fill_mz_gr, g)
    for j in range(N_G // 2):
        steps[f"fill{N_G + j}"] = functools.partial(fill_gm, j)

    def rg_dot(g):
        xn = xn_ref[...]
        val[f"rx{g}"] = _dot(xn, w_ref[:, OFF_RX + g * COL_G:OFF_RX + (g + 1) * COL_G])
        val[f"rz{g}"] = _dot(xn, w_ref[:, OFF_RZ + g * COL_G:OFF_RZ + (g + 1) * COL_G])

    def rg_conv(g):
        cs = slice(g * COL_G, (g + 1) * COL_G)
        u = _causal_conv(rxb_ref, cs, val.pop(f"rx{g}"), [prow(P_RCW + j, cs) for j in range(CONV_W)],
                         prow(P_RCB, cs))
        val[f"u{g}"] = u
        val[f"ub{g}"] = u.astype(BF16)

    def rg_gdot(g):
        val[f"gate{g}"] = _dot(val.pop(f"ub{g}"), wax_ref[g])

    def rg_gates(g):
        cs = slice(g * COL_G, (g + 1) * COL_G)
        u = val.pop(f"u{g}")
        gate = val.pop(f"gate{g}")
        r = jax.nn.sigmoid(gate[:, :COL_G] + prow(P_RBA, cs))
        i = jax.nn.sigmoid(gate[:, COL_G:] + prow(P_RBX, cs))
        log_a = (-RG_C * jax.nn.softplus(-prow(P_LAM, cs))) * r
        s_half = jnp.tanh(0.5 * log_a)
        inv = 1.0 / (1.0 - s_half)
        val[f"a{g}"] = (1.0 + s_half) * inv
        val[f"b{g}"] = (2.0 * jnp.sqrt(-s_half) * inv) * (i * u)

    def rg_scan(g):
        cs = slice(g * COL_G, (g + 1) * COL_G)
        h, h_last = _linear_scan(val.pop(f"a{g}"), val.pop(f"b{g}"), hr_ref[0:1, cs])
        hr_ref[0:1, cs] = h_last
        val[f"h{g}"] = h

    def rg_out(g):
        cs = slice(g * COL_G, (g + 1) * COL_G)
        yr_ref[:, cs] = (val.pop(f"h{g}") * _silu(val.pop(f"rz{g}"))).astype(BF16)

    def mm_dot(hd):
        cs = slice(hd * DH_M, (hd + 1) * DH_M)
        xn = xn_ref[...]
        val[f"mx{hd}"] = _dot(xn, w_ref[:, OFF_MX + hd * DH_M:OFF_MX + (hd + 1) * DH_M])
        mo_ref[:, cs] = _dot(xn, w_ref[:, OFF_MO + hd * DH_M:OFF_MO + (hd + 1) * DH_M])

    def mm_conv(hd):
        cs = slice(hd * DH_M, (hd + 1) * DH_M)
        mx = val.pop(f"mx{hd}")
        c = _silu(_causal_conv(mxb_ref, cs, mx, [prow(P_MCW + j, cs) for j in range(CONV_W)],
                               prow(P_MCB, cs)))
        c_ref[:, cs] = c
        val[f"cb{hd}"] = c.astype(BF16)
        val[f"mxb{hd}"] = mx.astype(BF16)

    def mm_qkv(hd):
        cs = slice(hd * DH_M, (hd + 1) * DH_M)
        qk = _dot(val.pop(f"cb{hd}"), wqk_ref[hd])
        q_ref[:, cs] = qk[:, :DH_M].astype(BF16)
        k_ref[:, cs] = qk[:, DH_M:].astype(BF16)
        v_ref[:, cs] = _dot(val.pop(f"mxb{hd}"), wv_ref[hd]).astype(BF16)

    for g in range(N_G):
        for nm, fn in (("rg_dot", rg_dot), ("rg_conv", rg_conv), ("rg_gdot", rg_gdot),
                       ("rg_gates", rg_gates), ("rg_scan", rg_scan), ("rg_out", rg_out),
                       ("mm_dot", mm_dot), ("mm_conv", mm_conv), ("mm_qkv", mm_qkv)):
            steps[f"{nm}{g}"] = functools.partial(fn, g)

    @step("gv_dot")
    def _():
        val["pre_if"] = _dot(xn_ref[...], wif_ref[...])

    @step("gv_z")
    def _():
        pre_if = val.pop("pre_if") + p_ref[P_BIF:P_BIF + 1, 0:LANES]
        lane = lax.broadcasted_iota(jnp.int32, pre_if.shape, 1)
        val["z"] = jnp.where(lane < H_M, pre_if, jax.nn.log_sigmoid(pre_if))

    @step("gv_cum")
    def _():
        sub = lax.broadcasted_iota(jnp.int32, (SUBLANES, LANES), 0)
        bc = val["z"].reshape(SEG, SUBLANES, LANES)
        s = 1
        while s < SEG:
            bc = jnp.concatenate([bc[:s], bc[s:] + bc[:-s]], axis=0)
            s *= 2
        seg_sum = bc[SEG - 1]
        s = 1
        while s < SUBLANES:
            seg_sum = jnp.where(sub >= s, seg_sum + pltpu.roll(seg_sum, s, 0), seg_sum)
            s *= 2
        before = jnp.where(sub >= 1, pltpu.roll(seg_sum, 1, 0), 0.0)
        val["bc"] = (bc + before).reshape(CHUNK, LANES)

    @step("gv_vec")
    def _():
        z = val.pop("z")
        bc = val.pop("bc")
        sub = lax.broadcasted_iota(jnp.int32, (SUBLANES, LANES), 0)
        m_row = mst_ref[...]
        r = pltpu.roll(z, H_M, 1) - bc
        cm = r.reshape(SEG, SUBLANES, LANES)
        s = 1
        while s < SEG:
            cm = jnp.concatenate([cm[:s], jnp.maximum(cm[s:], cm[:-s])], axis=0)
            s *= 2
        seg_max = cm[SEG - 1]
        s = 1
        while s < SUBLANES:
            seg_max = jnp.where(sub >= s, jnp.maximum(seg_max, pltpu.roll(seg_max, s, 0)), seg_max)
            s *= 2
        before = jnp.where(sub >= 1, pltpu.roll(seg_max, 1, 0), -jnp.inf)
        cm = jnp.maximum(cm, before).reshape(CHUNK, LANES)
        mj = jnp.maximum(m_row, cm)
        mj_last = mj[CHUNK - 1:CHUNK, :]
        val["gt"] = dict(
            rt=r.T,
            mj=mj,
            e_inter=jnp.exp(m_row - mj),
            e_negm=jnp.exp(-(bc + mj)),
            wg=jnp.exp(r - mj_last),
            decay=jnp.exp(m_row - mj_last))
        mst_ref[...] = bc[CHUNK - 1:CHUNK, :] + mj_last

    def ck_k(hd):
        cs = slice(hd * DH_M, (hd + 1) * DH_M)
        ln = slice(H_M + hd, H_M + hd + 1)
        gt = val["gt"]
        kw = k_ref[:, cs].astype(F32) * gt["wg"][:, ln]
        n_old = nst_ref[hd:hd + 1, :]
        val[f"n{hd}"] = n_old
        nst_ref[hd:hd + 1, :] = gt["decay"][:, ln] * n_old + jnp.sum(kw, axis=0, keepdims=True)
        val[f"kw{hd}"] = kw.astype(BF16)

    def ck_a(hd):
        cs = slice(hd * DH_M, (hd + 1) * DH_M)
        q = q_ref[:, cs]
        val[f"qk{hd}"] = lax.dot_general(q, k_ref[:, cs], (((1,), (1,)), ((), ())),
                                        preferred_element_type=F32)
        val[f"qc{hd}"] = _dot(q, cst_ref[hd].astype(BF16))
        val[f"kv{hd}"] = lax.dot_general(val.pop(f"kw{hd}"), v_ref[:, cs], (((0,), (0,)), ((), ())),
                                        preferred_element_type=F32)

    def ck_b(hd):
        ln = slice(H_M + hd, H_M + hd + 1)
        gt = val["gt"]
        causal = causal_ref[...] > 0.5
        w = jnp.where(causal, jnp.exp(gt["rt"][ln, :] - gt["mj"][:, ln]), 0.0)
        s_qk = val.pop(f"qk{hd}") * w
        val[f"rowsum{hd}"] = jnp.sum(s_qk, axis=-1, keepdims=True)
        val[f"s{hd}"] = s_qk.astype(BF16)

    def ck_c(hd):
        val[f"sv{hd}"] = _dot(val.pop(f"s{hd}"), v_ref[:, hd * DH_M:(hd + 1) * DH_M])

    def ck_d(hd):
        cs = slice(hd * DH_M, (hd + 1) * DH_M)
        ln = slice(H_M + hd, H_M + hd + 1)
        gt = val["gt"]
        e_inter = gt["e_inter"][:, ln]
        num = val.pop(f"sv{hd}") + e_inter * val.pop(f"qc{hd}")
        den = val.pop(f"rowsum{hd}") + \
            e_inter * jnp.sum(q_ref[:, cs].astype(F32) * val.pop(f"n{hd}"), axis=-1, keepdims=True)
        hm_ref[:, cs] = num * (1.0 / jnp.maximum(jnp.abs(den), gt["e_negm"][:, ln]))

    def ck_f(hd):
        ln = slice(H_M + hd, H_M + hd + 1)
        cst_ref[hd] = val["gt"]["decay"][:, ln] * cst_ref[hd] + val.pop(f"kv{hd}")

    for hd in range(H_M):
        for nm, fn in (("ck_k", ck_k), ("ck_a", ck_a), ("ck_b", ck_b), ("ck_c", ck_c), ("ck_d", ck_d),
                       ("ck_f", ck_f)):
            steps[f"{nm}{hd}"] = functools.partial(fn, hd)

    def epilogue(hd):
        cs = slice(hd * DH_M, (hd + 1) * DH_M)
        hg = jax.nn.sigmoid(mo_ref[:, cs]) * hm_ref[:, cs]
        mu = jnp.mean(hg, axis=-1, keepdims=True)
        dlt = hg - mu
        var = jnp.mean(dlt * dlt, axis=-1, keepdims=True)
        hn = dlt * lax.rsqrt(var + EPS) * prow(P_NORMG, cs) + prow(P_SKIP, cs) * c_ref[:, cs]
        ym_ref[:, cs] = (hn * _silu(mz_ref[:, cs])).astype(BF16)

    for hd in range(H_M):
        steps[f"epi{hd}"] = functools.partial(epilogue, hd)

    def br(j):
        val[f"br{j}"] = _dot(yr_ref[...], wt_ref[:, OFF_WBR + j * PAIR:OFF_WBR + (j + 1) * PAIR])

    def bm(j):
        val[f"bm{j}"] = _dot(ym_ref[...], wt_ref[:, OFF_WBM + j * PAIR:OFF_WBM + (j + 1) * PAIR])

    def merge(j):
        cs = slice(j * PAIR, (j + 1) * PAIR)
        y = jax.nn.sigmoid(gr_ref[:, cs]) * val.pop(f"br{j}") + \
            jax.nn.sigmoid(gm_ref[:, cs]) * val.pop(f"bm{j}")
        y_ref[:, cs] = y.astype(BF16)

    for j in range(N_G // 2):
        steps[f"br{j}"] = functools.partial(br, j)
        steps[f"bm{j}"] = functools.partial(bm, j)
        steps[f"merge{j}"] = functools.partial(merge, j)

    @step("unperm")
    def _():
        val["y_time"] = _dot(unperm_ref[...], y_ref[...]).astype(BF16)

    @step("out")
    def _():
        val["out"] = _dot(val.pop("y_time"), wt_ref[:, OFF_WOUT:OFF_WOUT + D_MODEL])

    @step("post")
    def _():
        out = val.pop("out")
        ms_o = jnp.mean(out * out, axis=-1, keepdims=True)
        o_ref[...] = x_ref[...] + out * lax.rsqrt(ms_o + EPS) * p_ref[P_GPOST:P_GPOST + 1, :]

    done = set()
    for name in ORDER:
        assert all(d in done for d in STEP_DEPS[name]), (name, STEP_DEPS[name])
        steps[name]()
        done.add(name)
    assert done == set(steps), set(steps) - done


def _block_diag(w, group):
    h, d, e = w.shape
    across = jnp.tile(w.reshape(h // group, group, d, e), (1, 1, 1, group))
    on_diagonal = np.arange(group * e)[None, :] // e == np.arange(group)[:, None]
    return jnp.where(on_diagonal[None, :, None, :], across, 0.0).reshape(h // group, group * d, group * e)


def _row_order_constants():
    p = np.arange(TILE_T)
    q = p % CHUNK
    time_of = (p // CHUNK) * CHUNK + (q % SUBLANES) * SEG + q // SUBLANES
    perm = np.zeros((TILE_T, TILE_T), np.float32)
    perm[p, time_of] = 1.0
    t = time_of[:CHUNK]
    causal_p = (t[None, :] <= t[:, None]).astype(np.float32)
    return jnp.asarray(perm, BF16), jnp.asarray(causal_p, F32)


def _layer(x, g_pre, w_in, rg_conv_w, rg_conv_b, rg_w_a, rg_b_a, rg_w_x, rg_b_x, rg_lambda,
           ml_conv_w, ml_conv_b, ml_w_q, ml_w_k, ml_w_v, ml_b_i, ml_b_f, ml_norm_g, ml_skip,
           w_branch_r, w_branch_m, w_out, g_post):
    bsz, seq, d = x.shape
    assert d == D_MODEL and seq % TILE_T == 0

    n_if = 2 * H_M
    o_i = 5 * D_MODEL
    def odd_width(*ws):
        width = sum(w.shape[-1] for w in ws)
        tiles = -(-width // LANES)
        zeros = jnp.zeros((ws[0].shape[0], (tiles + 1 - tiles % 2) * LANES - width), ws[0].dtype)
        return jnp.concatenate(list(ws) + [zeros], axis=1).astype(BF16)

    w_main = w_in.astype(BF16)
    assert -(-w_main.shape[-1] // LANES) % 2 == 1
    w_gate = odd_width(w_in[:, o_i + n_if:])
    w_if = odd_width(w_in[:, o_i:o_i + n_if])
    w_tail = odd_width(w_branch_r, w_branch_m, w_out)

    group = COL_G // DH_R
    wax = jnp.concatenate([_block_diag(rg_w_a, group), _block_diag(rg_w_x, group)], axis=2).astype(BF16)
    wqk = jnp.concatenate([ml_w_q, ml_w_k * (1.0 / math.sqrt(DH_M))], axis=2).astype(BF16)

    b_if = jnp.concatenate([ml_b_i, ml_b_f, jnp.zeros((D_MODEL - n_if,), F32)])
    rows = [g_pre[None], rg_conv_w, rg_conv_b[None], rg_b_a[None], rg_b_x[None], rg_lambda[None],
            ml_conv_w, ml_conv_b[None], ml_norm_g[None], ml_skip[None], g_post[None], b_if[None]]
    n_rows = sum(r.shape[0] for r in rows)
    params = jnp.concatenate(rows + [jnp.zeros((P_ROWS - n_rows, D_MODEL), F32)], axis=0)

    perm, causal_p = _row_order_constants()

    resident = pl.BlockSpec(memory_space=pltpu.VMEM)
    tile = pl.BlockSpec((None, TILE_T, D_MODEL), lambda b, t: (b, t, 0))
    slab_f32 = pltpu.VMEM((TILE_T, D_MODEL), F32)
    slab_bf16 = pltpu.VMEM((TILE_T, D_MODEL), BF16)
    conv_tail = pltpu.VMEM(((CONV_W - 1) * SUBLANES, D_MODEL), F32)
    scratch = [
        slab_bf16,
        conv_tail, conv_tail,
        slab_bf16,
        slab_f32,
        slab_bf16, slab_bf16, slab_bf16,
        slab_f32,
        slab_bf16,
        slab_bf16,
        slab_f32, slab_f32, slab_f32, slab_f32,
        pltpu.VMEM((SUBLANES, D_MODEL), F32),
        pltpu.VMEM((H_M, DH_M, DH_M), F32),
        pltpu.VMEM((SUBLANES, DH_M), F32),
        pltpu.VMEM((1, LANES), F32),
    ]
    return pl.pallas_call(
        _block_kernel,
        out_shape=jax.ShapeDtypeStruct(x.shape, x.dtype),
        grid=(bsz, seq // TILE_T),
        in_specs=[tile] + [resident] * 11,
        out_specs=tile,
        scratch_shapes=scratch,
        compiler_params=pltpu.CompilerParams(
            dimension_semantics=("arbitrary", "arbitrary"),
            vmem_limit_bytes=VMEM_LIMIT_BYTES),
        name="hybrid_block",
    )(x, w_main, w_gate, w_if, wax, wqk, ml_w_v.astype(BF16),
      w_tail, params,
      perm, perm.T, causal_p)


def kernel(x, g_pre, w_in, rg_conv_w, rg_conv_b, rg_w_a, rg_b_a, rg_w_x, rg_b_x, rg_lambda,
           ml_conv_w, ml_conv_b, ml_w_q, ml_w_k, ml_w_v, ml_b_i, ml_b_f, ml_norm_g, ml_skip,
           w_branch_r, w_branch_m, w_out, g_post):
    h = x
    for l in range(g_pre.shape[0]):
        h = _layer(h, g_pre[l], w_in[l], rg_conv_w[l], rg_conv_b[l], rg_w_a[l], rg_b_a[l],
                   rg_w_x[l], rg_b_x[l], rg_lambda[l], ml_conv_w[l], ml_conv_b[l],
                   ml_w_q[l], ml_w_k[l], ml_w_v[l], ml_b_i[l], ml_b_f[l], ml_norm_g[l],
                   ml_skip[l], w_branch_r[l], w_branch_m[l], w_out[l], g_post[l])
    return h
```

```python
import functools
import math

import jax
import jax.numpy as jnp
import numpy as np
from jax import lax
from jax.experimental import pallas as pl
from jax.experimental.pallas import tpu as pltpu

D_MODEL = 1024
H_R = 16
DH_R = D_MODEL // H_R
RG_C = 8.0
CONV_W = 4
H_M = 4
DH_M = D_MODEL // H_M
EPS = 1e-6

TILE_T = 256
CHUNK = TILE_T
COL_G = 256
N_G = D_MODEL // COL_G
SUBLANES = 8
LANES = 128
SEG = CHUNK // SUBLANES
VMEM_LIMIT_BYTES = 56 * 1024 * 1024

PAIR = 2 * COL_G
OFF_RX, OFF_RZ, OFF_MX, OFF_MZ, OFF_MO = (j * D_MODEL for j in range(5))
OFF_GR, OFF_GM = 0, D_MODEL
OFF_WBR, OFF_WBM, OFF_WOUT = 0, D_MODEL, 2 * D_MODEL

(P_GPRE, P_RCW, P_RCB, P_RBA, P_RBX, P_LAM, P_MCW, P_MCB, P_NORMG, P_SKIP, P_GPOST, P_BIF) = (
    0, 1, 5, 6, 7, 8, 9, 13, 14, 15, 16, 17)
P_ROWS = 24

F32 = jnp.float32
BF16 = jnp.bfloat16


def _dot(a, b):
    return jnp.dot(a, b, preferred_element_type=F32)


def _silu(x):
    return x * jax.nn.sigmoid(x)


def _step_deps():
    d = {"pre": ()}
    for g in range(N_G):
        d[f"fill{g}"] = ("pre",)
        d[f"rg_dot{g}"] = ("pre",)
        d[f"rg_conv{g}"] = (f"rg_dot{g}",)
        d[f"rg_gdot{g}"] = (f"rg_conv{g}",)
        d[f"rg_gates{g}"] = (f"rg_gdot{g}",)
        d[f"rg_scan{g}"] = (f"rg_gates{g}",)
        d[f"rg_out{g}"] = (f"rg_scan{g}",)
        d[f"mm_dot{g}"] = ("pre",)
        d[f"mm_conv{g}"] = (f"mm_dot{g}",)
        d[f"mm_qkv{g}"] = (f"mm_conv{g}",)
        d[f"ck_k{g}"] = (f"mm_qkv{g}", "gv_vec")
        d[f"ck_a{g}"] = (f"ck_k{g}",)
        d[f"ck_b{g}"] = (f"ck_a{g}",)
        d[f"ck_c{g}"] = (f"ck_b{g}",)
        d[f"ck_d{g}"] = (f"ck_c{g}",)
        d[f"ck_f{g}"] = (f"ck_a{g}",)
        d[f"epi{g}"] = (f"ck_d{g}", f"mm_dot{g}", f"mm_conv{g}", f"fill{g}")
    for j in range(N_G // 2):
        d[f"fill{N_G + j}"] = ("pre",)
        d[f"br{j}"] = tuple(f"rg_out{g}" for g in range(N_G))
        d[f"bm{j}"] = tuple(f"epi{g}" for g in range(N_G))
        d[f"merge{j}"] = (f"br{j}", f"bm{j}", f"fill{2 * j}", f"fill{2 * j + 1}", f"fill{N_G + j}")
    d["gv_dot"] = ("pre",)
    d["gv_z"] = ("gv_dot",)
    d["gv_cum"] = ("gv_z",)
    d["gv_vec"] = ("gv_cum",)
    d["out"] = ()
    d["post"] = ("out",)
    d["keep_x"] = ("pre", "post")
    d["unperm"] = tuple(f"merge{j}" for j in range(N_G // 2)) + ("out",)
    return d


STEP_DEPS = _step_deps()


def _issue_order():
    o = ["out", "pre", "mm_dot0", "post", "keep_x", "gv_dot", "mm_dot1", "mm_conv0"]
    gate_vec = {1: "gv_z", 2: "gv_cum"}
    for hd in range(1, H_M):
        o += [f"mm_dot{hd + 1}" if hd + 1 < H_M else "rg_dot0", f"mm_qkv{hd - 1}"]
        o += [gate_vec[hd]] if hd == 1 else []
        o += [f"mm_conv{hd}"]
        o += [gate_vec[hd]] if hd == 2 else []
    o += [f"mm_qkv{H_M - 1}", "gv_vec"]
    fills = [f"fill{i}" for i in range(N_G + N_G // 2)]
    for hd in range(H_M):
        o += [f"ck_k{hd}", f"ck_a{hd}"] + ([f"rg_dot{hd + 1}"] if hd + 1 < H_M else [])
        o += [f"rg_conv{hd}", f"ck_b{hd}", f"rg_gdot{hd}", f"ck_c{hd}", fills.pop(0)]
        o += [f"ck_d{hd}", f"rg_gates{hd}", f"ck_f{hd}", f"rg_scan{hd}", f"rg_out{hd}"]
        if 0 < hd < H_M - 1:
            o.append(fills.pop(0))
        if hd == H_M - 1:
            o += [f"br{j}" for j in range(N_G // 2)]
        o.append(f"epi{hd}")
    o += [f"bm{j}" for j in range(N_G // 2)] + [f"merge{j}" for j in range(N_G // 2)]
    o += ["unperm"]
    return o


ORDER = _issue_order()


def _linear_scan(a, b, h0):
    tn, c = a.shape
    a3 = a.reshape(tn // SUBLANES, SUBLANES, c)
    b3 = b.reshape(tn // SUBLANES, SUBLANES, c)
    sub = lax.broadcasted_iota(jnp.int32, (SUBLANES, c), 0)
    hs = []
    carry = h0
    for ck in range(tn // CHUNK):
        base = ck * SEG
        h = b3[base]
        p = a3[base]
        h_loc, p_loc = [h], [p]
        for r in range(1, SEG):
            h = a3[base + r] * h + b3[base + r]
            p = a3[base + r] * p
            h_loc.append(h)
            p_loc.append(p)
        s = 1
        while s < SUBLANES:
            keep = sub >= s
            h = jnp.where(keep, p * pltpu.roll(h, s, 0) + h, h)
            p = jnp.where(keep, p * pltpu.roll(p, s, 0), p)
            s *= 2
        end_state = p * carry + h
        seg_in = jnp.where(sub >= 1, pltpu.roll(end_state, 1, 0), carry)
        for r in range(SEG):
            hs.append(h_loc[r] + p_loc[r] * seg_in)
        carry = end_state[SUBLANES - 1:SUBLANES, :]
    return jnp.stack(hs, axis=0).reshape(tn, c), carry


def _causal_conv(tail_ref, cs, y, w_rows, bias):
    tn, c = y.shape
    n_tail = CONV_W - 1
    y3 = y.reshape(tn // SUBLANES, SUBLANES, c)
    first = lax.broadcasted_iota(jnp.int32, (n_tail, SUBLANES, c), 1) == 0
    prev_last = tail_ref[:, cs].reshape(n_tail, SUBLANES, c)
    outs = []
    for ck in range(tn // CHUNK):
        cur = y3[ck * SEG:(ck + 1) * SEG]
        wrapped = jnp.where(first, pltpu.roll(prev_last, 1, 1), pltpu.roll(cur[SEG - n_tail:], 1, 1))
        acc = bias + w_rows[CONV_W - 1] * cur
        for shift in range(1, CONV_W):
            shifted = jnp.concatenate([wrapped[n_tail - shift:], cur[:SEG - shift]], axis=0)
            acc = acc + w_rows[CONV_W - 1 - shift] * shifted
        outs.append(acc)
        prev_last = cur[SEG - n_tail:]
    tail_ref[:, cs] = prev_last.reshape(n_tail * SUBLANES, c)
    return jnp.concatenate(outs, axis=0).reshape(tn, c)


def _block_kernel(x_ref, w_ref, wg_ref, wif_ref, wax_ref, wqk_ref, wv_ref, wt_ref,
                  p_ref, perm_ref, unperm_ref, causal_ref, o_ref,
                  xn_ref, rxb_ref, mxb_ref, yr_ref, c_ref, q_ref, k_ref, v_ref, hm_ref,
                  ym_ref, y_ref, gr_ref, gm_ref, mo_ref, mz_ref,
                  hr_ref, cst_ref, nst_ref, mst_ref, ytime_ref, xprev_ref, *, tiles_per_seq):
    tn = x_ref.shape[0]

    @pl.when(pl.program_id(0) == 0)
    def _no_previous_tile():
        ytime_ref[...] = jnp.zeros_like(ytime_ref)
        xprev_ref[...] = jnp.zeros_like(xprev_ref)

    @pl.when(pl.program_id(0) % tiles_per_seq == 0)
    def _reset_state():
        rxb_ref[...] = jnp.zeros_like(rxb_ref)
        mxb_ref[...] = jnp.zeros_like(mxb_ref)
        hr_ref[...] = jnp.zeros_like(hr_ref)
        cst_ref[...] = jnp.zeros_like(cst_ref)
        nst_ref[...] = jnp.zeros_like(nst_ref)
        mst_ref[...] = jnp.zeros_like(mst_ref)

    def prow(r, cs):
        return p_ref[r:r + 1, cs]

    val = {}
    steps = {}

    def step(name):
        def register(fn):
            steps[name] = fn
            return fn
        return register

    @step("pre")
    def _():
        x = x_ref[...]
        ms = jnp.mean(x * x, axis=-1, keepdims=True)
        xn_time = (x * lax.rsqrt(ms + EPS) * p_ref[P_GPRE:P_GPRE + 1, :]).astype(BF16)
        xn_ref[...] = _dot(perm_ref[...], xn_time).astype(BF16)

    def fill_mz_gr(g):
        cs = slice(g * COL_G, (g + 1) * COL_G)
        xn = xn_ref[...]
        mz_ref[:, cs] = _dot(xn, w_ref[:, OFF_MZ + g * COL_G:OFF_MZ + (g + 1) * COL_G])
        gr_ref[:, cs] = _dot(xn, wg_ref[:, OFF_GR + g * COL_G:OFF_GR + (g + 1) * COL_G])

    def fill_gm(j):
        gm_ref[:, j * PAIR:(j + 1) * PAIR] = _dot(xn_ref[...], wg_ref[:, OFF_GM + j * PAIR:OFF_GM + (j + 1) * PAIR])

    for g in range(N_G):
        steps[f"fill{g}"] = functools.partial(fill_mz_gr, g)
    for j in range(N_G // 2):
        steps[f"fill{N_G + j}"] = functools.partial(fill_gm, j)

    def rg_dot(g):
        xn = xn_ref[...]
        val[f"rx{g}"] = _dot(xn, w_ref[:, OFF_RX + g * COL_G:OFF_RX + (g + 1) * COL_G])
        val[f"rz{g}"] = _dot(xn, w_ref[:, OFF_RZ + g * COL_G:OFF_RZ + (g + 1) * COL_G])

    def rg_conv(g):
        cs = slice(g * COL_G, (g + 1) * COL_G)
        u = _causal_conv(rxb_ref, cs, val.pop(f"rx{g}"), [prow(P_RCW + j, cs) for j in range(CONV_W)],
                         prow(P_RCB, cs))
        val[f"u{g}"] = u
        val[f"ub{g}"] = u.astype(BF16)

    def rg_gdot(g):
        val[f"gate{g}"] = _dot(val.pop(f"ub{g}"), wax_ref[g])

    def rg_gates(g):
        cs = slice(g * COL_G, (g + 1) * COL_G)
        u = val.pop(f"u{g}")
        gate = val.pop(f"gate{g}")
        r = jax.nn.sigmoid(gate[:, :COL_G] + prow(P_RBA, cs))
        i = jax.nn.sigmoid(gate[:, COL_G:] + prow(P_RBX, cs))
        log_a = (-RG_C * jax.nn.softplus(-prow(P_LAM, cs))) * r
        s_half = jnp.tanh(0.5 * log_a)
        inv = 1.0 / (1.0 - s_half)
        val[f"a{g}"] = (1.0 + s_half) * inv
        val[f"b{g}"] = (2.0 * jnp.sqrt(-s_half) * inv) * (i * u)

    def rg_scan(g):
        cs = slice(g * COL_G, (g + 1) * COL_G)
        h, h_last = _linear_scan(val.pop(f"a{g}"), val.pop(f"b{g}"), hr_ref[0:1, cs])
        hr_ref[0:1, cs] = h_last
        val[f"h{g}"] = h

    def rg_out(g):
        cs = slice(g * COL_G, (g + 1) * COL_G)
        yr_ref[:, cs] = (val.pop(f"h{g}") * _silu(val.pop(f"rz{g}"))).astype(BF16)

    def mm_dot(hd):
        cs = slice(hd * DH_M, (hd + 1) * DH_M)
        xn = xn_ref[...]
        val[f"mx{hd}"] = _dot(xn, w_ref[:, OFF_MX + hd * DH_M:OFF_MX + (hd + 1) * DH_M])
        mo_ref[:, cs] = _dot(xn, w_ref[:, OFF_MO + hd * DH_M:OFF_MO + (hd + 1) * DH_M])

    def mm_conv(hd):
        cs = slice(hd * DH_M, (hd + 1) * DH_M)
        mx = val.pop(f"mx{hd}")
        c = _silu(_causal_conv(mxb_ref, cs, mx, [prow(P_MCW + j, cs) for j in range(CONV_W)],
                               prow(P_MCB, cs)))
        c_ref[:, cs] = c
        val[f"cb{hd}"] = c.astype(BF16)
        val[f"mxb{hd}"] = mx.astype(BF16)

    def mm_qkv(hd):
        cs = slice(hd * DH_M, (hd + 1) * DH_M)
        qk = _dot(val.pop(f"cb{hd}"), wqk_ref[hd])
        q_ref[:, cs] = qk[:, :DH_M].astype(BF16)
        k_ref[:, cs] = qk[:, DH_M:].astype(BF16)
        v_ref[:, cs] = _dot(val.pop(f"mxb{hd}"), wv_ref[hd]).astype(BF16)

    for g in range(N_G):
        for nm, fn in (("rg_dot", rg_dot), ("rg_conv", rg_conv), ("rg_gdot", rg_gdot),
                       ("rg_gates", rg_gates), ("rg_scan", rg_scan), ("rg_out", rg_out),
                       ("mm_dot", mm_dot), ("mm_conv", mm_conv), ("mm_qkv", mm_qkv)):
            steps[f"{nm}{g}"] = functools.partial(fn, g)

    @step("gv_dot")
    def _():
        val["pre_if"] = _dot(xn_ref[...], wif_ref[...])

    @step("gv_z")
    def _():
        pre_if = val.pop("pre_if") + p_ref[P_BIF:P_BIF + 1, 0:LANES]
        lane = lax.broadcasted_iota(jnp.int32, pre_if.shape, 1)
        val["z"] = jnp.where(lane < H_M, pre_if, jax.nn.log_sigmoid(pre_if))

    @step("gv_cum")
    def _():
        sub = lax.broadcasted_iota(jnp.int32, (SUBLANES, LANES), 0)
        bc = val["z"].reshape(SEG, SUBLANES, LANES)
        s = 1
        while s < SEG:
            bc = jnp.concatenate([bc[:s], bc[s:] + bc[:-s]], axis=0)
            s *= 2
        seg_sum = bc[SEG - 1]
        s = 1
        while s < SUBLANES:
            seg_sum = jnp.where(sub >= s, seg_sum + pltpu.roll(seg_sum, s, 0), seg_sum)
            s *= 2
        before = jnp.where(sub >= 1, pltpu.roll(seg_sum, 1, 0), 0.0)
        val["bc"] = (bc + before).reshape(CHUNK, LANES)

    @step("gv_vec")
    def _():
        z = val.pop("z")
        bc = val.pop("bc")
        sub = lax.broadcasted_iota(jnp.int32, (SUBLANES, LANES), 0)
        m_row = mst_ref[...]
        r = pltpu.roll(z, H_M, 1) - bc
        cm = r.reshape(SEG, SUBLANES, LANES)
        s = 1
        while s < SEG:
            cm = jnp.concatenate([cm[:s], jnp.maximum(cm[s:], cm[:-s])], axis=0)
            s *= 2
        seg_max = cm[SEG - 1]
        s = 1
        while s < SUBLANES:
            seg_max = jnp.where(sub >= s, jnp.maximum(seg_max, pltpu.roll(seg_max, s, 0)), seg_max)
            s *= 2
        before = jnp.where(sub >= 1, pltpu.roll(seg_max, 1, 0), -jnp.inf)
        cm = jnp.maximum(cm, before).reshape(CHUNK, LANES)
        mj = jnp.maximum(m_row, cm)
        mj_last = mj[CHUNK - 1:CHUNK, :]
        val["gt"] = dict(
            rt=r.T,
            mj=mj,
            e_inter=jnp.exp(m_row - mj),
            e_negm=jnp.exp(-(bc + mj)),
            wg=jnp.exp(r - mj_last),
            decay=jnp.exp(m_row - mj_last))
        mst_ref[...] = bc[CHUNK - 1:CHUNK, :] + mj_last

    def ck_k(hd):
        cs = slice(hd * DH_M, (hd + 1) * DH_M)
        ln = slice(H_M + hd, H_M + hd + 1)
        gt = val["gt"]
        kw = k_ref[:, cs].astype(F32) * gt["wg"][:, ln]
        n_old = nst_ref[hd:hd + 1, :]
        val[f"n{hd}"] = n_old
        nst_ref[hd:hd + 1, :] = gt["decay"][:, ln] * n_old + jnp.sum(kw, axis=0, keepdims=True)
        val[f"kw{hd}"] = kw.astype(BF16)

    def ck_a(hd):
        cs = slice(hd * DH_M, (hd + 1) * DH_M)
        q = q_ref[:, cs]
        val[f"qk{hd}"] = lax.dot_general(q, k_ref[:, cs], (((1,), (1,)), ((), ())),
                                        preferred_element_type=F32)
        val[f"qc{hd}"] = _dot(q, cst_ref[hd].astype(BF16))
        val[f"kv{hd}"] = lax.dot_general(val.pop(f"kw{hd}"), v_ref[:, cs], (((0,), (0,)), ((), ())),
                                        preferred_element_type=F32)

    def ck_b(hd):
        ln = slice(H_M + hd, H_M + hd + 1)
        gt = val["gt"]
        causal = causal_ref[...] > 0.5
        w = jnp.where(causal, jnp.exp(gt["rt"][ln, :] - gt["mj"][:, ln]), 0.0)
        s_qk = val.pop(f"qk{hd}") * w
        val[f"rowsum{hd}"] = jnp.sum(s_qk, axis=-1, keepdims=True)
        val[f"s{hd}"] = s_qk.astype(BF16)

    def ck_c(hd):
        val[f"sv{hd}"] = _dot(val.pop(f"s{hd}"), v_ref[:, hd * DH_M:(hd + 1) * DH_M])

    def ck_d(hd):
        cs = slice(hd * DH_M, (hd + 1) * DH_M)
        ln = slice(H_M + hd, H_M + hd + 1)
        gt = val["gt"]
        e_inter = gt["e_inter"][:, ln]
        num = val.pop(f"sv{hd}") + e_inter * val.pop(f"qc{hd}")
        den = val.pop(f"rowsum{hd}") + \
            e_inter * jnp.sum(q_ref[:, cs].astype(F32) * val.pop(f"n{hd}"), axis=-1, keepdims=True)
        hm_ref[:, cs] = num * (1.0 / jnp.maximum(jnp.abs(den), gt["e_negm"][:, ln]))

    def ck_f(hd):
        ln = slice(H_M + hd, H_M + hd + 1)
        cst_ref[hd] = val["gt"]["decay"][:, ln] * cst_ref[hd] + val.pop(f"kv{hd}")

    for hd in range(H_M):
        for nm, fn in (("ck_k", ck_k), ("ck_a", ck_a), ("ck_b", ck_b), ("ck_c", ck_c), ("ck_d", ck_d),
                       ("ck_f", ck_f)):
            steps[f"{nm}{hd}"] = functools.partial(fn, hd)

    def epilogue(hd):
        cs = slice(hd * DH_M, (hd + 1) * DH_M)
        hg = jax.nn.sigmoid(mo_ref[:, cs]) * hm_ref[:, cs]
        mu = jnp.mean(hg, axis=-1, keepdims=True)
        dlt = hg - mu
        var = jnp.mean(dlt * dlt, axis=-1, keepdims=True)
        hn = dlt * lax.rsqrt(var + EPS) * prow(P_NORMG, cs) + prow(P_SKIP, cs) * c_ref[:, cs]
        ym_ref[:, cs] = (hn * _silu(mz_ref[:, cs])).astype(BF16)

    for hd in range(H_M):
        steps[f"epi{hd}"] = functools.partial(epilogue, hd)

    def br(j):
        val[f"br{j}"] = _dot(yr_ref[...], wt_ref[:, OFF_WBR + j * PAIR:OFF_WBR + (j + 1) * PAIR])

    def bm(j):
        val[f"bm{j}"] = _dot(ym_ref[...], wt_ref[:, OFF_WBM + j * PAIR:OFF_WBM + (j + 1) * PAIR])

    def merge(j):
        cs = slice(j * PAIR, (j + 1) * PAIR)
        y = jax.nn.sigmoid(gr_ref[:, cs]) * val.pop(f"br{j}") + \
            jax.nn.sigmoid(gm_ref[:, cs]) * val.pop(f"bm{j}")
        y_ref[:, cs] = y.astype(BF16)

    for j in range(N_G // 2):
        steps[f"br{j}"] = functools.partial(br, j)
        steps[f"bm{j}"] = functools.partial(bm, j)
        steps[f"merge{j}"] = functools.partial(merge, j)

    @step("unperm")
    def _():
        ytime_ref[...] = _dot(unperm_ref[...], y_ref[...]).astype(BF16)

    @step("out")
    def _():
        val["out"] = _dot(ytime_ref[...], wt_ref[:, OFF_WOUT:OFF_WOUT + D_MODEL])

    @step("post")
    def _():
        out = val.pop("out")
        ms_o = jnp.mean(out * out, axis=-1, keepdims=True)
        o_ref[...] = xprev_ref[...] + out * lax.rsqrt(ms_o + EPS) * p_ref[P_GPOST:P_GPOST + 1, :]

    @step("keep_x")
    def _():
        xprev_ref[...] = x_ref[...]

    done = set()
    for name in ORDER:
        assert all(d in done for d in STEP_DEPS[name]), (name, STEP_DEPS[name])
        steps[name]()
        done.add(name)
    assert done == set(steps), set(steps) - done


def _block_diag(w, group):
    h, d, e = w.shape
    across = jnp.tile(w.reshape(h // group, group, d, e), (1, 1, 1, group))
    on_diagonal = np.arange(group * e)[None, :] // e == np.arange(group)[:, None]
    return jnp.where(on_diagonal[None, :, None, :], across, 0.0).reshape(h // group, group * d, group * e)


def _row_order_constants():
    p = np.arange(TILE_T)
    q = p % CHUNK
    time_of = (p // CHUNK) * CHUNK + (q % SUBLANES) * SEG + q // SUBLANES
    perm = np.zeros((TILE_T, TILE_T), np.float32)
    perm[p, time_of] = 1.0
    t = time_of[:CHUNK]
    causal_p = (t[None, :] <= t[:, None]).astype(np.float32)
    return jnp.asarray(perm, BF16), jnp.asarray(causal_p, F32)


def _layer(x, g_pre, w_in, rg_conv_w, rg_conv_b, rg_w_a, rg_b_a, rg_w_x, rg_b_x, rg_lambda,
           ml_conv_w, ml_conv_b, ml_w_q, ml_w_k, ml_w_v, ml_b_i, ml_b_f, ml_norm_g, ml_skip,
           w_branch_r, w_branch_m, w_out, g_post):
    bsz, seq, d = x.shape
    assert d == D_MODEL and seq % TILE_T == 0

    n_if = 2 * H_M
    o_i = 5 * D_MODEL
    def odd_width(*ws):
        width = sum(w.shape[-1] for w in ws)
        tiles = -(-width // LANES)
        zeros = jnp.zeros((ws[0].shape[0], (tiles + 1 - tiles % 2) * LANES - width), ws[0].dtype)
        return jnp.concatenate(list(ws) + [zeros], axis=1).astype(BF16)

    w_main = w_in.astype(BF16)
    assert -(-w_main.shape[-1] // LANES) % 2 == 1
    w_gate = odd_width(w_in[:, o_i + n_if:])
    w_if = odd_width(w_in[:, o_i:o_i + n_if])
    w_tail = odd_width(w_branch_r, w_branch_m, w_out)

    group = COL_G // DH_R
    wax = jnp.concatenate([_block_diag(rg_w_a, group), _block_diag(rg_w_x, group)], axis=2).astype(BF16)
    wqk = jnp.concatenate([ml_w_q, ml_w_k * (1.0 / math.sqrt(DH_M))], axis=2).astype(BF16)

    b_if = jnp.concatenate([ml_b_i, ml_b_f, jnp.zeros((D_MODEL - n_if,), F32)])
    rows = [g_pre[None], rg_conv_w, rg_conv_b[None], rg_b_a[None], rg_b_x[None], rg_lambda[None],
            ml_conv_w, ml_conv_b[None], ml_norm_g[None], ml_skip[None], g_post[None], b_if[None]]
    n_rows = sum(r.shape[0] for r in rows)
    params = jnp.concatenate(rows + [jnp.zeros((P_ROWS - n_rows, D_MODEL), F32)], axis=0)

    perm, causal_p = _row_order_constants()

    nt = seq // TILE_T
    n_tiles = bsz * nt
    resident = pl.BlockSpec(memory_space=pltpu.VMEM)

    def tile_in(i):
        j = jnp.minimum(i, n_tiles - 1)
        return (j // nt, j % nt, 0)

    def tile_out(i):
        j = jnp.maximum(i - 1, 0)
        return (j // nt, j % nt, 0)

    x_tile = pl.BlockSpec((None, TILE_T, D_MODEL), tile_in)
    o_tile = pl.BlockSpec((None, TILE_T, D_MODEL), tile_out)
    slab_f32 = pltpu.VMEM((TILE_T, D_MODEL), F32)
    slab_bf16 = pltpu.VMEM((TILE_T, D_MODEL), BF16)
    conv_tail = pltpu.VMEM(((CONV_W - 1) * SUBLANES, D_MODEL), F32)
    scratch = [
        slab_bf16,
        conv_tail, conv_tail,
        slab_bf16,
        slab_f32,
        slab_bf16, slab_bf16, slab_bf16,
        slab_f32,
        slab_bf16,
        slab_bf16,
        slab_f32, slab_f32, slab_f32, slab_f32,
        pltpu.VMEM((SUBLANES, D_MODEL), F32),
        pltpu.VMEM((H_M, DH_M, DH_M), F32),
        pltpu.VMEM((SUBLANES, DH_M), F32),
        pltpu.VMEM((1, LANES), F32),
        slab_bf16,
        slab_f32,
    ]
    return pl.pallas_call(
        functools.partial(_block_kernel, tiles_per_seq=nt),
        out_shape=jax.ShapeDtypeStruct(x.shape, x.dtype),
        grid=(n_tiles + 1,),
        in_specs=[x_tile] + [resident] * 11,
        out_specs=o_tile,
        scratch_shapes=scratch,
        compiler_params=pltpu.CompilerParams(
            dimension_semantics=("arbitrary",),
            vmem_limit_bytes=VMEM_LIMIT_BYTES),
        name="hybrid_block",
    )(x, w_main, w_gate, w_if, wax, wqk, ml_w_v.astype(BF16),
      w_tail, params,
      perm, perm.T, causal_p)


def kernel(x, g_pre, w_in, rg_conv_w, rg_conv_b, rg_w_a, rg_b_a, rg_w_x, rg_b_x, rg_lambda,
           ml_conv_w, ml_conv_b, ml_w_q, ml_w_k, ml_w_v, ml_b_i, ml_b_f, ml_norm_g, ml_skip,
           w_branch_r, w_branch_m, w_out, g_post):
    h = x
    for l in range(g_pre.shape[0]):
        h = _layer(h, g_pre[l], w_in[l], rg_conv_w[l], rg_conv_b[l], rg_w_a[l], rg_b_a[l],
                   rg_w_x[l], rg_b_x[l], rg_lambda[l], ml_conv_w[l], ml_conv_b[l],
                   ml_w_q[l], ml_w_k[l], ml_w_v[l], ml_b_i[l], ml_b_f[l], ml_norm_g[l],
                   ml_skip[l], w_branch_r[l], w_branch_m[l], w_out[l], g_post[l])
    return h
```

```python
import functools
import math

import jax
import jax.numpy as jnp
import numpy as np
from jax import lax
from jax.experimental import pallas as pl
from jax.experimental.pallas import tpu as pltpu

D_MODEL = 1024
H_R = 16
DH_R = D_MODEL // H_R
RG_C = 8.0
CONV_W = 4
H_M = 4
DH_M = D_MODEL // H_M
EPS = 1e-6

TILE_T = 256
CHUNK = TILE_T
COL_G = 256
N_G = D_MODEL // COL_G
SUBLANES = 8
LANES = 128
SEG = CHUNK // SUBLANES
VMEM_LIMIT_BYTES = 56 * 1024 * 1024

PAIR = 2 * COL_G
OFF_RX, OFF_RZ, OFF_MX, OFF_MZ, OFF_MO = (j * D_MODEL for j in range(5))
OFF_GR, OFF_GM = 0, D_MODEL
OFF_WBR, OFF_WBM, OFF_WOUT = 0, D_MODEL, 2 * D_MODEL

(P_GPRE, P_RCW, P_RCB, P_RBA, P_RBX, P_LAM, P_MCW, P_MCB, P_NORMG, P_SKIP, P_GPOST, P_BIF) = (
    0, 1, 5, 6, 7, 8, 9, 13, 14, 15, 16, 17)
P_ROWS = 24

F32 = jnp.float32
BF16 = jnp.bfloat16


def _dot(a, b):
    return jnp.dot(a, b, preferred_element_type=F32)


def _silu(x):
    return x * jax.nn.sigmoid(x)


def _step_deps():
    d = {"pre": ()}
    for g in range(N_G):
        d[f"fill{g}"] = ("pre",)
        d[f"rg_dot{g}"] = ("pre",)
        d[f"rg_conv{g}"] = (f"rg_dot{g}",)
        d[f"rg_gdot{g}"] = (f"rg_conv{g}",)
        d[f"rg_gates{g}"] = (f"rg_gdot{g}",)
        d[f"rg_scan{g}"] = (f"rg_gates{g}",)
        d[f"rg_out{g}"] = (f"rg_scan{g}",)
        d[f"mm_dot{g}"] = ("pre",)
        d[f"mm_conv{g}"] = (f"mm_dot{g}",)
        d[f"mm_qkv{g}"] = (f"mm_conv{g}",)
        d[f"ck_k{g}"] = (f"mm_qkv{g}", "gv_vec")
        d[f"ck_a{g}"] = (f"ck_k{g}",)
        d[f"ck_b{g}"] = (f"ck_a{g}",)
        d[f"ck_c{g}"] = (f"ck_b{g}",)
        d[f"ck_d{g}"] = (f"ck_c{g}",)
        d[f"ck_f{g}"] = (f"ck_a{g}",)
        d[f"epi{g}"] = (f"ck_d{g}", f"mm_dot{g}", f"mm_conv{g}", f"fill{g}")
    for j in range(N_G // 2):
        d[f"fill{N_G + j}"] = ("pre",)
        d[f"br{j}"] = tuple(f"rg_out{g}" for g in range(N_G))
        d[f"bm{j}"] = tuple(f"epi{g}" for g in range(N_G))
        d[f"merge{j}"] = (f"br{j}", f"bm{j}", f"fill{2 * j}", f"fill{2 * j + 1}", f"fill{N_G + j}")
    d["gv_dot"] = ("pre",)
    d["gv_z"] = ("gv_dot",)
    d["gv_cum"] = ("gv_z",)
    d["gv_vec"] = ("gv_cum",)
    d["out"] = ()
    d["post"] = ("out",)
    d["keep_x"] = ("pre", "post")
    d["unperm"] = tuple(f"merge{j}" for j in range(N_G // 2)) + ("out",)
    return d


STEP_DEPS = _step_deps()


def _issue_order():
    o = ["out", "pre", "mm_dot0", "post", "keep_x", "gv_dot", "mm_dot1", "mm_conv0"]
    gate_vec = {1: "gv_z", 2: "gv_cum"}
    for hd in range(1, H_M):
        o += [f"mm_dot{hd + 1}" if hd + 1 < H_M else "rg_dot0", f"mm_qkv{hd - 1}"]
        o += [gate_vec[hd]] if hd == 1 else []
        o += [f"mm_conv{hd}"]
        o += [gate_vec[hd]] if hd == 2 else []
    o += [f"mm_qkv{H_M - 1}", "gv_vec"]
    fills = [f"fill{i}" for i in range(N_G + N_G // 2)]
    for hd in range(H_M):
        o += [f"ck_k{hd}", f"ck_a{hd}"] + ([f"rg_dot{hd + 1}"] if hd + 1 < H_M else [])
        o += [f"rg_conv{hd}", f"ck_b{hd}", f"rg_gdot{hd}", f"ck_c{hd}", fills.pop(0)]
        o += [f"ck_d{hd}", f"rg_gates{hd}", f"ck_f{hd}", f"rg_scan{hd}", f"rg_out{hd}"]
        if 0 < hd < H_M - 1:
            o.append(fills.pop(0))
        if hd == H_M - 1:
            o += [f"br{j}" for j in range(N_G // 2)]
        o.append(f"epi{hd}")
    o += [f"bm{j}" for j in range(N_G // 2)] + [f"merge{j}" for j in range(N_G // 2)]
    o += ["unperm"]
    return o


ORDER = _issue_order()


def _linear_scan(a, b, h0):
    tn, c = a.shape
    a3 = a.reshape(tn // SUBLANES, SUBLANES, c)
    b3 = b.reshape(tn // SUBLANES, SUBLANES, c)
    sub = lax.broadcasted_iota(jnp.int32, (SUBLANES, c), 0)
    hs = []
    carry = h0
    for ck in range(tn // CHUNK):
        base = ck * SEG
        h = b3[base]
        p = a3[base]
        h_loc, p_loc = [h], [p]
        for r in range(1, SEG):
            h = a3[base + r] * h + b3[base + r]
            p = a3[base + r] * p
            h_loc.append(h)
            p_loc.append(p)
        s = 1
        while s < SUBLANES:
            keep = sub >= s
            h = jnp.where(keep, p * pltpu.roll(h, s, 0) + h, h)
            p = jnp.where(keep, p * pltpu.roll(p, s, 0), p)
            s *= 2
        end_state = p * carry + h
        seg_in = jnp.where(sub >= 1, pltpu.roll(end_state, 1, 0), carry)
        for r in range(SEG):
            hs.append(h_loc[r] + p_loc[r] * seg_in)
        carry = end_state[SUBLANES - 1:SUBLANES, :]
    return jnp.stack(hs, axis=0).reshape(tn, c), carry


def _causal_conv(tail_ref, cs, y, w_rows, bias):
    tn, c = y.shape
    n_tail = CONV_W - 1
    y3 = y.reshape(tn // SUBLANES, SUBLANES, c)
    first = lax.broadcasted_iota(jnp.int32, (n_tail, SUBLANES, c), 1) == 0
    prev_last = tail_ref[:, cs].reshape(n_tail, SUBLANES, c)
    outs = []
    for ck in range(tn // CHUNK):
        cur = y3[ck * SEG:(ck + 1) * SEG]
        wrapped = jnp.where(first, pltpu.roll(prev_last, 1, 1), pltpu.roll(cur[SEG - n_tail:], 1, 1))
        acc = bias + w_rows[CONV_W - 1] * cur
        for shift in range(1, CONV_W):
            shifted = jnp.concatenate([wrapped[n_tail - shift:], cur[:SEG - shift]], axis=0)
            acc = acc + w_rows[CONV_W - 1 - shift] * shifted
        outs.append(acc)
        prev_last = cur[SEG - n_tail:]
    tail_ref[:, cs] = prev_last.reshape(n_tail * SUBLANES, c)
    return jnp.concatenate(outs, axis=0).reshape(tn, c)


def _block_kernel(x_ref, w_ref, wg_ref, wif_ref, wax_ref, wqk_ref, wv_ref, wt_ref,
                  p_ref, perm_ref, unperm_ref, causal_ref, o_ref,
                  xn_ref, rxb_ref, mxb_ref, yr_ref, c_ref, q_ref, k_ref, v_ref, hm_ref,
                  ym_ref, y_ref, gr_ref, gm_ref, mo_ref, mz_ref,
                  hr_ref, cst_ref, nst_ref, mst_ref, ytime_ref, xprev_ref, *, tiles_per_seq):
    tn = x_ref.shape[0]

    @pl.when(pl.program_id(0) == 0)
    def _no_previous_tile():
        ytime_ref[...] = jnp.zeros_like(ytime_ref)
        xprev_ref[...] = jnp.zeros_like(xprev_ref)

    @pl.when(pl.program_id(0) % tiles_per_seq == 0)
    def _reset_state():
        rxb_ref[...] = jnp.zeros_like(rxb_ref)
        mxb_ref[...] = jnp.zeros_like(mxb_ref)
        hr_ref[...] = jnp.zeros_like(hr_ref)
        cst_ref[...] = jnp.zeros_like(cst_ref)
        nst_ref[...] = jnp.zeros_like(nst_ref)
        mst_ref[...] = jnp.zeros_like(mst_ref)

    def prow(r, cs):
        return p_ref[r:r + 1, cs]

    val = {}
    steps = {}

    def step(name):
        def register(fn):
            steps[name] = fn
            return fn
        return register

    @step("pre")
    def _():
        x = x_ref[...]
        ms = jnp.mean(x * x, axis=-1, keepdims=True)
        xn_time = (x * lax.rsqrt(ms + EPS) * p_ref[P_GPRE:P_GPRE + 1, :]).astype(BF16)
        xn_ref[...] = _dot(perm_ref[...], xn_time).astype(BF16)

    def fill_mz_gr(g):
        cs = slice(g * COL_G, (g + 1) * COL_G)
        xn = xn_ref[...]
        mz_ref[:, cs] = _dot(xn, w_ref[:, OFF_MZ + g * COL_G:OFF_MZ + (g + 1) * COL_G])
        gr_ref[:, cs] = _dot(xn, wg_ref[:, OFF_GR + g * COL_G:OFF_GR + (g + 1) * COL_G])

    def fill_gm(j):
        gm_ref[:, j * PAIR:(j + 1) * PAIR] = _dot(xn_ref[...], wg_ref[:, OFF_GM + j * PAIR:OFF_GM + (j + 1) * PAIR])

    for g in range(N_G):
        steps[f"fill{g}"] = functools.partial(fill_mz_gr, g)
    for j in range(N_G // 2):
        steps[f"fill{N_G + j}"] = functools.partial(fill_gm, j)

    def rg_dot(g):
        xn = xn_ref[...]
        val[f"rx{g}"] = _dot(xn, w_ref[:, OFF_RX + g * COL_G:OFF_RX + (g + 1) * COL_G])
        val[f"rz{g}"] = _dot(xn, w_ref[:, OFF_RZ + g * COL_G:OFF_RZ + (g + 1) * COL_G])

    def rg_conv(g):
        cs = slice(g * COL_G, (g + 1) * COL_G)
        u = _causal_conv(rxb_ref, cs, val.pop(f"rx{g}"), [prow(P_RCW + j, cs) for j in range(CONV_W)],
                         prow(P_RCB, cs))
        val[f"u{g}"] = u
        val[f"ub{g}"] = u.astype(BF16)

    def rg_gdot(g):
        val[f"gate{g}"] = _dot(val.pop(f"ub{g}"), wax_ref[g])

    def rg_gates(g):
        cs = slice(g * COL_G, (g + 1) * COL_G)
        u = val.pop(f"u{g}")
        gate = val.pop(f"gate{g}")
        r = jax.nn.sigmoid(gate[:, :COL_G] + prow(P_RBA, cs))
        i = jax.nn.sigmoid(gate[:, COL_G:] + prow(P_RBX, cs))
        half_log_a = (-0.5 * RG_C * jax.nn.softplus(-prow(P_LAM, cs))) * r
        t = -jnp.tanh(half_log_a)
        inv = 1.0 / (1.0 + t)
        root = jnp.where(t > 0.0, t * lax.rsqrt(t), 0.0)
        val[f"a{g}"] = (1.0 - t) * inv
        val[f"b{g}"] = ((root + root) * inv) * (i * u)

    def rg_scan(g):
        cs = slice(g * COL_G, (g + 1) * COL_G)
        h, h_last = _linear_scan(val.pop(f"a{g}"), val.pop(f"b{g}"), hr_ref[0:1, cs])
        hr_ref[0:1, cs] = h_last
        val[f"h{g}"] = h

    def rg_out(g):
        cs = slice(g * COL_G, (g + 1) * COL_G)
        yr_ref[:, cs] = (val.pop(f"h{g}") * _silu(val.pop(f"rz{g}"))).astype(BF16)

    def mm_dot(hd):
        cs = slice(hd * DH_M, (hd + 1) * DH_M)
        xn = xn_ref[...]
        val[f"mx{hd}"] = _dot(xn, w_ref[:, OFF_MX + hd * DH_M:OFF_MX + (hd + 1) * DH_M])
        mo_ref[:, cs] = _dot(xn, w_ref[:, OFF_MO + hd * DH_M:OFF_MO + (hd + 1) * DH_M])

    def mm_conv(hd):
        cs = slice(hd * DH_M, (hd + 1) * DH_M)
        mx = val.pop(f"mx{hd}")
        c = _silu(_causal_conv(mxb_ref, cs, mx, [prow(P_MCW + j, cs) for j in range(CONV_W)],
                               prow(P_MCB, cs)))
        c_ref[:, cs] = c
        val[f"cb{hd}"] = c.astype(BF16)
        val[f"mxb{hd}"] = mx.astype(BF16)

    def mm_qkv(hd):
        cs = slice(hd * DH_M, (hd + 1) * DH_M)
        qk = _dot(val.pop(f"cb{hd}"), wqk_ref[hd])
        q_ref[:, cs] = qk[:, :DH_M].astype(BF16)
        k_ref[:, cs] = qk[:, DH_M:].astype(BF16)
        v_ref[:, cs] = _dot(val.pop(f"mxb{hd}"), wv_ref[hd]).astype(BF16)

    for g in range(N_G):
        for nm, fn in (("rg_dot", rg_dot), ("rg_conv", rg_conv), ("rg_gdot", rg_gdot),
                       ("rg_gates", rg_gates), ("rg_scan", rg_scan), ("rg_out", rg_out),
                       ("mm_dot", mm_dot), ("mm_conv", mm_conv), ("mm_qkv", mm_qkv)):
            steps[f"{nm}{g}"] = functools.partial(fn, g)

    @step("gv_dot")
    def _():
        val["pre_if"] = _dot(xn_ref[...], wif_ref[...])

    @step("gv_z")
    def _():
        pre_if = val.pop("pre_if") + p_ref[P_BIF:P_BIF + 1, 0:LANES]
        lane = lax.broadcasted_iota(jnp.int32, pre_if.shape, 1)
        val["z"] = jnp.where(lane < H_M, pre_if, jax.nn.log_sigmoid(pre_if))

    @step("gv_cum")
    def _():
        sub = lax.broadcasted_iota(jnp.int32, (SUBLANES, LANES), 0)
        bc = val["z"].reshape(SEG, SUBLANES, LANES)
        s = 1
        while s < SEG:
            bc = jnp.concatenate([bc[:s], bc[s:] + bc[:-s]], axis=0)
            s *= 2
        seg_sum = bc[SEG - 1]
        s = 1
        while s < SUBLANES:
            seg_sum = jnp.where(sub >= s, seg_sum + pltpu.roll(seg_sum, s, 0), seg_sum)
            s *= 2
        before = jnp.where(sub >= 1, pltpu.roll(seg_sum, 1, 0), 0.0)
        val["bc"] = (bc + before).reshape(CHUNK, LANES)

    @step("gv_vec")
    def _():
        z = val.pop("z")
        bc = val.pop("bc")
        sub = lax.broadcasted_iota(jnp.int32, (SUBLANES, LANES), 0)
        m_row = mst_ref[...]
        r = pltpu.roll(z, H_M, 1) - bc
        cm = r.reshape(SEG, SUBLANES, LANES)
        s = 1
        while s < SEG:
            cm = jnp.concatenate([cm[:s], jnp.maximum(cm[s:], cm[:-s])], axis=0)
            s *= 2
        seg_max = cm[SEG - 1]
        s = 1
        while s < SUBLANES:
            seg_max = jnp.where(sub >= s, jnp.maximum(seg_max, pltpu.roll(seg_max, s, 0)), seg_max)
            s *= 2
        before = jnp.where(sub >= 1, pltpu.roll(seg_max, 1, 0), -jnp.inf)
        cm = jnp.maximum(cm, before).reshape(CHUNK, LANES)
        mj = jnp.maximum(m_row, cm)
        mj_last = mj[CHUNK - 1:CHUNK, :]
        val["gt"] = dict(
            rt=r.T,
            mj=mj,
            e_inter=jnp.exp(m_row - mj),
            e_negm=jnp.exp(-(bc + mj)),
            wg=jnp.exp(r - mj_last),
            decay=jnp.exp(m_row - mj_last))
        mst_ref[...] = bc[CHUNK - 1:CHUNK, :] + mj_last

    def ck_k(hd):
        cs = slice(hd * DH_M, (hd + 1) * DH_M)
        ln = slice(H_M + hd, H_M + hd + 1)
        gt = val["gt"]
        kw = k_ref[:, cs].astype(F32) * gt["wg"][:, ln]
        n_old = nst_ref[hd:hd + 1, :]
        val[f"n{hd}"] = n_old
        nst_ref[hd:hd + 1, :] = gt["decay"][:, ln] * n_old + jnp.sum(kw, axis=0, keepdims=True)
        val[f"kw{hd}"] = kw.astype(BF16)

    def ck_a(hd):
        cs = slice(hd * DH_M, (hd + 1) * DH_M)
        q = q_ref[:, cs]
        val[f"qk{hd}"] = lax.dot_general(q, k_ref[:, cs], (((1,), (1,)), ((), ())),
                                        preferred_element_type=F32)
        val[f"qc{hd}"] = _dot(q, cst_ref[hd].astype(BF16))
        val[f"kv{hd}"] = lax.dot_general(val.pop(f"kw{hd}"), v_ref[:, cs], (((0,), (0,)), ((), ())),
                                        preferred_element_type=F32)

    def ck_b(hd):
        ln = slice(H_M + hd, H_M + hd + 1)
        gt = val["gt"]
        causal = causal_ref[...] > 0.5
        w = jnp.where(causal, jnp.exp(gt["rt"][ln, :] - gt["mj"][:, ln]), 0.0)
        s_qk = val.pop(f"qk{hd}") * w
        val[f"rowsum{hd}"] = jnp.sum(s_qk, axis=-1, keepdims=True)
        val[f"s{hd}"] = s_qk.astype(BF16)

    def ck_c(hd):
        val[f"sv{hd}"] = _dot(val.pop(f"s{hd}"), v_ref[:, hd * DH_M:(hd + 1) * DH_M])

    def ck_d(hd):
        cs = slice(hd * DH_M, (hd + 1) * DH_M)
        ln = slice(H_M + hd, H_M + hd + 1)
        gt = val["gt"]
        e_inter = gt["e_inter"][:, ln]
        num = val.pop(f"sv{hd}") + e_inter * val.pop(f"qc{hd}")
        den = val.pop(f"rowsum{hd}") + \
            e_inter * jnp.sum(q_ref[:, cs].astype(F32) * val.pop(f"n{hd}"), axis=-1, keepdims=True)
        hm_ref[:, cs] = num * (1.0 / jnp.maximum(jnp.abs(den), gt["e_negm"][:, ln]))

    def ck_f(hd):
        ln = slice(H_M + hd, H_M + hd + 1)
        cst_ref[hd] = val["gt"]["decay"][:, ln] * cst_ref[hd] + val.pop(f"kv{hd}")

    for hd in range(H_M):
        for nm, fn in (("ck_k", ck_k), ("ck_a", ck_a), ("ck_b", ck_b), ("ck_c", ck_c), ("ck_d", ck_d),
                       ("ck_f", ck_f)):
            steps[f"{nm}{hd}"] = functools.partial(fn, hd)

    def epilogue(hd):
        cs = slice(hd * DH_M, (hd + 1) * DH_M)
        hg = jax.nn.sigmoid(mo_ref[:, cs]) * hm_ref[:, cs]
        mu = jnp.mean(hg, axis=-1, keepdims=True)
        dlt = hg - mu
        var = jnp.mean(dlt * dlt, axis=-1, keepdims=True)
        hn = dlt * lax.rsqrt(var + EPS) * prow(P_NORMG, cs) + prow(P_SKIP, cs) * c_ref[:, cs]
        ym_ref[:, cs] = (hn * _silu(mz_ref[:, cs])).astype(BF16)

    for hd in range(H_M):
        steps[f"epi{hd}"] = functools.partial(epilogue, hd)

    def br(j):
        val[f"br{j}"] = _dot(yr_ref[...], wt_ref[:, OFF_WBR + j * PAIR:OFF_WBR + (j + 1) * PAIR])

    def bm(j):
        val[f"bm{j}"] = _dot(ym_ref[...], wt_ref[:, OFF_WBM + j * PAIR:OFF_WBM + (j + 1) * PAIR])

    def merge(j):
        cs = slice(j * PAIR, (j + 1) * PAIR)
        y = jax.nn.sigmoid(gr_ref[:, cs]) * val.pop(f"br{j}") + \
            jax.nn.sigmoid(gm_ref[:, cs]) * val.pop(f"bm{j}")
        y_ref[:, cs] = y.astype(BF16)

    for j in range(N_G // 2):
        steps[f"br{j}"] = functools.partial(br, j)
        steps[f"bm{j}"] = functools.partial(bm, j)
        steps[f"merge{j}"] = functools.partial(merge, j)

    @step("unperm")
    def _():
        ytime_ref[...] = _dot(unperm_ref[...], y_ref[...]).astype(BF16)

    @step("out")
    def _():
        val["out"] = _dot(ytime_ref[...], wt_ref[:, OFF_WOUT:OFF_WOUT + D_MODEL])

    @step("post")
    def _():
        out = val.pop("out")
        ms_o = jnp.mean(out * out, axis=-1, keepdims=True)
        o_ref[...] = xprev_ref[...] + out * lax.rsqrt(ms_o + EPS) * p_ref[P_GPOST:P_GPOST + 1, :]

    @step("keep_x")
    def _():
        xprev_ref[...] = x_ref[...]

    done = set()
    for name in ORDER:
        assert all(d in done for d in STEP_DEPS[name]), (name, STEP_DEPS[name])
        steps[name]()
        done.add(name)
    assert done == set(steps), set(steps) - done


def _block_diag(w, group):
    h, d, e = w.shape
    across = jnp.tile(w.reshape(h // group, group, d, e), (1, 1, 1, group))
    on_diagonal = np.arange(group * e)[None, :] // e == np.arange(group)[:, None]
    return jnp.where(on_diagonal[None, :, None, :], across, 0.0).reshape(h // group, group * d, group * e)


def _row_order_constants():
    p = np.arange(TILE_T)
    q = p % CHUNK
    time_of = (p // CHUNK) * CHUNK + (q % SUBLANES) * SEG + q // SUBLANES
    perm = np.zeros((TILE_T, TILE_T), np.float32)
    perm[p, time_of] = 1.0
    t = time_of[:CHUNK]
    causal_p = (t[None, :] <= t[:, None]).astype(np.float32)
    return jnp.asarray(perm, BF16), jnp.asarray(causal_p, F32)


def _layer(x, g_pre, w_in, rg_conv_w, rg_conv_b, rg_w_a, rg_b_a, rg_w_x, rg_b_x, rg_lambda,
           ml_conv_w, ml_conv_b, ml_w_q, ml_w_k, ml_w_v, ml_b_i, ml_b_f, ml_norm_g, ml_skip,
           w_branch_r, w_branch_m, w_out, g_post):
    bsz, seq, d = x.shape
    assert d == D_MODEL and seq % TILE_T == 0

    n_if = 2 * H_M
    o_i = 5 * D_MODEL
    def odd_width(*ws):
        width = sum(w.shape[-1] for w in ws)
        tiles = -(-width // LANES)
        zeros = jnp.zeros((ws[0].shape[0], (tiles + 1 - tiles % 2) * LANES - width), ws[0].dtype)
        return jnp.concatenate(list(ws) + [zeros], axis=1).astype(BF16)

    w_main = w_in.astype(BF16)
    assert -(-w_main.shape[-1] // LANES) % 2 == 1
    w_gate = odd_width(w_in[:, o_i + n_if:])
    w_if = odd_width(w_in[:, o_i:o_i + n_if])
    w_tail = odd_width(w_branch_r, w_branch_m, w_out)

    group = COL_G // DH_R
    wax = jnp.concatenate([_block_diag(rg_w_a, group), _block_diag(rg_w_x, group)], axis=2).astype(BF16)
    wqk = jnp.concatenate([ml_w_q, ml_w_k * (1.0 / math.sqrt(DH_M))], axis=2).astype(BF16)

    b_if = jnp.concatenate([ml_b_i, ml_b_f, jnp.zeros((D_MODEL - n_if,), F32)])
    rows = [g_pre[None], rg_conv_w, rg_conv_b[None], rg_b_a[None], rg_b_x[None], rg_lambda[None],
            ml_conv_w, ml_conv_b[None], ml_norm_g[None], ml_skip[None], g_post[None], b_if[None]]
    n_rows = sum(r.shape[0] for r in rows)
    params = jnp.concatenate(rows + [jnp.zeros((P_ROWS - n_rows, D_MODEL), F32)], axis=0)

    perm, causal_p = _row_order_constants()

    nt = seq // TILE_T
    n_tiles = bsz * nt
    resident = pl.BlockSpec(memory_space=pltpu.VMEM)

    def tile_in(i):
        j = jnp.minimum(i, n_tiles - 1)
        return (j // nt, j % nt, 0)

    def tile_out(i):
        j = jnp.maximum(i - 1, 0)
        return (j // nt, j % nt, 0)

    x_tile = pl.BlockSpec((None, TILE_T, D_MODEL), tile_in)
    o_tile = pl.BlockSpec((None, TILE_T, D_MODEL), tile_out)
    slab_f32 = pltpu.VMEM((TILE_T, D_MODEL), F32)
    slab_bf16 = pltpu.VMEM((TILE_T, D_MODEL), BF16)
    conv_tail = pltpu.VMEM(((CONV_W - 1) * SUBLANES, D_MODEL), F32)
    scratch = [
        slab_bf16,
        conv_tail, conv_tail,
        slab_bf16,
        slab_f32,
        slab_bf16, slab_bf16, slab_bf16,
        slab_f32,
        slab_bf16,
        slab_bf16,
        slab_f32, slab_f32, slab_f32, slab_f32,
        pltpu.VMEM((SUBLANES, D_MODEL), F32),
        pltpu.VMEM((H_M, DH_M, DH_M), F32),
        pltpu.VMEM((SUBLANES, DH_M), F32),
        pltpu.VMEM((1, LANES), F32),
        slab_bf16,
        slab_f32,
    ]
    return pl.pallas_call(
        functools.partial(_block_kernel, tiles_per_seq=nt),
        out_shape=jax.ShapeDtypeStruct(x.shape, x.dtype),
        grid=(n_tiles + 1,),
        in_specs=[x_tile] + [resident] * 11,
        out_specs=o_tile,
        scratch_shapes=scratch,
        compiler_params=pltpu.CompilerParams(
            dimension_semantics=("arbitrary",),
            vmem_limit_bytes=VMEM_LIMIT_BYTES),
        name="hybrid_block",
    )(x, w_main, w_gate, w_if, wax, wqk, ml_w_v.astype(BF16),
      w_tail, params,
      perm, perm.T, causal_p)


def kernel(x, g_pre, w_in, rg_conv_w, rg_conv_b, rg_w_a, rg_b_a, rg_w_x, rg_b_x, rg_lambda,
           ml_conv_w, ml_conv_b, ml_w_q, ml_w_k, ml_w_v, ml_b_i, ml_b_f, ml_norm_g, ml_skip,
           w_branch_r, w_branch_m, w_out, g_post):
    h = x
    for l in range(g_pre.shape[0]):
        h = _layer(h, g_pre[l], w_in[l], rg_conv_w[l], rg_conv_b[l], rg_w_a[l], rg_b_a[l],
                   rg_w_x[l], rg_b_x[l], rg_lambda[l], ml_conv_w[l], ml_conv_b[l],
                   ml_w_q[l], ml_w_k[l], ml_w_v[l], ml_b_i[l], ml_b_f[l], ml_norm_g[l],
                   ml_skip[l], w_branch_r[l], w_branch_m[l], w_out[l], g_post[l])
    return h
```

```python
import functools
import math

import jax
import jax.numpy as jnp
import numpy as np
from jax import lax
from jax.experimental import pallas as pl
from jax.experimental.pallas import tpu as pltpu

D_MODEL = 1024
H_R = 16
DH_R = D_MODEL // H_R
RG_C = 8.0
CONV_W = 4
H_M = 4
DH_M = D_MODEL // H_M
EPS = 1e-6

TILE_T = 256
CHUNK = TILE_T
COL_G = 256
N_G = D_MODEL // COL_G
SUBLANES = 8
LANES = 128
SEG = CHUNK // SUBLANES
VMEM_LIMIT_BYTES = 56 * 1024 * 1024

PAIR = 2 * COL_G
OFF_RX, OFF_RZ, OFF_MX, OFF_MZ, OFF_MO = (j * D_MODEL for j in range(5))
OFF_GR, OFF_GM = 0, D_MODEL
OFF_WBR, OFF_WBM, OFF_WOUT = 0, D_MODEL, 2 * D_MODEL

(P_GPRE, P_RCW, P_RCB, P_RBA, P_RBX, P_LAM, P_MCW, P_MCB, P_NORMG, P_SKIP, P_GPOST, P_BIF) = (
    0, 1, 5, 6, 7, 8, 9, 13, 14, 15, 16, 17)
P_ROWS = 24

F32 = jnp.float32
BF16 = jnp.bfloat16


def _dot(a, b):
    return jnp.dot(a, b, preferred_element_type=F32)


def _silu(x):
    return x * jax.nn.sigmoid(x)


def _step_deps():
    d = {"pre": ()}
    for g in range(N_G):
        d[f"fill{g}"] = ("pre",)
        d[f"rg_dot{g}"] = ("pre",)
        d[f"rg_conv{g}"] = (f"rg_dot{g}",)
        d[f"rg_gdot{g}"] = (f"rg_conv{g}",)
        d[f"rg_gates{g}"] = (f"rg_gdot{g}",)
        d[f"rg_scan{g}"] = (f"rg_gates{g}",)
        d[f"rg_out{g}"] = (f"rg_scan{g}",)
        d[f"mm_dot{g}"] = ("pre",)
        d[f"mm_conv{g}"] = (f"mm_dot{g}",)
        d[f"mm_qkv{g}"] = (f"mm_conv{g}",)
        d[f"ck_k{g}"] = (f"mm_qkv{g}", "gv_vec")
        d[f"ck_a{g}"] = (f"ck_k{g}",)
        d[f"ck_b{g}"] = (f"ck_a{g}",)
        d[f"ck_c{g}"] = (f"ck_b{g}",)
        d[f"ck_d{g}"] = (f"ck_c{g}",)
        d[f"ck_f{g}"] = (f"ck_a{g}",)
        d[f"epi{g}"] = (f"ck_d{g}", f"mm_dot{g}", f"mm_conv{g}", f"fill{g}")
    for j in range(N_G // 2):
        d[f"fill{N_G + j}"] = ("pre",)
        d[f"br{j}"] = tuple(f"rg_out{g}" for g in range(N_G))
        d[f"bm{j}"] = tuple(f"epi{g}" for g in range(N_G))
        d[f"merge{j}"] = (f"br{j}", f"bm{j}", f"fill{2 * j}", f"fill{2 * j + 1}", f"fill{N_G + j}")
    d["gv_dot"] = ("pre",)
    d["gv_z"] = ("gv_dot",)
    d["gv_cum"] = ("gv_z",)
    d["gv_vec"] = ("gv_cum",)
    d["out"] = ()
    d["post"] = ("out",)
    d["keep_x"] = ("pre", "post")
    d["unperm"] = tuple(f"merge{j}" for j in range(N_G // 2)) + ("out",)
    return d


STEP_DEPS = _step_deps()


def _issue_order():
    o = ["out", "pre", "mm_dot0", "post", "keep_x", "gv_dot", "mm_dot1", "mm_conv0"]
    gate_vec = {1: "gv_z", 2: "gv_cum"}
    for hd in range(1, H_M):
        o += [f"mm_dot{hd + 1}" if hd + 1 < H_M else "rg_dot0", f"mm_qkv{hd - 1}"]
        o += [gate_vec[hd]] if hd == 1 else []
        o += [f"mm_conv{hd}"]
        o += [gate_vec[hd]] if hd == 2 else []
    o += [f"mm_qkv{H_M - 1}", "gv_vec"]
    fills = [f"fill{i}" for i in range(N_G + N_G // 2)]
    for hd in range(H_M):
        o += [f"ck_k{hd}", f"ck_a{hd}"] + ([f"rg_dot{hd + 1}"] if hd + 1 < H_M else [])
        o += [f"rg_conv{hd}", f"ck_b{hd}", f"rg_gdot{hd}", f"ck_c{hd}", fills.pop(0)]
        o += [f"ck_d{hd}", f"rg_gates{hd}", f"ck_f{hd}", f"rg_scan{hd}", f"rg_out{hd}"]
        if 0 < hd < H_M - 1:
            o.append(fills.pop(0))
        if hd == H_M - 1:
            o += [f"br{j}" for j in range(N_G // 2)]
        o.append(f"epi{hd}")
    o += [f"bm{j}" for j in range(N_G // 2)] + [f"merge{j}" for j in range(N_G // 2)]
    o += ["unperm"]
    return o


ORDER = _issue_order()


def _linear_scan(a, b, h0):
    tn, c = a.shape
    a3 = a.reshape(tn // SUBLANES, SUBLANES, c)
    b3 = b.reshape(tn // SUBLANES, SUBLANES, c)
    sub = lax.broadcasted_iota(jnp.int32, (SUBLANES, c), 0)
    hs = []
    carry = h0
    for ck in range(tn // CHUNK):
        base = ck * SEG
        h = b3[base]
        p = a3[base]
        h_loc, p_loc = [h], [p]
        for r in range(1, SEG):
            h = a3[base + r] * h + b3[base + r]
            p = a3[base + r] * p
            h_loc.append(h)
            p_loc.append(p)
        s = 1
        while s < SUBLANES:
            keep = sub >= s
            h = jnp.where(keep, p * pltpu.roll(h, s, 0) + h, h)
            p = jnp.where(keep, p * pltpu.roll(p, s, 0), p)
            s *= 2
        end_state = p * carry + h
        seg_in = jnp.where(sub >= 1, pltpu.roll(end_state, 1, 0), carry)
        for r in range(SEG):
            hs.append(h_loc[r] + p_loc[r] * seg_in)
        carry = end_state[SUBLANES - 1:SUBLANES, :]
    return jnp.stack(hs, axis=0).reshape(tn, c), carry


def _causal_conv(tail_ref, cs, y, w_rows, bias):
    tn, c = y.shape
    n_tail = CONV_W - 1
    y3 = y.reshape(tn // SUBLANES, SUBLANES, c)
    first = lax.broadcasted_iota(jnp.int32, (n_tail, SUBLANES, c), 1) == 0
    prev_last = tail_ref[:, cs].reshape(n_tail, SUBLANES, c)
    outs = []
    for ck in range(tn // CHUNK):
        cur = y3[ck * SEG:(ck + 1) * SEG]
        wrapped = jnp.where(first, pltpu.roll(prev_last, 1, 1), pltpu.roll(cur[SEG - n_tail:], 1, 1))
        acc = bias + w_rows[CONV_W - 1] * cur
        for shift in range(1, CONV_W):
            shifted = jnp.concatenate([wrapped[n_tail - shift:], cur[:SEG - shift]], axis=0)
            acc = acc + w_rows[CONV_W - 1 - shift] * shifted
        outs.append(acc)
        prev_last = cur[SEG - n_tail:]
    tail_ref[:, cs] = prev_last.reshape(n_tail * SUBLANES, c)
    return jnp.concatenate(outs, axis=0).reshape(tn, c)


def _block_kernel(x_ref, w_ref, wg_ref, wif_ref, wax_ref, wqk_ref, wv_ref, wt_ref,
                  p_ref, perm_ref, unperm_ref, causal_ref, o_ref,
                  xn_ref, rxb_ref, mxb_ref, yr_ref, c_ref, q_ref, k_ref, v_ref, hm_ref,
                  ym_ref, y_ref, gr_ref, gm_ref, mo_ref, mz_ref,
                  hr_ref, cst_ref, nst_ref, mst_ref, ytime_ref, xprev_ref, *, tiles_per_seq):
    tn = x_ref.shape[0]

    @pl.when(pl.program_id(0) == 0)
    def _no_previous_tile():
        ytime_ref[...] = jnp.zeros_like(ytime_ref)
        xprev_ref[...] = jnp.zeros_like(xprev_ref)

    @pl.when(pl.program_id(0) % tiles_per_seq == 0)
    def _reset_state():
        rxb_ref[...] = jnp.zeros_like(rxb_ref)
        mxb_ref[...] = jnp.zeros_like(mxb_ref)
        hr_ref[...] = jnp.zeros_like(hr_ref)
        cst_ref[...] = jnp.zeros_like(cst_ref)
        nst_ref[...] = jnp.zeros_like(nst_ref)
        mst_ref[...] = jnp.zeros_like(mst_ref)

    def prow(r, cs):
        return p_ref[r:r + 1, cs]

    val = {}
    steps = {}

    def step(name):
        def register(fn):
            steps[name] = fn
            return fn
        return register

    @step("pre")
    def _():
        x = x_ref[...]
        ms = jnp.mean(x * x, axis=-1, keepdims=True)
        xn_time = (x * lax.rsqrt(ms + EPS) * p_ref[P_GPRE:P_GPRE + 1, :]).astype(BF16)
        xn_ref[...] = _dot(perm_ref[...], xn_time).astype(BF16)

    def fill_mz_gr(g):
        cs = slice(g * COL_G, (g + 1) * COL_G)
        xn = xn_ref[...]
        mz_ref[:, cs] = _dot(xn, w_ref[:, OFF_MZ + g * COL_G:OFF_MZ + (g + 1) * COL_G])
        gr_ref[:, cs] = _dot(xn, wg_ref[:, OFF_GR + g * COL_G:OFF_GR + (g + 1) * COL_G])

    def fill_gm(j):
        gm_ref[:, j * PAIR:(j + 1) * PAIR] = _dot(xn_ref[...], wg_ref[:, OFF_GM + j * PAIR:OFF_GM + (j + 1) * PAIR])

    for g in range(N_G):
        steps[f"fill{g}"] = functools.partial(fill_mz_gr, g)
    for j in range(N_G // 2):
        steps[f"fill{N_G + j}"] = functools.partial(fill_gm, j)

    def rg_dot(g):
        xn = xn_ref[...]
        val[f"rx{g}"] = _dot(xn, w_ref[:, OFF_RX + g * COL_G:OFF_RX + (g + 1) * COL_G])
        val[f"rz{g}"] = _dot(xn, w_ref[:, OFF_RZ + g * COL_G:OFF_RZ + (g + 1) * COL_G])

    def rg_conv(g):
        cs = slice(g * COL_G, (g + 1) * COL_G)
        u = _causal_conv(rxb_ref, cs, val.pop(f"rx{g}"), [prow(P_RCW + j, cs) for j in range(CONV_W)],
                         prow(P_RCB, cs))
        val[f"u{g}"] = u
        val[f"ub{g}"] = u.astype(BF16)

    def rg_gdot(g):
        val[f"gate{g}"] = _dot(val.pop(f"ub{g}"), wax_ref[g])

    def rg_gates(g):
        cs = slice(g * COL_G, (g + 1) * COL_G)
        u = val.pop(f"u{g}")
        gate = val.pop(f"gate{g}")
        r = jax.nn.sigmoid(gate[:, :COL_G] + prow(P_RBA, cs))
        i = jax.nn.sigmoid(gate[:, COL_G:] + prow(P_RBX, cs))
        log_a = (-RG_C * jax.nn.softplus(-prow(P_LAM, cs))) * r
        s_half = jnp.tanh(0.5 * log_a)
        inv = 1.0 / (1.0 - s_half)
        val[f"a{g}"] = (1.0 + s_half) * inv
        val[f"b{g}"] = (2.0 * jnp.sqrt(-s_half) * inv) * (i * u)

    def rg_scan(g):
        cs = slice(g * COL_G, (g + 1) * COL_G)
        h, h_last = _linear_scan(val.pop(f"a{g}"), val.pop(f"b{g}"), hr_ref[0:1, cs])
        hr_ref[0:1, cs] = h_last
        val[f"h{g}"] = h

    def rg_out(g):
        cs = slice(g * COL_G, (g + 1) * COL_G)
        yr_ref[:, cs] = (val.pop(f"h{g}") * _silu(val.pop(f"rz{g}"))).astype(BF16)

    def mm_dot(hd):
        cs = slice(hd * DH_M, (hd + 1) * DH_M)
        xn = xn_ref[...]
        val[f"mx{hd}"] = _dot(xn, w_ref[:, OFF_MX + hd * DH_M:OFF_MX + (hd + 1) * DH_M])
        mo_ref[:, cs] = _dot(xn, w_ref[:, OFF_MO + hd * DH_M:OFF_MO + (hd + 1) * DH_M])

    def mm_conv(hd):
        cs = slice(hd * DH_M, (hd + 1) * DH_M)
        mx = val.pop(f"mx{hd}")
        c = _silu(_causal_conv(mxb_ref, cs, mx, [prow(P_MCW + j, cs) for j in range(CONV_W)],
                               prow(P_MCB, cs)))
        c_ref[:, cs] = c
        val[f"cb{hd}"] = c.astype(BF16)
        val[f"mxb{hd}"] = mx.astype(BF16)

    def mm_qkv(hd):
        cs = slice(hd * DH_M, (hd + 1) * DH_M)
        qk = _dot(val.pop(f"cb{hd}"), wqk_ref[hd])
        q_ref[:, cs] = qk[:, :DH_M].astype(BF16)
        k_ref[:, cs] = qk[:, DH_M:].astype(BF16)
        v_ref[:, cs] = _dot(val.pop(f"mxb{hd}"), wv_ref[hd]).astype(BF16)

    for g in range(N_G):
        for nm, fn in (("rg_dot", rg_dot), ("rg_conv", rg_conv), ("rg_gdot", rg_gdot),
                       ("rg_gates", rg_gates), ("rg_scan", rg_scan), ("rg_out", rg_out),
                       ("mm_dot", mm_dot), ("mm_conv", mm_conv), ("mm_qkv", mm_qkv)):
            steps[f"{nm}{g}"] = functools.partial(fn, g)

    @step("gv_dot")
    def _():
        val["pre_if"] = _dot(xn_ref[...], wif_ref[...])

    @step("gv_z")
    def _():
        pre_if = val.pop("pre_if") + p_ref[P_BIF:P_BIF + 1, 0:LANES]
        lane = lax.broadcasted_iota(jnp.int32, pre_if.shape, 1)
        val["z"] = jnp.where(lane < H_M, pre_if, jax.nn.log_sigmoid(pre_if))

    @step("gv_cum")
    def _():
        sub = lax.broadcasted_iota(jnp.int32, (SUBLANES, LANES), 0)
        bc = val["z"].reshape(SEG, SUBLANES, LANES)
        s = 1
        while s < SEG:
            bc = jnp.concatenate([bc[:s], bc[s:] + bc[:-s]], axis=0)
            s *= 2
        seg_sum = bc[SEG - 1]
        s = 1
        while s < SUBLANES:
            seg_sum = jnp.where(sub >= s, seg_sum + pltpu.roll(seg_sum, s, 0), seg_sum)
            s *= 2
        before = jnp.where(sub >= 1, pltpu.roll(seg_sum, 1, 0), 0.0)
        val["bc"] = (bc + before).reshape(CHUNK, LANES)

    @step("gv_vec")
    def _():
        z = val.pop("z")
        bc = val.pop("bc")
        sub = lax.broadcasted_iota(jnp.int32, (SUBLANES, LANES), 0)
        m_row = mst_ref[...]
        r = pltpu.roll(z, H_M, 1) - bc
        cm = r.reshape(SEG, SUBLANES, LANES)
        s = 1
        while s < SEG:
            cm = jnp.concatenate([cm[:s], jnp.maximum(cm[s:], cm[:-s])], axis=0)
            s *= 2
        seg_max = cm[SEG - 1]
        s = 1
        while s < SUBLANES:
            seg_max = jnp.where(sub >= s, jnp.maximum(seg_max, pltpu.roll(seg_max, s, 0)), seg_max)
            s *= 2
        before = jnp.where(sub >= 1, pltpu.roll(seg_max, 1, 0), -jnp.inf)
        cm = jnp.maximum(cm, before).reshape(CHUNK, LANES)
        mj = jnp.maximum(m_row, cm)
        mj_last = mj[CHUNK - 1:CHUNK, :]
        val["gt"] = dict(
            rt=r.T,
            mj=mj,
            e_inter=jnp.exp(m_row - mj),
            e_negm=jnp.exp(-(bc + mj)),
            wg=jnp.exp(r - mj_last),
            decay=jnp.exp(m_row - mj_last))
        mst_ref[...] = bc[CHUNK - 1:CHUNK, :] + mj_last

    def ck_k(hd):
        cs = slice(hd * DH_M, (hd + 1) * DH_M)
        ln = slice(H_M + hd, H_M + hd + 1)
        gt = val["gt"]
        kw = k_ref[:, cs].astype(F32) * gt["wg"][:, ln]
        n_old = nst_ref[hd:hd + 1, :]
        val[f"n{hd}"] = n_old
        nst_ref[hd:hd + 1, :] = gt["decay"][:, ln] * n_old + jnp.sum(kw, axis=0, keepdims=True)
        val[f"kw{hd}"] = kw.astype(BF16)

    def ck_a(hd):
        cs = slice(hd * DH_M, (hd + 1) * DH_M)
        q = q_ref[:, cs]
        val[f"qk{hd}"] = lax.dot_general(q, k_ref[:, cs], (((1,), (1,)), ((), ())),
                                        preferred_element_type=F32)
        val[f"qc{hd}"] = _dot(q, cst_ref[hd].astype(BF16))
        val[f"kv{hd}"] = lax.dot_general(val.pop(f"kw{hd}"), v_ref[:, cs], (((0,), (0,)), ((), ())),
                                        preferred_element_type=F32)

    def ck_b(hd):
        ln = slice(H_M + hd, H_M + hd + 1)
        gt = val["gt"]
        causal = causal_ref[...] > 0.5
        w = jnp.where(causal, jnp.exp(gt["rt"][ln, :] - gt["mj"][:, ln]), 0.0)
        s_qk = val.pop(f"qk{hd}") * w
        val[f"rowsum{hd}"] = jnp.sum(s_qk, axis=-1, keepdims=True)
        val[f"s{hd}"] = s_qk.astype(BF16)

    def ck_c(hd):
        val[f"sv{hd}"] = _dot(val.pop(f"s{hd}"), v_ref[:, hd * DH_M:(hd + 1) * DH_M])

    def ck_d(hd):
        cs = slice(hd * DH_M, (hd + 1) * DH_M)
        ln = slice(H_M + hd, H_M + hd + 1)
        gt = val["gt"]
        e_inter = gt["e_inter"][:, ln]
        num = val.pop(f"sv{hd}") + e_inter * val.pop(f"qc{hd}")
        den = val.pop(f"rowsum{hd}") + \
            e_inter * jnp.sum(q_ref[:, cs].astype(F32) * val.pop(f"n{hd}"), axis=-1, keepdims=True)
        hm_ref[:, cs] = num * (1.0 / jnp.maximum(jnp.abs(den), gt["e_negm"][:, ln]))

    def ck_f(hd):
        ln = slice(H_M + hd, H_M + hd + 1)
        cst_ref[hd] = val["gt"]["decay"][:, ln] * cst_ref[hd] + val.pop(f"kv{hd}")

    for hd in range(H_M):
        for nm, fn in (("ck_k", ck_k), ("ck_a", ck_a), ("ck_b", ck_b), ("ck_c", ck_c), ("ck_d", ck_d),
                       ("ck_f", ck_f)):
            steps[f"{nm}{hd}"] = functools.partial(fn, hd)

    def epilogue(hd):
        cs = slice(hd * DH_M, (hd + 1) * DH_M)
        hg = jax.nn.sigmoid(mo_ref[:, cs]) * hm_ref[:, cs]
        mu = jnp.mean(hg, axis=-1, keepdims=True)
        dlt = hg - mu
        var = jnp.mean(dlt * dlt, axis=-1, keepdims=True)
        hn = dlt * lax.rsqrt(var + EPS) * prow(P_NORMG, cs) + prow(P_SKIP, cs) * c_ref[:, cs]
        ym_ref[:, cs] = (hn * _silu(mz_ref[:, cs])).astype(BF16)

    for hd in range(H_M):
        steps[f"epi{hd}"] = functools.partial(epilogue, hd)

    def br(j):
        val[f"br{j}"] = _dot(yr_ref[...], wt_ref[:, OFF_WBR + j * PAIR:OFF_WBR + (j + 1) * PAIR])

    def bm(j):
        val[f"bm{j}"] = _dot(ym_ref[...], wt_ref[:, OFF_WBM + j * PAIR:OFF_WBM + (j + 1) * PAIR])

    def merge(j):
        cs = slice(j * PAIR, (j + 1) * PAIR)
        y = jax.nn.sigmoid(gr_ref[:, cs]) * val.pop(f"br{j}") + \
            jax.nn.sigmoid(gm_ref[:, cs]) * val.pop(f"bm{j}")
        y_ref[:, cs] = y.astype(BF16)

    for j in range(N_G // 2):
        steps[f"br{j}"] = functools.partial(br, j)
        steps[f"bm{j}"] = functools.partial(bm, j)
        steps[f"merge{j}"] = functools.partial(merge, j)

    @step("unperm")
    def _():
        ytime_ref[...] = _dot(unperm_ref[...], y_ref[...]).astype(BF16)

    @step("out")
    def _():
        val["out"] = _dot(ytime_ref[...], wt_ref[:, OFF_WOUT:OFF_WOUT + D_MODEL])

    @step("post")
    def _():
        out = val.pop("out")
        ms_o = jnp.mean(out * out, axis=-1, keepdims=True)
        o_ref[...] = xprev_ref[...] + out * lax.rsqrt(ms_o + EPS) * p_ref[P_GPOST:P_GPOST + 1, :]

    @step("keep_x")
    def _():
        xprev_ref[...] = x_ref[...]

    done = set()
    for name in ORDER:
        assert all(d in done for d in STEP_DEPS[name]), (name, STEP_DEPS[name])
        steps[name]()
        done.add(name)
    assert done == set(steps), set(steps) - done


def _block_diag(w, group):
    h, d, e = w.shape
    across = jnp.tile(w.reshape(h // group, group, d, e), (1, 1, 1, group))
    on_diagonal = np.arange(group * e)[None, :] // e == np.arange(group)[:, None]
    return jnp.where(on_diagonal[None, :, None, :], across, 0.0).reshape(h // group, group * d, group * e)


def _row_order_constants():
    p = np.arange(TILE_T)
    q = p % CHUNK
    time_of = (p // CHUNK) * CHUNK + (q % SUBLANES) * SEG + q // SUBLANES
    perm = np.zeros((TILE_T, TILE_T), np.float32)
    perm[p, time_of] = 1.0
    t = time_of[:CHUNK]
    causal_p = (t[None, :] <= t[:, None]).astype(np.float32)
    return jnp.asarray(perm, BF16), jnp.asarray(causal_p, F32)


def _layer(x, g_pre, w_in, rg_conv_w, rg_conv_b, rg_w_a, rg_b_a, rg_w_x, rg_b_x, rg_lambda,
           ml_conv_w, ml_conv_b, ml_w_q, ml_w_k, ml_w_v, ml_b_i, ml_b_f, ml_norm_g, ml_skip,
           w_branch_r, w_branch_m, w_out, g_post):
    bsz, seq, d = x.shape
    assert d == D_MODEL and seq % TILE_T == 0

    n_if = 2 * H_M
    o_i = 5 * D_MODEL
    def odd_width(*ws):
        width = sum(w.shape[-1] for w in ws)
        tiles = -(-width // LANES)
        zeros = jnp.zeros((ws[0].shape[0], (tiles + 1 - tiles % 2) * LANES - width), ws[0].dtype)
        return jnp.concatenate(list(ws) + [zeros], axis=1).astype(BF16)

    w_main = w_in.astype(BF16)
    assert -(-w_main.shape[-1] // LANES) % 2 == 1
    w_gate = odd_width(w_in[:, o_i + n_if:])
    w_if = odd_width(w_in[:, o_i:o_i + n_if])
    w_tail = odd_width(w_branch_r, w_branch_m, w_out)

    group = COL_G // DH_R
    wax = jnp.concatenate([_block_diag(rg_w_a, group), _block_diag(rg_w_x, group)], axis=2).astype(BF16)
    wqk = jnp.concatenate([ml_w_q, ml_w_k * (1.0 / math.sqrt(DH_M))], axis=2).astype(BF16)

    b_if = jnp.concatenate([ml_b_i, ml_b_f, jnp.zeros((D_MODEL - n_if,), F32)])
    rows = [g_pre[None], rg_conv_w, rg_conv_b[None], rg_b_a[None], rg_b_x[None], rg_lambda[None],
            ml_conv_w, ml_conv_b[None], ml_norm_g[None], ml_skip[None], g_post[None], b_if[None]]
    n_rows = sum(r.shape[0] for r in rows)
    params = jnp.concatenate(rows + [jnp.zeros((P_ROWS - n_rows, D_MODEL), F32)], axis=0)

    perm, causal_p = _row_order_constants()

    nt = seq // TILE_T
    n_tiles = bsz * nt
    resident = pl.BlockSpec(memory_space=pltpu.VMEM)

    def tile_in(i):
        j = jnp.minimum(i, n_tiles - 1)
        return (j // nt, j % nt, 0)

    def tile_out(i):
        j = jnp.maximum(i - 1, 0)
        return (j // nt, j % nt, 0)

    x_tile = pl.BlockSpec((None, TILE_T, D_MODEL), tile_in)
    o_tile = pl.BlockSpec((None, TILE_T, D_MODEL), tile_out)
    slab_f32 = pltpu.VMEM((TILE_T, D_MODEL), F32)
    slab_bf16 = pltpu.VMEM((TILE_T, D_MODEL), BF16)
    conv_tail = pltpu.VMEM(((CONV_W - 1) * SUBLANES, D_MODEL), F32)
    scratch = [
        slab_bf16,
        conv_tail, conv_tail,
        slab_bf16,
        slab_f32,
        slab_bf16, slab_bf16, slab_bf16,
        slab_f32,
        slab_bf16,
        slab_bf16,
        slab_f32, slab_f32, slab_f32, slab_f32,
        pltpu.VMEM((SUBLANES, D_MODEL), F32),
        pltpu.VMEM((H_M, DH_M, DH_M), F32),
        pltpu.VMEM((SUBLANES, DH_M), F32),
        pltpu.VMEM((1, LANES), F32),
        slab_bf16,
        slab_f32,
    ]
    return pl.pallas_call(
        functools.partial(_block_kernel, tiles_per_seq=nt),
        out_shape=jax.ShapeDtypeStruct(x.shape, x.dtype),
        grid=(n_tiles + 1,),
        in_specs=[x_tile] + [resident] * 11,
        out_specs=o_tile,
        scratch_shapes=scratch,
        compiler_params=pltpu.CompilerParams(
            dimension_semantics=("arbitrary",),
            vmem_limit_bytes=VMEM_LIMIT_BYTES),
        name="hybrid_block",
    )(x, w_main, w_gate, w_if, wax, wqk, ml_w_v.astype(BF16),
      w_tail, params,
      perm, perm.T, causal_p)


def kernel(x, g_pre, w_in, rg_conv_w, rg_conv_b, rg_w_a, rg_b_a, rg_w_x, rg_b_x, rg_lambda,
           ml_conv_w, ml_conv_b, ml_w_q, ml_w_k, ml_w_v, ml_b_i, ml_b_f, ml_norm_g, ml_skip,
           w_branch_r, w_branch_m, w_out, g_post):
    h = x
    for l in range(g_pre.shape[0]):
        h = _layer(h, g_pre[l], w_in[l], rg_conv_w[l], rg_conv_b[l], rg_w_a[l], rg_b_a[l],
                   rg_w_x[l], rg_b_x[l], rg_lambda[l], ml_conv_w[l], ml_conv_b[l],
                   ml_w_q[l], ml_w_k[l], ml_w_v[l], ml_b_i[l], ml_b_f[l], ml_norm_g[l],
                   ml_skip[l], w_branch_r[l], w_branch_m[l], w_out[l], g_post[l])
    return h
```

```python
import functools
import math

import jax
import jax.numpy as jnp
import numpy as np
from jax import lax
from jax.experimental import pallas as pl
from jax.experimental.pallas import tpu as pltpu

D_MODEL = 1024
H_R = 16
DH_R = D_MODEL // H_R
RG_C = 8.0
CONV_W = 4
H_M = 4
DH_M = D_MODEL // H_M
EPS = 1e-6

TILE_T = 256
CHUNK = TILE_T
COL_G = 256
N_G = D_MODEL // COL_G
SUBLANES = 8
LANES = 128
SEG = CHUNK // SUBLANES
VMEM_LIMIT_BYTES = 56 * 1024 * 1024

PAIR = 2 * COL_G
OFF_RX, OFF_RZ, OFF_MX, OFF_MZ, OFF_MO = (j * D_MODEL for j in range(5))
OFF_GR, OFF_GM = 0, D_MODEL
OFF_WBR, OFF_WBM, OFF_WOUT = 0, D_MODEL, 2 * D_MODEL

(P_GPRE, P_RCW, P_RCB, P_RBA, P_RBX, P_LAM, P_MCW, P_MCB, P_NORMG, P_SKIP, P_GPOST, P_BIF) = (
    0, 1, 5, 6, 7, 8, 9, 13, 14, 15, 16, 17)
P_ROWS = 24

F32 = jnp.float32
BF16 = jnp.bfloat16


def _dot(a, b):
    return jnp.dot(a, b, preferred_element_type=F32)


def _silu(x):
    return x * jax.nn.sigmoid(x)


def _step_deps():
    d = {"pre": ()}
    for g in range(N_G):
        d[f"fill{g}"] = ("pre",)
        d[f"rg_dot{g}"] = ("pre",)
        d[f"rg_conv{g}"] = (f"rg_dot{g}",)
        d[f"rg_gdot{g}"] = (f"rg_conv{g}",)
        d[f"rg_gates{g}"] = (f"rg_gdot{g}",)
        d[f"rg_scan{g}"] = (f"rg_gates{g}",)
        d[f"rg_out{g}"] = (f"rg_scan{g}",)
        d[f"mm_dot{g}"] = ("pre",)
        d[f"mm_conv{g}"] = (f"mm_dot{g}",)
        d[f"mm_qkv{g}"] = (f"mm_conv{g}",)
        d[f"ck_k{g}"] = (f"mm_qkv{g}", "gv_vec")
        d[f"ck_a{g}"] = (f"ck_k{g}",)
        d[f"ck_b{g}"] = (f"ck_a{g}",)
        d[f"ck_c{g}"] = (f"ck_b{g}",)
        d[f"ck_d{g}"] = (f"ck_c{g}",)
        d[f"ck_f{g}"] = (f"ck_a{g}",)
        d[f"epi{g}"] = (f"ck_d{g}", f"mm_dot{g}", f"mm_conv{g}", f"fill{g}")
    for j in range(N_G // 2):
        d[f"fill{N_G + j}"] = ("pre",)
        d[f"br{j}"] = tuple(f"rg_out{g}" for g in range(N_G))
        d[f"bm{j}"] = tuple(f"epi{g}" for g in range(N_G))
        d[f"merge{j}"] = (f"br{j}", f"bm{j}", f"fill{2 * j}", f"fill{2 * j + 1}", f"fill{N_G + j}")
    d["gv_dot"] = ("pre",)
    d["gv_z"] = ("gv_dot",)
    d["gv_cum"] = ("gv_z",)
    d["gv_vec"] = ("gv_cum",)
    d["out"] = ()
    d["post"] = ("out",)
    d["keep_x"] = ("pre", "post")
    d["unperm"] = tuple(f"merge{j}" for j in range(N_G // 2)) + ("out",)
    return d


STEP_DEPS = _step_deps()


def _issue_order():
    assert (H_M, N_G) == (4, 4)
    head = ["out", "post", "pre", "mm_dot0", "keep_x", "gv_dot", "mm_conv0", "mm_dot2", "mm_dot1",
            "mm_qkv0", "gv_z", "mm_conv1", "mm_dot3", "mm_qkv1", "mm_conv2", "gv_cum", "rg_dot0",
            "mm_qkv2", "mm_conv3", "mm_qkv3", "rg_dot1", "gv_vec"]
    per_head = [
        ["ck_k0", "ck_a0", "rg_conv0", "ck_b0", "rg_gdot0", "ck_c0", "fill0", "ck_d0", "rg_gates0",
         "ck_f0", "rg_scan0", "rg_out0", "epi0"],
        ["ck_k1", "rg_dot2", "rg_conv1", "ck_a1", "ck_b1", "rg_gdot1", "ck_c1", "fill1", "ck_d1",
         "rg_gates1", "ck_f1", "rg_scan1", "rg_out1", "fill2", "epi1"],
        ["ck_k2", "ck_a2", "rg_conv2", "ck_b2", "rg_gdot2", "ck_c2", "fill3", "ck_d2", "rg_gates2",
         "ck_f2", "rg_dot3", "rg_scan2", "rg_out2", "epi2"],
        ["ck_k3", "ck_a3", "rg_conv3", "ck_b3", "fill4", "rg_gdot3", "ck_c3", "fill5", "ck_d3",
         "rg_gates3", "ck_f3", "rg_scan3", "rg_out3", "br1", "br0", "epi3"],
    ]
    tail = ["bm0", "bm1", "merge0", "merge1", "unperm"]
    return head + [name for steps_of_head in per_head for name in steps_of_head] + tail


ORDER = _issue_order()


def _linear_scan(a, b, h0):
    tn, c = a.shape
    a3 = a.reshape(tn // SUBLANES, SUBLANES, c)
    b3 = b.reshape(tn // SUBLANES, SUBLANES, c)
    sub = lax.broadcasted_iota(jnp.int32, (SUBLANES, c), 0)
    hs = []
    carry = h0
    for ck in range(tn // CHUNK):
        base = ck * SEG
        h = b3[base]
        p = a3[base]
        h_loc, p_loc = [h], [p]
        for r in range(1, SEG):
            h = a3[base + r] * h + b3[base + r]
            p = a3[base + r] * p
            h_loc.append(h)
            p_loc.append(p)
        s = 1
        while s < SUBLANES:
            keep = sub >= s
            h = jnp.where(keep, p * pltpu.roll(h, s, 0) + h, h)
            p = jnp.where(keep, p * pltpu.roll(p, s, 0), p)
            s *= 2
        end_state = p * carry + h
        seg_in = jnp.where(sub >= 1, pltpu.roll(end_state, 1, 0), carry)
        for r in range(SEG):
            hs.append(h_loc[r] + p_loc[r] * seg_in)
        carry = end_state[SUBLANES - 1:SUBLANES, :]
    return jnp.stack(hs, axis=0).reshape(tn, c), carry


def _causal_conv(tail_ref, cs, y, w_rows, bias):
    tn, c = y.shape
    n_tail = CONV_W - 1
    y3 = y.reshape(tn // SUBLANES, SUBLANES, c)
    first = lax.broadcasted_iota(jnp.int32, (n_tail, SUBLANES, c), 1) == 0
    prev_last = tail_ref[:, cs].reshape(n_tail, SUBLANES, c)
    outs = []
    for ck in range(tn // CHUNK):
        cur = y3[ck * SEG:(ck + 1) * SEG]
        wrapped = jnp.where(first, pltpu.roll(prev_last, 1, 1), pltpu.roll(cur[SEG - n_tail:], 1, 1))
        acc = bias + w_rows[CONV_W - 1] * cur
        for shift in range(1, CONV_W):
            shifted = jnp.concatenate([wrapped[n_tail - shift:], cur[:SEG - shift]], axis=0)
            acc = acc + w_rows[CONV_W - 1 - shift] * shifted
        outs.append(acc)
        prev_last = cur[SEG - n_tail:]
    tail_ref[:, cs] = prev_last.reshape(n_tail * SUBLANES, c)
    return jnp.concatenate(outs, axis=0).reshape(tn, c)


def _block_kernel(x_ref, w_ref, wg_ref, wif_ref, wax_ref, wqk_ref, wv_ref, wt_ref,
                  p_ref, perm_ref, unperm_ref, causal_ref, o_ref,
                  xn_ref, rxb_ref, mxb_ref, yr_ref, c_ref, q_ref, k_ref, v_ref, hm_ref,
                  ym_ref, y_ref, gr_ref, gm_ref, mo_ref, mz_ref,
                  hr_ref, cst_ref, nst_ref, mst_ref, ytime_ref, xprev_ref, *, tiles_per_seq):
    tn = x_ref.shape[0]

    @pl.when(pl.program_id(0) == 0)
    def _no_previous_tile():
        ytime_ref[...] = jnp.zeros_like(ytime_ref)
        xprev_ref[...] = jnp.zeros_like(xprev_ref)

    @pl.when(pl.program_id(0) % tiles_per_seq == 0)
    def _reset_state():
        rxb_ref[...] = jnp.zeros_like(rxb_ref)
        mxb_ref[...] = jnp.zeros_like(mxb_ref)
        hr_ref[...] = jnp.zeros_like(hr_ref)
        cst_ref[...] = jnp.zeros_like(cst_ref)
        nst_ref[...] = jnp.zeros_like(nst_ref)
        mst_ref[...] = jnp.zeros_like(mst_ref)

    def prow(r, cs):
        return p_ref[r:r + 1, cs]

    val = {}
    steps = {}

    def step(name):
        def register(fn):
            steps[name] = fn
            return fn
        return register

    @step("pre")
    def _():
        x = x_ref[...]
        ms = jnp.mean(x * x, axis=-1, keepdims=True)
        xn_time = (x * lax.rsqrt(ms + EPS) * p_ref[P_GPRE:P_GPRE + 1, :]).astype(BF16)
        xn_ref[...] = _dot(perm_ref[...], xn_time).astype(BF16)

    def fill_mz_gr(g):
        cs = slice(g * COL_G, (g + 1) * COL_G)
        xn = xn_ref[...]
        mz_ref[:, cs] = _dot(xn, w_ref[:, OFF_MZ + g * COL_G:OFF_MZ + (g + 1) * COL_G])
        gr_ref[:, cs] = _dot(xn, wg_ref[:, OFF_GR + g * COL_G:OFF_GR + (g + 1) * COL_G])

    def fill_gm(j):
        gm_ref[:, j * PAIR:(j + 1) * PAIR] = _dot(xn_ref[...], wg_ref[:, OFF_GM + j * PAIR:OFF_GM + (j + 1) * PAIR])

    for g in range(N_G):
        steps[f"fill{g}"] = functools.partial(fill_mz_gr, g)
    for j in range(N_G // 2):
        steps[f"fill{N_G + j}"] = functools.partial(fill_gm, j)

    def rg_dot(g):
        xn = xn_ref[...]
        val[f"rx{g}"] = _dot(xn, w_ref[:, OFF_RX + g * COL_G:OFF_RX + (g + 1) * COL_G])
        val[f"rz{g}"] = _dot(xn, w_ref[:, OFF_RZ + g * COL_G:OFF_RZ + (g + 1) * COL_G])

    def rg_conv(g):
        cs = slice(g * COL_G, (g + 1) * COL_G)
        u = _causal_conv(rxb_ref, cs, val.pop(f"rx{g}"), [prow(P_RCW + j, cs) for j in range(CONV_W)],
                         prow(P_RCB, cs))
        val[f"u{g}"] = u
        val[f"ub{g}"] = u.astype(BF16)

    def rg_gdot(g):
        val[f"gate{g}"] = _dot(val.pop(f"ub{g}"), wax_ref[g])

    def rg_gates(g):
        cs = slice(g * COL_G, (g + 1) * COL_G)
        u = val.pop(f"u{g}")
        gate = val.pop(f"gate{g}")
        r = jax.nn.sigmoid(gate[:, :COL_G] + prow(P_RBA, cs))
        i = jax.nn.sigmoid(gate[:, COL_G:] + prow(P_RBX, cs))
        log_a = (-RG_C * jax.nn.softplus(-prow(P_LAM, cs))) * r
        s_half = jnp.tanh(0.5 * log_a)
        inv = 1.0 / (1.0 - s_half)
        val[f"a{g}"] = (1.0 + s_half) * inv
        val[f"b{g}"] = (2.0 * jnp.sqrt(-s_half) * inv) * (i * u)

    def rg_scan(g):
        cs = slice(g * COL_G, (g + 1) * COL_G)
        h, h_last = _linear_scan(val.pop(f"a{g}"), val.pop(f"b{g}"), hr_ref[0:1, cs])
        hr_ref[0:1, cs] = h_last
        val[f"h{g}"] = h

    def rg_out(g):
        cs = slice(g * COL_G, (g + 1) * COL_G)
        yr_ref[:, cs] = (val.pop(f"h{g}") * _silu(val.pop(f"rz{g}"))).astype(BF16)

    def mm_dot(hd):
        cs = slice(hd * DH_M, (hd + 1) * DH_M)
        xn = xn_ref[...]
        val[f"mx{hd}"] = _dot(xn, w_ref[:, OFF_MX + hd * DH_M:OFF_MX + (hd + 1) * DH_M])
        mo_ref[:, cs] = _dot(xn, w_ref[:, OFF_MO + hd * DH_M:OFF_MO + (hd + 1) * DH_M])

    def mm_conv(hd):
        cs = slice(hd * DH_M, (hd + 1) * DH_M)
        mx = val.pop(f"mx{hd}")
        c = _silu(_causal_conv(mxb_ref, cs, mx, [prow(P_MCW + j, cs) for j in range(CONV_W)],
                               prow(P_MCB, cs)))
        c_ref[:, cs] = c
        val[f"cb{hd}"] = c.astype(BF16)
        val[f"mxb{hd}"] = mx.astype(BF16)

    def mm_qkv(hd):
        cs = slice(hd * DH_M, (hd + 1) * DH_M)
        qk = _dot(val.pop(f"cb{hd}"), wqk_ref[hd])
        q_ref[:, cs] = qk[:, :DH_M].astype(BF16)
        k_ref[:, cs] = qk[:, DH_M:].astype(BF16)
        v_ref[:, cs] = _dot(val.pop(f"mxb{hd}"), wv_ref[hd]).astype(BF16)

    for g in range(N_G):
        for nm, fn in (("rg_dot", rg_dot), ("rg_conv", rg_conv), ("rg_gdot", rg_gdot),
                       ("rg_gates", rg_gates), ("rg_scan", rg_scan), ("rg_out", rg_out),
                       ("mm_dot", mm_dot), ("mm_conv", mm_conv), ("mm_qkv", mm_qkv)):
            steps[f"{nm}{g}"] = functools.partial(fn, g)

    @step("gv_dot")
    def _():
        val["pre_if"] = _dot(xn_ref[...], wif_ref[...])

    @step("gv_z")
    def _():
        pre_if = val.pop("pre_if") + p_ref[P_BIF:P_BIF + 1, 0:LANES]
        lane = lax.broadcasted_iota(jnp.int32, pre_if.shape, 1)
        val["z"] = jnp.where(lane < H_M, pre_if, jax.nn.log_sigmoid(pre_if))

    @step("gv_cum")
    def _():
        sub = lax.broadcasted_iota(jnp.int32, (SUBLANES, LANES), 0)
        bc = val["z"].reshape(SEG, SUBLANES, LANES)
        s = 1
        while s < SEG:
            bc = jnp.concatenate([bc[:s], bc[s:] + bc[:-s]], axis=0)
            s *= 2
        seg_sum = bc[SEG - 1]
        s = 1
        while s < SUBLANES:
            seg_sum = jnp.where(sub >= s, seg_sum + pltpu.roll(seg_sum, s, 0), seg_sum)
            s *= 2
        before = jnp.where(sub >= 1, pltpu.roll(seg_sum, 1, 0), 0.0)
        val["bc"] = (bc + before).reshape(CHUNK, LANES)

    @step("gv_vec")
    def _():
        z = val.pop("z")
        bc = val.pop("bc")
        sub = lax.broadcasted_iota(jnp.int32, (SUBLANES, LANES), 0)
        m_row = mst_ref[...]
        r = pltpu.roll(z, H_M, 1) - bc
        cm = r.reshape(SEG, SUBLANES, LANES)
        s = 1
        while s < SEG:
            cm = jnp.concatenate([cm[:s], jnp.maximum(cm[s:], cm[:-s])], axis=0)
            s *= 2
        seg_max = cm[SEG - 1]
        s = 1
        while s < SUBLANES:
            seg_max = jnp.where(sub >= s, jnp.maximum(seg_max, pltpu.roll(seg_max, s, 0)), seg_max)
            s *= 2
        before = jnp.where(sub >= 1, pltpu.roll(seg_max, 1, 0), -jnp.inf)
        cm = jnp.maximum(cm, before).reshape(CHUNK, LANES)
        mj = jnp.maximum(m_row, cm)
        mj_last = mj[CHUNK - 1:CHUNK, :]
        val["gt"] = dict(
            rt=r.T,
            mj=mj,
            e_inter=jnp.exp(m_row - mj),
            e_negm=jnp.exp(-(bc + mj)),
            wg=jnp.exp(r - mj_last),
            decay=jnp.exp(m_row - mj_last))
        mst_ref[...] = bc[CHUNK - 1:CHUNK, :] + mj_last

    def ck_k(hd):
        cs = slice(hd * DH_M, (hd + 1) * DH_M)
        ln = slice(H_M + hd, H_M + hd + 1)
        gt = val["gt"]
        kw = k_ref[:, cs].astype(F32) * gt["wg"][:, ln]
        n_old = nst_ref[hd:hd + 1, :]
        val[f"n{hd}"] = n_old
        nst_ref[hd:hd + 1, :] = gt["decay"][:, ln] * n_old + jnp.sum(kw, axis=0, keepdims=True)
        val[f"kw{hd}"] = kw.astype(BF16)

    def ck_a(hd):
        cs = slice(hd * DH_M, (hd + 1) * DH_M)
        q = q_ref[:, cs]
        val[f"qk{hd}"] = lax.dot_general(q, k_ref[:, cs], (((1,), (1,)), ((), ())),
                                        preferred_element_type=F32)
        val[f"qc{hd}"] = _dot(q, cst_ref[hd].astype(BF16))
        val[f"kv{hd}"] = lax.dot_general(val.pop(f"kw{hd}"), v_ref[:, cs], (((0,), (0,)), ((), ())),
                                        preferred_element_type=F32)

    def ck_b(hd):
        ln = slice(H_M + hd, H_M + hd + 1)
        gt = val["gt"]
        causal = causal_ref[...] > 0.5
        w = jnp.where(causal, jnp.exp(gt["rt"][ln, :] - gt["mj"][:, ln]), 0.0)
        s_qk = val.pop(f"qk{hd}") * w
        val[f"rowsum{hd}"] = jnp.sum(s_qk, axis=-1, keepdims=True)
        val[f"s{hd}"] = s_qk.astype(BF16)

    def ck_c(hd):
        val[f"sv{hd}"] = _dot(val.pop(f"s{hd}"), v_ref[:, hd * DH_M:(hd + 1) * DH_M])

    def ck_d(hd):
        cs = slice(hd * DH_M, (hd + 1) * DH_M)
        ln = slice(H_M + hd, H_M + hd + 1)
        gt = val["gt"]
        e_inter = gt["e_inter"][:, ln]
        num = val.pop(f"sv{hd}") + e_inter * val.pop(f"qc{hd}")
        den = val.pop(f"rowsum{hd}") + \
            e_inter * jnp.sum(q_ref[:, cs].astype(F32) * val.pop(f"n{hd}"), axis=-1, keepdims=True)
        hm_ref[:, cs] = num * (1.0 / jnp.maximum(jnp.abs(den), gt["e_negm"][:, ln]))

    def ck_f(hd):
        ln = slice(H_M + hd, H_M + hd + 1)
        cst_ref[hd] = val["gt"]["decay"][:, ln] * cst_ref[hd] + val.pop(f"kv{hd}")

    for hd in range(H_M):
        for nm, fn in (("ck_k", ck_k), ("ck_a", ck_a), ("ck_b", ck_b), ("ck_c", ck_c), ("ck_d", ck_d),
                       ("ck_f", ck_f)):
            steps[f"{nm}{hd}"] = functools.partial(fn, hd)

    def epilogue(hd):
        cs = slice(hd * DH_M, (hd + 1) * DH_M)
        hg = jax.nn.sigmoid(mo_ref[:, cs]) * hm_ref[:, cs]
        mu = jnp.mean(hg, axis=-1, keepdims=True)
        dlt = hg - mu
        var = jnp.mean(dlt * dlt, axis=-1, keepdims=True)
        hn = dlt * lax.rsqrt(var + EPS) * prow(P_NORMG, cs) + prow(P_SKIP, cs) * c_ref[:, cs]
        ym_ref[:, cs] = (hn * _silu(mz_ref[:, cs])).astype(BF16)

    for hd in range(H_M):
        steps[f"epi{hd}"] = functools.partial(epilogue, hd)

    def br(j):
        val[f"br{j}"] = _dot(yr_ref[...], wt_ref[:, OFF_WBR + j * PAIR:OFF_WBR + (j + 1) * PAIR])

    def bm(j):
        val[f"bm{j}"] = _dot(ym_ref[...], wt_ref[:, OFF_WBM + j * PAIR:OFF_WBM + (j + 1) * PAIR])

    def merge(j):
        cs = slice(j * PAIR, (j + 1) * PAIR)
        y = jax.nn.sigmoid(gr_ref[:, cs]) * val.pop(f"br{j}") + \
            jax.nn.sigmoid(gm_ref[:, cs]) * val.pop(f"bm{j}")
        y_ref[:, cs] = y.astype(BF16)

    for j in range(N_G // 2):
        steps[f"br{j}"] = functools.partial(br, j)
        steps[f"bm{j}"] = functools.partial(bm, j)
        steps[f"merge{j}"] = functools.partial(merge, j)

    @step("unperm")
    def _():
        ytime_ref[...] = _dot(unperm_ref[...], y_ref[...]).astype(BF16)

    @step("out")
    def _():
        val["out"] = _dot(ytime_ref[...], wt_ref[:, OFF_WOUT:OFF_WOUT + D_MODEL])

    @step("post")
    def _():
        out = val.pop("out")
        ms_o = jnp.mean(out * out, axis=-1, keepdims=True)
        o_ref[...] = xprev_ref[...] + out * lax.rsqrt(ms_o + EPS) * p_ref[P_GPOST:P_GPOST + 1, :]

    @step("keep_x")
    def _():
        xprev_ref[...] = x_ref[...]

    done = set()
    for name in ORDER:
        assert all(d in done for d in STEP_DEPS[name]), (name, STEP_DEPS[name])
        steps[name]()
        done.add(name)
    assert done == set(steps), set(steps) - done


def _block_diag(w, group):
    h, d, e = w.shape
    across = jnp.tile(w.reshape(h // group, group, d, e), (1, 1, 1, group))
    on_diagonal = np.arange(group * e)[None, :] // e == np.arange(group)[:, None]
    return jnp.where(on_diagonal[None, :, None, :], across, 0.0).reshape(h // group, group * d, group * e)


def _row_order_constants():
    p = np.arange(TILE_T)
    q = p % CHUNK
    time_of = (p // CHUNK) * CHUNK + (q % SUBLANES) * SEG + q // SUBLANES
    perm = np.zeros((TILE_T, TILE_T), np.float32)
    perm[p, time_of] = 1.0
    t = time_of[:CHUNK]
    causal_p = (t[None, :] <= t[:, None]).astype(np.float32)
    return jnp.asarray(perm, BF16), jnp.asarray(causal_p, F32)


def _layer(x, g_pre, w_in, rg_conv_w, rg_conv_b, rg_w_a, rg_b_a, rg_w_x, rg_b_x, rg_lambda,
           ml_conv_w, ml_conv_b, ml_w_q, ml_w_k, ml_w_v, ml_b_i, ml_b_f, ml_norm_g, ml_skip,
           w_branch_r, w_branch_m, w_out, g_post):
    bsz, seq, d = x.shape
    assert d == D_MODEL and seq % TILE_T == 0

    n_if = 2 * H_M
    o_i = 5 * D_MODEL
    def odd_width(*ws):
        width = sum(w.shape[-1] for w in ws)
        tiles = -(-width // LANES)
        zeros = jnp.zeros((ws[0].shape[0], (tiles + 1 - tiles % 2) * LANES - width), ws[0].dtype)
        return jnp.concatenate(list(ws) + [zeros], axis=1).astype(BF16)

    w_main = w_in.astype(BF16)
    assert -(-w_main.shape[-1] // LANES) % 2 == 1
    w_gate = odd_width(w_in[:, o_i + n_if:])
    w_if = odd_width(w_in[:, o_i:o_i + n_if])
    w_tail = odd_width(w_branch_r, w_branch_m, w_out)

    group = COL_G // DH_R
    wax = jnp.concatenate([_block_diag(rg_w_a, group), _block_diag(rg_w_x, group)], axis=2).astype(BF16)
    wqk = jnp.concatenate([ml_w_q, ml_w_k * (1.0 / math.sqrt(DH_M))], axis=2).astype(BF16)

    b_if = jnp.concatenate([ml_b_i, ml_b_f, jnp.zeros((D_MODEL - n_if,), F32)])
    rows = [g_pre[None], rg_conv_w, rg_conv_b[None], rg_b_a[None], rg_b_x[None], rg_lambda[None],
            ml_conv_w, ml_conv_b[None], ml_norm_g[None], ml_skip[None], g_post[None], b_if[None]]
    n_rows = sum(r.shape[0] for r in rows)
    params = jnp.concatenate(rows + [jnp.zeros((P_ROWS - n_rows, D_MODEL), F32)], axis=0)

    perm, causal_p = _row_order_constants()

    nt = seq // TILE_T
    n_tiles = bsz * nt
    resident = pl.BlockSpec(memory_space=pltpu.VMEM)

    def tile_in(i):
        j = jnp.minimum(i, n_tiles - 1)
        return (j // nt, j % nt, 0)

    def tile_out(i):
        j = jnp.maximum(i - 1, 0)
        return (j // nt, j % nt, 0)

    x_tile = pl.BlockSpec((None, TILE_T, D_MODEL), tile_in)
    o_tile = pl.BlockSpec((None, TILE_T, D_MODEL), tile_out)
    slab_f32 = pltpu.VMEM((TILE_T, D_MODEL), F32)
    slab_bf16 = pltpu.VMEM((TILE_T, D_MODEL), BF16)
    conv_tail = pltpu.VMEM(((CONV_W - 1) * SUBLANES, D_MODEL), F32)
    scratch = [
        slab_bf16,
        conv_tail, conv_tail,
        slab_bf16,
        slab_f32,
        slab_bf16, slab_bf16, slab_bf16,
        slab_f32,
        slab_bf16,
        slab_bf16,
        slab_f32, slab_f32, slab_f32, slab_f32,
        pltpu.VMEM((SUBLANES, D_MODEL), F32),
        pltpu.VMEM((H_M, DH_M, DH_M), F32),
        pltpu.VMEM((SUBLANES, DH_M), F32),
        pltpu.VMEM((1, LANES), F32),
        slab_bf16,
        slab_f32,
    ]
    return pl.pallas_call(
        functools.partial(_block_kernel, tiles_per_seq=nt),
        out_shape=jax.ShapeDtypeStruct(x.shape, x.dtype),
        grid=(n_tiles + 1,),
        in_specs=[x_tile] + [resident] * 11,
        out_specs=o_tile,
        scratch_shapes=scratch,
        compiler_params=pltpu.CompilerParams(
            dimension_semantics=("arbitrary",),
            vmem_limit_bytes=VMEM_LIMIT_BYTES),
        name="hybrid_block",
    )(x, w_main, w_gate, w_if, wax, wqk, ml_w_v.astype(BF16),
      w_tail, params,
      perm, perm.T, causal_p)


def kernel(x, g_pre, w_in, rg_conv_w, rg_conv_b, rg_w_a, rg_b_a, rg_w_x, rg_b_x, rg_lambda,
           ml_conv_w, ml_conv_b, ml_w_q, ml_w_k, ml_w_v, ml_b_i, ml_b_f, ml_norm_g, ml_skip,
           w_branch_r, w_branch_m, w_out, g_post):
    h = x
    for l in range(g_pre.shape[0]):
        h = _layer(h, g_pre[l], w_in[l], rg_conv_w[l], rg_conv_b[l], rg_w_a[l], rg_b_a[l],
                   rg_w_x[l], rg_b_x[l], rg_lambda[l], ml_conv_w[l], ml_conv_b[l],
                   ml_w_q[l], ml_w_k[l], ml_w_v[l], ml_b_i[l], ml_b_f[l], ml_norm_g[l],
                   ml_skip[l], w_branch_r[l], w_branch_m[l], w_out[l], g_post[l])
    return h
```

```python
import functools
import math

import jax
import jax.numpy as jnp
import numpy as np
from jax import lax
from jax.experimental import pallas as pl
from jax.experimental.pallas import tpu as pltpu

D_MODEL = 1024
H_R = 16
DH_R = D_MODEL // H_R
RG_C = 8.0
CONV_W = 4
H_M = 4
DH_M = D_MODEL // H_M
EPS = 1e-6

TILE_T = 256
CHUNK = TILE_T
COL_G = 256
N_G = D_MODEL // COL_G
SUBLANES = 8
LANES = 128
SEG = CHUNK // SUBLANES
VMEM_LIMIT_BYTES = 56 * 1024 * 1024

PAIR = 2 * COL_G
OFF_RX, OFF_RZ, OFF_MX, OFF_MZ, OFF_MO = (j * D_MODEL for j in range(5))
OFF_GR, OFF_GM = 0, D_MODEL
OFF_WBR, OFF_WBM, OFF_WOUT = 0, D_MODEL, 2 * D_MODEL

(P_GPRE, P_RCW, P_RCB, P_RBA, P_RBX, P_LAM, P_MCW, P_MCB, P_NORMG, P_SKIP, P_GPOST, P_BIF) = (
    0, 1, 5, 6, 7, 8, 9, 13, 14, 15, 16, 17)
P_ROWS = 24

F32 = jnp.float32
BF16 = jnp.bfloat16


def _dot(a, b):
    return jnp.dot(a, b, preferred_element_type=F32)


def _silu(x):
    return x * jax.nn.sigmoid(x)


def _step_deps():
    d = {"pre": ()}
    for g in range(N_G):
        d[f"fill{g}"] = ("pre",)
        d[f"rg_dot{g}"] = ("pre",)
        d[f"rg_conv{g}"] = (f"rg_dot{g}",)
        d[f"rg_gdot{g}"] = (f"rg_conv{g}",)
        d[f"rg_gates{g}"] = (f"rg_gdot{g}",)
        d[f"rg_scan{g}"] = (f"rg_gates{g}",)
        d[f"rg_out{g}"] = (f"rg_scan{g}",)
        d[f"mm_dot{g}"] = ("pre",)
        d[f"mm_conv{g}"] = (f"mm_dot{g}",)
        d[f"mm_qkv{g}"] = (f"mm_conv{g}",)
        d[f"ck_k{g}"] = (f"mm_qkv{g}", "gv_vec")
        d[f"ck_a{g}"] = (f"ck_k{g}",)
        d[f"ck_b{g}"] = (f"ck_a{g}",)
        d[f"ck_c{g}"] = (f"ck_b{g}",)
        d[f"ck_d{g}"] = (f"ck_c{g}",)
        d[f"ck_f{g}"] = (f"ck_a{g}",)
        d[f"epi{g}"] = (f"ck_d{g}", f"mm_dot{g}", f"mm_conv{g}", f"fill{g}")
    for j in range(N_G // 2):
        d[f"fill{N_G + j}"] = ("pre",)
        d[f"br{j}"] = tuple(f"rg_out{g}" for g in range(N_G))
        d[f"bm{j}"] = tuple(f"epi{g}" for g in range(N_G))
        d[f"merge{j}"] = (f"br{j}", f"bm{j}", f"fill{2 * j}", f"fill{2 * j + 1}", f"fill{N_G + j}")
    d["gv_dot"] = ("pre",)
    d["gv_z"] = ("gv_dot",)
    d["gv_cum"] = ("gv_z",)
    d["gv_vec"] = ("gv_cum",)
    d["out"] = ()
    d["post"] = ("out",)
    d["keep_x"] = ("pre", "post")
    d["unperm"] = tuple(f"merge{j}" for j in range(N_G // 2)) + ("out",)
    return d


STEP_DEPS = _step_deps()


def _issue_order():
    assert (H_M, N_G) == (4, 4)
    head = ["out", "post", "pre", "mm_dot0", "keep_x", "gv_dot", "mm_conv0", "mm_dot2", "mm_dot1",
            "mm_qkv0", "gv_z", "mm_conv1", "mm_dot3", "mm_qkv1", "mm_conv2", "gv_cum", "rg_dot0",
            "mm_qkv2", "mm_conv3", "mm_qkv3", "rg_dot1", "gv_vec"]
    per_head = [
        ["ck_k0", "ck_a0", "rg_conv0", "ck_b0", "rg_gdot0", "ck_c0", "fill0", "ck_d0", "rg_gates0",
         "ck_f0", "rg_scan0", "rg_out0", "epi0"],
        ["ck_k1", "rg_dot2", "rg_conv1", "ck_a1", "ck_b1", "rg_gdot1", "ck_c1", "fill1", "ck_d1",
         "rg_gates1", "ck_f1", "rg_scan1", "rg_out1", "fill2", "epi1"],
        ["ck_k2", "ck_a2", "rg_conv2", "ck_b2", "rg_gdot2", "ck_c2", "fill3", "ck_d2", "rg_gates2",
         "ck_f2", "rg_dot3", "rg_scan2", "rg_out2", "epi2"],
        ["ck_k3", "ck_a3", "rg_conv3", "ck_b3", "fill4", "rg_gdot3", "ck_c3", "fill5", "ck_d3",
         "rg_gates3", "ck_f3", "rg_scan3", "rg_out3", "br1", "br0", "epi3"],
    ]
    tail = ["bm0", "bm1", "merge0", "merge1", "unperm"]
    return head + [name for steps_of_head in per_head for name in steps_of_head] + tail


ORDER = _issue_order()


def _linear_scan(a, b, h0):
    tn, c = a.shape
    a3 = a.reshape(tn // SUBLANES, SUBLANES, c)
    b3 = b.reshape(tn // SUBLANES, SUBLANES, c)
    sub = lax.broadcasted_iota(jnp.int32, (SUBLANES, c), 0)
    hs = []
    carry = h0
    for ck in range(tn // CHUNK):
        base = ck * SEG
        h = b3[base]
        p = a3[base]
        h_loc, p_loc = [h], [p]
        for r in range(1, SEG):
            h = a3[base + r] * h + b3[base + r]
            p = a3[base + r] * p
            h_loc.append(h)
            p_loc.append(p)
        s = 1
        while s < SUBLANES:
            keep = sub >= s
            h = jnp.where(keep, p * pltpu.roll(h, s, 0) + h, h)
            p = jnp.where(keep, p * pltpu.roll(p, s, 0), p)
            s *= 2
        end_state = p * carry + h
        seg_in = jnp.where(sub >= 1, pltpu.roll(end_state, 1, 0), carry)
        for r in range(SEG):
            hs.append(h_loc[r] + p_loc[r] * seg_in)
        carry = end_state[SUBLANES - 1:SUBLANES, :]
    return jnp.stack(hs, axis=0).reshape(tn, c), carry


def _causal_conv(tail_ref, cs, y, w_rows, bias):
    tn, c = y.shape
    n_tail = CONV_W - 1
    y3 = y.reshape(tn // SUBLANES, SUBLANES, c)
    first = lax.broadcasted_iota(jnp.int32, (n_tail, SUBLANES, c), 1) == 0
    prev_last = tail_ref[:, cs].reshape(n_tail, SUBLANES, c)
    outs = []
    for ck in range(tn // CHUNK):
        cur = y3[ck * SEG:(ck + 1) * SEG]
        wrapped = jnp.where(first, pltpu.roll(prev_last, 1, 1), pltpu.roll(cur[SEG - n_tail:], 1, 1))
        acc = bias + w_rows[CONV_W - 1] * cur
        for shift in range(1, CONV_W):
            shifted = jnp.concatenate([wrapped[n_tail - shift:], cur[:SEG - shift]], axis=0)
            acc = acc + w_rows[CONV_W - 1 - shift] * shifted
        outs.append(acc)
        prev_last = cur[SEG - n_tail:]
    tail_ref[:, cs] = prev_last.reshape(n_tail * SUBLANES, c)
    return jnp.concatenate(outs, axis=0).reshape(tn, c)


def _block_kernel(x_ref, w_ref, wg_ref, wif_ref, wax_ref, wqk_ref, wv_ref, wt_ref,
                  p_ref, perm_ref, unperm_ref, causal_ref, o_ref,
                  xn_ref, rxb_ref, mxb_ref, yr_ref, c_ref, q_ref, k_ref, v_ref, hm_ref,
                  ym_ref, y_ref, gr_ref, gm_ref, mo_ref, mz_ref,
                  hr_ref, cst_ref, nst_ref, mst_ref, ytime_ref, xprev_ref, *, tiles_per_seq):
    tn = x_ref.shape[0]

    @pl.when(pl.program_id(0) == 0)
    def _no_previous_tile():
        ytime_ref[...] = jnp.zeros_like(ytime_ref)
        xprev_ref[...] = jnp.zeros_like(xprev_ref)

    @pl.when(pl.program_id(0) % tiles_per_seq == 0)
    def _reset_state():
        rxb_ref[...] = jnp.zeros_like(rxb_ref)
        mxb_ref[...] = jnp.zeros_like(mxb_ref)
        hr_ref[...] = jnp.zeros_like(hr_ref)
        cst_ref[...] = jnp.zeros_like(cst_ref)
        nst_ref[...] = jnp.zeros_like(nst_ref)
        mst_ref[...] = jnp.zeros_like(mst_ref)

    def prow(r, cs):
        return p_ref[r:r + 1, cs]

    val = {}
    steps = {}

    def step(name):
        def register(fn):
            steps[name] = fn
            return fn
        return register

    @step("pre")
    def _():
        x = x_ref[...]
        ms = jnp.mean(x * x, axis=-1, keepdims=True)
        xn_time = (x * lax.rsqrt(ms + EPS) * p_ref[P_GPRE:P_GPRE + 1, :]).astype(BF16)
        xn_ref[...] = _dot(perm_ref[...], xn_time).astype(BF16)

    def fill_mz_gr(g):
        cs = slice(g * COL_G, (g + 1) * COL_G)
        xn = xn_ref[...]
        mz_ref[:, cs] = _dot(xn, w_ref[:, OFF_MZ + g * COL_G:OFF_MZ + (g + 1) * COL_G])
        gr_ref[:, cs] = _dot(xn, wg_ref[:, OFF_GR + g * COL_G:OFF_GR + (g + 1) * COL_G])

    def fill_gm(j):
        gm_ref[:, j * PAIR:(j + 1) * PAIR] = _dot(xn_ref[...], wg_ref[:, OFF_GM + j * PAIR:OFF_GM + (j + 1) * PAIR])

    for g in range(N_G):
        steps[f"fill{g}"] = functools.partial(fill_mz_gr, g)
    for j in range(N_G // 2):
        steps[f"fill{N_G + j}"] = functools.partial(fill_gm, j)

    def rg_dot(g):
        xn = xn_ref[...]
        val[f"rx{g}"] = _dot(xn, w_ref[:, OFF_RX + g * COL_G:OFF_RX + (g + 1) * COL_G])
        val[f"rz{g}"] = _dot(xn, w_ref[:, OFF_RZ + g * COL_G:OFF_RZ + (g + 1) * COL_G])

    def rg_conv(g):
        cs = slice(g * COL_G, (g + 1) * COL_G)
        u = _causal_conv(rxb_ref, cs, val.pop(f"rx{g}"), [prow(P_RCW + j, cs) for j in range(CONV_W)],
                         prow(P_RCB, cs))
        val[f"u{g}"] = u
        val[f"ub{g}"] = u.astype(BF16)

    def rg_gdot(g):
        val[f"gate{g}"] = _dot(val.pop(f"ub{g}"), wax_ref[g])

    def rg_gates(g):
        cs = slice(g * COL_G, (g + 1) * COL_G)
        u = val.pop(f"u{g}")
        gate = val.pop(f"gate{g}")
        r = jax.nn.sigmoid(gate[:, :COL_G] + prow(P_RBA, cs))
        i = jax.nn.sigmoid(gate[:, COL_G:] + prow(P_RBX, cs))
        log_a = (-RG_C * jax.nn.softplus(-prow(P_LAM, cs))) * r
        s_half = jnp.tanh(0.5 * log_a)
        inv = 1.0 / (1.0 - s_half)
        val[f"a{g}"] = (1.0 + s_half) * inv
        val[f"b{g}"] = (2.0 * jnp.sqrt(-s_half) * inv) * (i * u)

    def rg_scan(g):
        cs = slice(g * COL_G, (g + 1) * COL_G)
        h, h_last = _linear_scan(val.pop(f"a{g}"), val.pop(f"b{g}"), hr_ref[0:1, cs])
        hr_ref[0:1, cs] = h_last
        val[f"h{g}"] = h

    def rg_out(g):
        cs = slice(g * COL_G, (g + 1) * COL_G)
        yr_ref[:, cs] = (val.pop(f"h{g}") * _silu(val.pop(f"rz{g}"))).astype(BF16)

    def mm_dot(hd):
        cs = slice(hd * DH_M, (hd + 1) * DH_M)
        xn = xn_ref[...]
        val[f"mx{hd}"] = _dot(xn, w_ref[:, OFF_MX + hd * DH_M:OFF_MX + (hd + 1) * DH_M])
        mo_ref[:, cs] = _dot(xn, w_ref[:, OFF_MO + hd * DH_M:OFF_MO + (hd + 1) * DH_M])

    def mm_conv(hd):
        cs = slice(hd * DH_M, (hd + 1) * DH_M)
        mx = val.pop(f"mx{hd}")
        c = _silu(_causal_conv(mxb_ref, cs, mx, [prow(P_MCW + j, cs) for j in range(CONV_W)],
                               prow(P_MCB, cs)))
        c_ref[:, cs] = c
        val[f"cb{hd}"] = c.astype(BF16)
        val[f"mxb{hd}"] = mx.astype(BF16)

    def mm_qkv(hd):
        cs = slice(hd * DH_M, (hd + 1) * DH_M)
        qk = _dot(val.pop(f"cb{hd}"), wqk_ref[hd])
        q_ref[:, cs] = qk[:, :DH_M].astype(BF16)
        k_ref[:, cs] = qk[:, DH_M:].astype(BF16)
        v_ref[:, cs] = _dot(val.pop(f"mxb{hd}"), wv_ref[hd]).astype(BF16)

    for g in range(N_G):
        for nm, fn in (("rg_dot", rg_dot), ("rg_conv", rg_conv), ("rg_gdot", rg_gdot),
                       ("rg_gates", rg_gates), ("rg_scan", rg_scan), ("rg_out", rg_out),
                       ("mm_dot", mm_dot), ("mm_conv", mm_conv), ("mm_qkv", mm_qkv)):
            steps[f"{nm}{g}"] = functools.partial(fn, g)

    @step("gv_dot")
    def _():
        val["pre_if"] = _dot(xn_ref[...], wif_ref[...])

    @step("gv_z")
    def _():
        pre_if = val.pop("pre_if") + p_ref[P_BIF:P_BIF + 1, 0:LANES]
        lane = lax.broadcasted_iota(jnp.int32, pre_if.shape, 1)
        val["z"] = jnp.where(lane < H_M, pre_if, jax.nn.log_sigmoid(pre_if))

    @step("gv_cum")
    def _():
        sub = lax.broadcasted_iota(jnp.int32, (SUBLANES, LANES), 0)
        bc = val["z"].reshape(SEG, SUBLANES, LANES)
        s = 1
        while s < SEG:
            bc = jnp.concatenate([bc[:s], bc[s:] + bc[:-s]], axis=0)
            s *= 2
        seg_sum = bc[SEG - 1]
        s = 1
        while s < SUBLANES:
            seg_sum = jnp.where(sub >= s, seg_sum + pltpu.roll(seg_sum, s, 0), seg_sum)
            s *= 2
        before = jnp.where(sub >= 1, pltpu.roll(seg_sum, 1, 0), 0.0)
        val["bc"] = (bc + before).reshape(CHUNK, LANES)

    @step("gv_vec")
    def _():
        z = val.pop("z")
        bc = val.pop("bc")
        sub = lax.broadcasted_iota(jnp.int32, (SUBLANES, LANES), 0)
        m_row = mst_ref[...]
        r = pltpu.roll(z, H_M, 1) - bc
        cm = r.reshape(SEG, SUBLANES, LANES)
        s = 1
        while s < SEG:
            cm = jnp.concatenate([cm[:s], jnp.maximum(cm[s:], cm[:-s])], axis=0)
            s *= 2
        seg_max = cm[SEG - 1]
        s = 1
        while s < SUBLANES:
            seg_max = jnp.where(sub >= s, jnp.maximum(seg_max, pltpu.roll(seg_max, s, 0)), seg_max)
            s *= 2
        before = jnp.where(sub >= 1, pltpu.roll(seg_max, 1, 0), -jnp.inf)
        cm = jnp.maximum(cm, before).reshape(CHUNK, LANES)
        mj = jnp.maximum(m_row, cm)
        mj_last = mj[CHUNK - 1:CHUNK, :]
        val["gt"] = dict(
            rt=r.T,
            mj=mj,
            e_inter=jnp.exp(m_row - mj),
            e_negm=jnp.exp(-(bc + mj)),
            wg=jnp.exp(r - mj_last),
            decay=jnp.exp(m_row - mj_last))
        mst_ref[...] = bc[CHUNK - 1:CHUNK, :] + mj_last

    def ck_k(hd):
        cs = slice(hd * DH_M, (hd + 1) * DH_M)
        ln = slice(H_M + hd, H_M + hd + 1)
        gt = val["gt"]
        kw = k_ref[:, cs].astype(F32) * gt["wg"][:, ln]
        n_old = nst_ref[hd:hd + 1, :]
        val[f"n{hd}"] = n_old
        nst_ref[hd:hd + 1, :] = gt["decay"][:, ln] * n_old + jnp.sum(kw, axis=0, keepdims=True)
        val[f"kw{hd}"] = kw.astype(BF16)

    def ck_a(hd):
        cs = slice(hd * DH_M, (hd + 1) * DH_M)
        q = q_ref[:, cs]
        val[f"qk{hd}"] = lax.dot_general(q, k_ref[:, cs], (((1,), (1,)), ((), ())),
                                        preferred_element_type=F32)
        val[f"qc{hd}"] = _dot(q, cst_ref[hd].astype(BF16))
        val[f"kv{hd}"] = lax.dot_general(val.pop(f"kw{hd}"), v_ref[:, cs], (((0,), (0,)), ((), ())),
                                        preferred_element_type=F32)

    def ck_b(hd):
        ln = slice(H_M + hd, H_M + hd + 1)
        gt = val["gt"]
        causal = causal_ref[...] > 0.5
        w = jnp.where(causal, jnp.exp(gt["rt"][ln, :] - gt["mj"][:, ln]), 0.0)
        s_qk = val.pop(f"qk{hd}") * w
        val[f"rowsum{hd}"] = jnp.sum(s_qk, axis=-1, keepdims=True)
        val[f"s{hd}"] = s_qk.astype(BF16)

    def ck_c(hd):
        val[f"sv{hd}"] = _dot(val.pop(f"s{hd}"), v_ref[:, hd * DH_M:(hd + 1) * DH_M])

    def ck_d(hd):
        cs = slice(hd * DH_M, (hd + 1) * DH_M)
        ln = slice(H_M + hd, H_M + hd + 1)
        gt = val["gt"]
        e_inter = gt["e_inter"][:, ln]
        num = val.pop(f"sv{hd}") + e_inter * val.pop(f"qc{hd}")
        den = val.pop(f"rowsum{hd}") + \
            e_inter * jnp.sum(q_ref[:, cs].astype(F32) * val.pop(f"n{hd}"), axis=-1, keepdims=True)
        hm_ref[:, cs] = num * (1.0 / jnp.maximum(jnp.abs(den), gt["e_negm"][:, ln]))

    def ck_f(hd):
        ln = slice(H_M + hd, H_M + hd + 1)
        cst_ref[hd] = val["gt"]["decay"][:, ln] * cst_ref[hd] + val.pop(f"kv{hd}")

    for hd in range(H_M):
        for nm, fn in (("ck_k", ck_k), ("ck_a", ck_a), ("ck_b", ck_b), ("ck_c", ck_c), ("ck_d", ck_d),
                       ("ck_f", ck_f)):
            steps[f"{nm}{hd}"] = functools.partial(fn, hd)

    def epilogue(hd):
        cs = slice(hd * DH_M, (hd + 1) * DH_M)
        hg = jax.nn.sigmoid(mo_ref[:, cs]) * hm_ref[:, cs]
        mu = jnp.mean(hg, axis=-1, keepdims=True)
        dlt = hg - mu
        var = jnp.mean(dlt * dlt, axis=-1, keepdims=True)
        hn = dlt * lax.rsqrt(var + EPS) * prow(P_NORMG, cs) + prow(P_SKIP, cs) * c_ref[:, cs]
        ym_ref[:, cs] = (hn * _silu(mz_ref[:, cs])).astype(BF16)

    for hd in range(H_M):
        steps[f"epi{hd}"] = functools.partial(epilogue, hd)

    def br(j):
        val[f"br{j}"] = _dot(yr_ref[...], wt_ref[:, OFF_WBR + j * PAIR:OFF_WBR + (j + 1) * PAIR])

    def bm(j):
        val[f"bm{j}"] = _dot(ym_ref[...], wt_ref[:, OFF_WBM + j * PAIR:OFF_WBM + (j + 1) * PAIR])

    def merge(j):
        cs = slice(j * PAIR, (j + 1) * PAIR)
        y = jax.nn.sigmoid(gr_ref[:, cs]) * val.pop(f"br{j}") + \
            jax.nn.sigmoid(gm_ref[:, cs]) * val.pop(f"bm{j}")
        y_ref[:, cs] = y.astype(BF16)

    for j in range(N_G // 2):
        steps[f"br{j}"] = functools.partial(br, j)
        steps[f"bm{j}"] = functools.partial(bm, j)
        steps[f"merge{j}"] = functools.partial(merge, j)

    @step("unperm")
    def _():
        ytime_ref[...] = _dot(unperm_ref[...], y_ref[...]).astype(BF16)

    @step("out")
    def _():
        val["out"] = _dot(ytime_ref[...], wt_ref[:, OFF_WOUT:OFF_WOUT + D_MODEL])

    @step("post")
    def _():
        out = val.pop("out")
        ms_o = jnp.mean(out * out, axis=-1, keepdims=True)
        o_ref[...] = xprev_ref[...] + out * lax.rsqrt(ms_o + EPS) * p_ref[P_GPOST:P_GPOST + 1, :]

    @step("keep_x")
    def _():
        xprev_ref[...] = x_ref[...]

    done = set()
    for name in ORDER:
        assert all(d in done for d in STEP_DEPS[name]), (name, STEP_DEPS[name])
        steps[name]()
        done.add(name)
    assert done == set(steps), set(steps) - done


def _block_diag(w, group):
    h, d, e = w.shape
    across = jnp.tile(w.reshape(h // group, group, d, e), (1, 1, 1, group))
    on_diagonal = np.arange(group * e)[None, :] // e == np.arange(group)[:, None]
    return jnp.where(on_diagonal[None, :, None, :], across, 0.0).reshape(h // group, group * d, group * e)


def _row_order_constants():
    p = np.arange(TILE_T)
    q = p % CHUNK
    time_of = (p // CHUNK) * CHUNK + (q % SUBLANES) * SEG + q // SUBLANES
    perm = np.zeros((TILE_T, TILE_T), np.float32)
    perm[p, time_of] = 1.0
    t = time_of[:CHUNK]
    causal_p = (t[None, :] <= t[:, None]).astype(np.float32)
    return jnp.asarray(perm, BF16), jnp.asarray(causal_p, F32)


def _layer(x, g_pre, w_in, rg_conv_w, rg_conv_b, rg_w_a, rg_b_a, rg_w_x, rg_b_x, rg_lambda,
           ml_conv_w, ml_conv_b, ml_w_q, ml_w_k, ml_w_v, ml_b_i, ml_b_f, ml_norm_g, ml_skip,
           w_branch_r, w_branch_m, w_out, g_post):
    bsz, seq, d = x.shape
    assert d == D_MODEL and seq % TILE_T == 0

    n_if = 2 * H_M
    o_i = 5 * D_MODEL
    def odd_width(*ws):
        width = sum(w.shape[-1] for w in ws)
        tiles = -(-width // LANES)
        zeros = jnp.zeros((ws[0].shape[0], (tiles + 1 - tiles % 2) * LANES - width), ws[0].dtype)
        return jnp.concatenate(list(ws) + [zeros], axis=1).astype(BF16)

    w_main = odd_width(w_in[:, :o_i])
    w_gate = odd_width(w_in[:, o_i + n_if:])
    w_if = odd_width(w_in[:, o_i:o_i + n_if])
    w_tail = odd_width(w_branch_r, w_branch_m, w_out)

    group = COL_G // DH_R
    wax = jnp.concatenate([_block_diag(rg_w_a, group), _block_diag(rg_w_x, group)], axis=2).astype(BF16)
    wqk = jnp.concatenate([ml_w_q, ml_w_k * (1.0 / math.sqrt(DH_M))], axis=2).astype(BF16)

    b_if = jnp.concatenate([ml_b_i, ml_b_f, jnp.zeros((D_MODEL - n_if,), F32)])
    rows = [g_pre[None], rg_conv_w, rg_conv_b[None], rg_b_a[None], rg_b_x[None], rg_lambda[None],
            ml_conv_w, ml_conv_b[None], ml_norm_g[None], ml_skip[None], g_post[None], b_if[None]]
    n_rows = sum(r.shape[0] for r in rows)
    params = jnp.concatenate(rows + [jnp.zeros((P_ROWS - n_rows, D_MODEL), F32)], axis=0)

    perm, causal_p = _row_order_constants()

    nt = seq // TILE_T
    n_tiles = bsz * nt
    resident = pl.BlockSpec(memory_space=pltpu.VMEM)

    def tile_in(i):
        j = jnp.minimum(i, n_tiles - 1)
        return (j // nt, j % nt, 0)

    def tile_out(i):
        j = jnp.maximum(i - 1, 0)
        return (j // nt, j % nt, 0)

    x_tile = pl.BlockSpec((None, TILE_T, D_MODEL), tile_in)
    o_tile = pl.BlockSpec((None, TILE_T, D_MODEL), tile_out)
    slab_f32 = pltpu.VMEM((TILE_T, D_MODEL), F32)
    slab_bf16 = pltpu.VMEM((TILE_T, D_MODEL), BF16)
    conv_tail = pltpu.VMEM(((CONV_W - 1) * SUBLANES, D_MODEL), F32)
    scratch = [
        slab_bf16,
        conv_tail, conv_tail,
        slab_bf16,
        slab_f32,
        slab_bf16, slab_bf16, slab_bf16,
        slab_f32,
        slab_bf16,
        slab_bf16,
        slab_f32, slab_f32, slab_f32, slab_f32,
        pltpu.VMEM((SUBLANES, D_MODEL), F32),
        pltpu.VMEM((H_M, DH_M, DH_M), F32),
        pltpu.VMEM((SUBLANES, DH_M), F32),
        pltpu.VMEM((1, LANES), F32),
        slab_bf16,
        slab_f32,
    ]
    return pl.pallas_call(
        functools.partial(_block_kernel, tiles_per_seq=nt),
        out_shape=jax.ShapeDtypeStruct(x.shape, x.dtype),
        grid=(n_tiles + 1,),
        in_specs=[x_tile] + [resident] * 11,
        out_specs=o_tile,
        scratch_shapes=scratch,
        compiler_params=pltpu.CompilerParams(
            dimension_semantics=("arbitrary",),
            vmem_limit_bytes=VMEM_LIMIT_BYTES),
        name="hybrid_block",
    )(x, w_main, w_gate, w_if, wax, wqk, ml_w_v.astype(BF16),
      w_tail, params,
      perm, perm.T, causal_p)


def kernel(x, g_pre, w_in, rg_conv_w, rg_conv_b, rg_w_a, rg_b_a, rg_w_x, rg_b_x, rg_lambda,
           ml_conv_w, ml_conv_b, ml_w_q, ml_w_k, ml_w_v, ml_b_i, ml_b_f, ml_norm_g, ml_skip,
           w_branch_r, w_branch_m, w_out, g_post):
    h = x
    for l in range(g_pre.shape[0]):
        h = _layer(h, g_pre[l], w_in[l], rg_conv_w[l], rg_conv_b[l], rg_w_a[l], rg_b_a[l],
                   rg_w_x[l], rg_b_x[l], rg_lambda[l], ml_conv_w[l], ml_conv_b[l],
                   ml_w_q[l], ml_w_k[l], ml_w_v[l], ml_b_i[l], ml_b_f[l], ml_norm_g[l],
                   ml_skip[l], w_branch_r[l], w_branch_m[l], w_out[l], g_post[l])
    return h
```

```python
import functools
import math

import jax
import jax.numpy as jnp
import numpy as np
from jax import lax
from jax.experimental import pallas as pl
from jax.experimental.pallas import tpu as pltpu

D_MODEL = 1024
H_R = 16
DH_R = D_MODEL // H_R
RG_C = 8.0
CONV_W = 4
H_M = 4
DH_M = D_MODEL // H_M
EPS = 1e-6

TILE_T = 256
CHUNK = TILE_T
COL_G = 256
N_G = D_MODEL // COL_G
SUBLANES = 8
LANES = 128
SEG = CHUNK // SUBLANES
VMEM_LIMIT_BYTES = 56 * 1024 * 1024

PAIR = 2 * COL_G
CAST_ROWS = 128
OFF_RX, OFF_RZ, OFF_MX, OFF_MZ, OFF_MO = (j * D_MODEL for j in range(5))
OFF_GR, OFF_GM = 0, D_MODEL
OFF_WBR, OFF_WBM, OFF_WOUT = 0, D_MODEL, 2 * D_MODEL

(P_GPRE, P_RCW, P_RCB, P_RBA, P_RBX, P_LAM, P_MCW, P_MCB, P_NORMG, P_SKIP, P_GPOST, P_BIF) = (
    0, 1, 5, 6, 7, 8, 9, 13, 14, 15, 16, 17)
P_ROWS = 24

F32 = jnp.float32
BF16 = jnp.bfloat16


def _dot(a, b):
    return jnp.dot(a, b, preferred_element_type=F32)


def _silu(x):
    return x * jax.nn.sigmoid(x)


def _step_deps():
    d = {"pre": ()}
    for g in range(N_G):
        d[f"fill{g}"] = ("pre",)
        d[f"rg_dot{g}"] = ("pre",)
        d[f"rg_conv{g}"] = (f"rg_dot{g}",)
        d[f"rg_gdot{g}"] = (f"rg_conv{g}",)
        d[f"rg_gates{g}"] = (f"rg_gdot{g}",)
        d[f"rg_scan{g}"] = (f"rg_gates{g}",)
        d[f"rg_out{g}"] = (f"rg_scan{g}",)
        d[f"mm_dot{g}"] = ("pre",)
        d[f"mm_conv{g}"] = (f"mm_dot{g}",)
        d[f"mm_qkv{g}"] = (f"mm_conv{g}",)
        d[f"ck_k{g}"] = (f"mm_qkv{g}", "gv_vec")
        d[f"ck_a{g}"] = (f"ck_k{g}",)
        d[f"ck_b{g}"] = (f"ck_a{g}",)
        d[f"ck_c{g}"] = (f"ck_b{g}",)
        d[f"ck_d{g}"] = (f"ck_c{g}",)
        d[f"ck_f{g}"] = (f"ck_a{g}",)
        d[f"epi{g}"] = (f"ck_d{g}", f"mm_dot{g}", f"mm_conv{g}", f"fill{g}")
    for j in range(N_G // 2):
        d[f"fill{N_G + j}"] = ("pre",)
        d[f"br{j}"] = tuple(f"rg_out{g}" for g in range(N_G))
        d[f"bm{j}"] = tuple(f"epi{g}" for g in range(N_G))
        d[f"merge{j}"] = (f"br{j}", f"bm{j}", f"fill{2 * j}", f"fill{2 * j + 1}", f"fill{N_G + j}")
    d["gv_dot"] = ("pre",)
    d["gv_z"] = ("gv_dot",)
    d["gv_cum"] = ("gv_z",)
    d["gv_vec"] = ("gv_cum",)
    d["out"] = ()
    d["post"] = ("out",)
    d["keep_x"] = ("pre", "post")
    d["unperm"] = tuple(f"merge{j}" for j in range(N_G // 2)) + ("out",)
    return d


STEP_DEPS = _step_deps()


def _issue_order():
    assert (H_M, N_G) == (4, 4)
    head = ["out", "post", "pre", "mm_dot0", "keep_x", "gv_dot", "mm_conv0", "mm_dot2", "mm_dot1",
            "mm_qkv0", "gv_z", "mm_conv1", "mm_dot3", "mm_qkv1", "mm_conv2", "gv_cum", "rg_dot0",
            "mm_qkv2", "mm_conv3", "mm_qkv3", "rg_dot1", "gv_vec"]
    per_head = [
        ["ck_k0", "ck_a0", "rg_conv0", "ck_b0", "rg_gdot0", "ck_c0", "fill0", "ck_d0", "rg_gates0",
         "ck_f0", "rg_scan0", "rg_out0", "epi0"],
        ["ck_k1", "rg_dot2", "rg_conv1", "ck_a1", "ck_b1", "rg_gdot1", "ck_c1", "fill1", "ck_d1",
         "rg_gates1", "ck_f1", "rg_scan1", "rg_out1", "fill2", "epi1"],
        ["ck_k2", "ck_a2", "rg_conv2", "ck_b2", "rg_gdot2", "ck_c2", "fill3", "ck_d2", "rg_gates2",
         "ck_f2", "rg_dot3", "rg_scan2", "rg_out2", "epi2"],
        ["ck_k3", "ck_a3", "rg_conv3", "ck_b3", "fill4", "rg_gdot3", "ck_c3", "fill5", "ck_d3",
         "rg_gates3", "ck_f3", "rg_scan3", "rg_out3", "br1", "br0", "epi3"],
    ]
    tail = ["bm0", "bm1", "merge0", "merge1", "unperm"]
    return head + [name for steps_of_head in per_head for name in steps_of_head] + tail


ORDER = _issue_order()


def _linear_scan(a, b, h0):
    tn, c = a.shape
    a3 = a.reshape(tn // SUBLANES, SUBLANES, c)
    b3 = b.reshape(tn // SUBLANES, SUBLANES, c)
    sub = lax.broadcasted_iota(jnp.int32, (SUBLANES, c), 0)
    hs = []
    carry = h0
    for ck in range(tn // CHUNK):
        base = ck * SEG
        h = b3[base]
        p = a3[base]
        h_loc, p_loc = [h], [p]
        for r in range(1, SEG):
            h = a3[base + r] * h + b3[base + r]
            p = a3[base + r] * p
            h_loc.append(h)
            p_loc.append(p)
        s = 1
        while s < SUBLANES:
            keep = sub >= s
            h = jnp.where(keep, p * pltpu.roll(h, s, 0) + h, h)
            p = jnp.where(keep, p * pltpu.roll(p, s, 0), p)
            s *= 2
        end_state = p * carry + h
        seg_in = jnp.where(sub >= 1, pltpu.roll(end_state, 1, 0), carry)
        for r in range(SEG):
            hs.append(h_loc[r] + p_loc[r] * seg_in)
        carry = end_state[SUBLANES - 1:SUBLANES, :]
    return jnp.stack(hs, axis=0).reshape(tn, c), carry


def _causal_conv(tail_ref, cs, y, w_rows, bias):
    tn, c = y.shape
    n_tail = CONV_W - 1
    y3 = y.reshape(tn // SUBLANES, SUBLANES, c)
    first = lax.broadcasted_iota(jnp.int32, (n_tail, SUBLANES, c), 1) == 0
    prev_last = tail_ref[:, cs].reshape(n_tail, SUBLANES, c)
    outs = []
    for ck in range(tn // CHUNK):
        cur = y3[ck * SEG:(ck + 1) * SEG]
        wrapped = jnp.where(first, pltpu.roll(prev_last, 1, 1), pltpu.roll(cur[SEG - n_tail:], 1, 1))
        acc = bias + w_rows[CONV_W - 1] * cur
        for shift in range(1, CONV_W):
            shifted = jnp.concatenate([wrapped[n_tail - shift:], cur[:SEG - shift]], axis=0)
            acc = acc + w_rows[CONV_W - 1 - shift] * shifted
        outs.append(acc)
        prev_last = cur[SEG - n_tail:]
    tail_ref[:, cs] = prev_last.reshape(n_tail * SUBLANES, c)
    return jnp.concatenate(outs, axis=0).reshape(tn, c)


def _block_kernel(x_ref, w_ref, wg_ref, wif_ref, wax_ref, wqk_ref, wv_ref, wt_ref,
                  p_ref, perm_ref, unperm_ref, causal_ref, o_ref,
                  xn_ref, rxb_ref, mxb_ref, yr_ref, c_ref, q_ref, k_ref, v_ref, hm_ref,
                  ym_ref, y_ref, gr_ref, gm_ref, mo_ref, mz_ref,
                  hr_ref, cst_ref, nst_ref, mst_ref, ytime_ref, xprev_ref, *, tiles_per_seq):
    tn = x_ref.shape[0]

    @pl.when(pl.program_id(0) == 0)
    def _no_previous_tile():
        ytime_ref[...] = jnp.zeros_like(ytime_ref)
        xprev_ref[...] = jnp.zeros_like(xprev_ref)

    @pl.when(pl.program_id(0) % tiles_per_seq == 0)
    def _reset_state():
        rxb_ref[...] = jnp.zeros_like(rxb_ref)
        mxb_ref[...] = jnp.zeros_like(mxb_ref)
        hr_ref[...] = jnp.zeros_like(hr_ref)
        cst_ref[...] = jnp.zeros_like(cst_ref)
        nst_ref[...] = jnp.zeros_like(nst_ref)
        mst_ref[...] = jnp.zeros_like(mst_ref)

    def prow(r, cs):
        return p_ref[r:r + 1, cs]

    val = {}
    steps = {}

    def step(name):
        def register(fn):
            steps[name] = fn
            return fn
        return register

    @step("pre")
    def _():
        x = x_ref[...]
        ms = jnp.mean(x * x, axis=-1, keepdims=True)
        xn_time = (x * lax.rsqrt(ms + EPS) * p_ref[P_GPRE:P_GPRE + 1, :]).astype(BF16)
        xn_ref[...] = _dot(perm_ref[...], xn_time).astype(BF16)

    def fill_mz_gr(g):
        cs = slice(g * COL_G, (g + 1) * COL_G)
        xn = xn_ref[...]
        mz_ref[:, cs] = _dot(xn, w_ref[:, OFF_MZ + g * COL_G:OFF_MZ + (g + 1) * COL_G])
        gr_ref[:, cs] = _dot(xn, wg_ref[:, OFF_GR + g * COL_G:OFF_GR + (g + 1) * COL_G])

    def fill_gm(j):
        gm_ref[:, j * PAIR:(j + 1) * PAIR] = _dot(xn_ref[...], wg_ref[:, OFF_GM + j * PAIR:OFF_GM + (j + 1) * PAIR])

    for g in range(N_G):
        steps[f"fill{g}"] = functools.partial(fill_mz_gr, g)
    for j in range(N_G // 2):
        steps[f"fill{N_G + j}"] = functools.partial(fill_gm, j)

    def rg_dot(g):
        xn = xn_ref[...]
        val[f"rx{g}"] = _dot(xn, w_ref[:, OFF_RX + g * COL_G:OFF_RX + (g + 1) * COL_G])
        val[f"rz{g}"] = _dot(xn, w_ref[:, OFF_RZ + g * COL_G:OFF_RZ + (g + 1) * COL_G])

    def rg_conv(g):
        cs = slice(g * COL_G, (g + 1) * COL_G)
        u = _causal_conv(rxb_ref, cs, val.pop(f"rx{g}"), [prow(P_RCW + j, cs) for j in range(CONV_W)],
                         prow(P_RCB, cs))
        val[f"u{g}"] = u
        val[f"ub{g}"] = u.astype(BF16)

    def rg_gdot(g):
        val[f"gate{g}"] = _dot(val.pop(f"ub{g}"), wax_ref[g])

    def rg_gates(g):
        cs = slice(g * COL_G, (g + 1) * COL_G)
        u = val.pop(f"u{g}")
        gate = val.pop(f"gate{g}")
        r = jax.nn.sigmoid(gate[:, :COL_G] + prow(P_RBA, cs))
        i = jax.nn.sigmoid(gate[:, COL_G:] + prow(P_RBX, cs))
        log_a = (-RG_C * jax.nn.softplus(-prow(P_LAM, cs))) * r
        s_half = jnp.tanh(0.5 * log_a)
        inv = 1.0 / (1.0 - s_half)
        val[f"a{g}"] = (1.0 + s_half) * inv
        val[f"b{g}"] = (2.0 * jnp.sqrt(-s_half) * inv) * (i * u)

    def rg_scan(g):
        cs = slice(g * COL_G, (g + 1) * COL_G)
        h, h_last = _linear_scan(val.pop(f"a{g}"), val.pop(f"b{g}"), hr_ref[0:1, cs])
        hr_ref[0:1, cs] = h_last
        val[f"h{g}"] = h

    def rg_out(g):
        cs = slice(g * COL_G, (g + 1) * COL_G)
        yr_ref[:, cs] = (val.pop(f"h{g}") * _silu(val.pop(f"rz{g}"))).astype(BF16)

    def mm_dot(hd):
        cs = slice(hd * DH_M, (hd + 1) * DH_M)
        xn = xn_ref[...]
        val[f"mx{hd}"] = _dot(xn, w_ref[:, OFF_MX + hd * DH_M:OFF_MX + (hd + 1) * DH_M])
        mo_ref[:, cs] = _dot(xn, w_ref[:, OFF_MO + hd * DH_M:OFF_MO + (hd + 1) * DH_M])

    def mm_conv(hd):
        cs = slice(hd * DH_M, (hd + 1) * DH_M)
        mx = val.pop(f"mx{hd}")
        c = _silu(_causal_conv(mxb_ref, cs, mx, [prow(P_MCW + j, cs) for j in range(CONV_W)],
                               prow(P_MCB, cs)))
        c_ref[:, cs] = c
        val[f"cb{hd}"] = c.astype(BF16)
        val[f"mxb{hd}"] = mx.astype(BF16)

    def mm_qkv(hd):
        cs = slice(hd * DH_M, (hd + 1) * DH_M)
        qk = _dot(val.pop(f"cb{hd}"), wqk_ref[hd])
        q_ref[:, cs] = qk[:, :DH_M].astype(BF16)
        k_ref[:, cs] = qk[:, DH_M:].astype(BF16)
        v_ref[:, cs] = _dot(val.pop(f"mxb{hd}"), wv_ref[hd]).astype(BF16)

    for g in range(N_G):
        for nm, fn in (("rg_dot", rg_dot), ("rg_conv", rg_conv), ("rg_gdot", rg_gdot),
                       ("rg_gates", rg_gates), ("rg_scan", rg_scan), ("rg_out", rg_out),
                       ("mm_dot", mm_dot), ("mm_conv", mm_conv), ("mm_qkv", mm_qkv)):
            steps[f"{nm}{g}"] = functools.partial(fn, g)

    @step("gv_dot")
    def _():
        val["pre_if"] = _dot(xn_ref[...], wif_ref[...])

    @step("gv_z")
    def _():
        pre_if = val.pop("pre_if") + p_ref[P_BIF:P_BIF + 1, 0:LANES]
        lane = lax.broadcasted_iota(jnp.int32, pre_if.shape, 1)
        val["z"] = jnp.where(lane < H_M, pre_if, jax.nn.log_sigmoid(pre_if))

    @step("gv_cum")
    def _():
        sub = lax.broadcasted_iota(jnp.int32, (SUBLANES, LANES), 0)
        bc = val["z"].reshape(SEG, SUBLANES, LANES)
        s = 1
        while s < SEG:
            bc = jnp.concatenate([bc[:s], bc[s:] + bc[:-s]], axis=0)
            s *= 2
        seg_sum = bc[SEG - 1]
        s = 1
        while s < SUBLANES:
            seg_sum = jnp.where(sub >= s, seg_sum + pltpu.roll(seg_sum, s, 0), seg_sum)
            s *= 2
        before = jnp.where(sub >= 1, pltpu.roll(seg_sum, 1, 0), 0.0)
        val["bc"] = (bc + before).reshape(CHUNK, LANES)

    @step("gv_vec")
    def _():
        z = val.pop("z")
        bc = val.pop("bc")
        sub = lax.broadcasted_iota(jnp.int32, (SUBLANES, LANES), 0)
        m_row = mst_ref[...]
        r = pltpu.roll(z, H_M, 1) - bc
        cm = r.reshape(SEG, SUBLANES, LANES)
        s = 1
        while s < SEG:
            cm = jnp.concatenate([cm[:s], jnp.maximum(cm[s:], cm[:-s])], axis=0)
            s *= 2
        seg_max = cm[SEG - 1]
        s = 1
        while s < SUBLANES:
            seg_max = jnp.where(sub >= s, jnp.maximum(seg_max, pltpu.roll(seg_max, s, 0)), seg_max)
            s *= 2
        before = jnp.where(sub >= 1, pltpu.roll(seg_max, 1, 0), -jnp.inf)
        cm = jnp.maximum(cm, before).reshape(CHUNK, LANES)
        mj = jnp.maximum(m_row, cm)
        mj_last = mj[CHUNK - 1:CHUNK, :]
        val["gt"] = dict(
            rt=r.T,
            mj=mj,
            e_inter=jnp.exp(m_row - mj),
            e_negm=jnp.exp(-(bc + mj)),
            wg=jnp.exp(r - mj_last),
            decay=jnp.exp(m_row - mj_last))
        mst_ref[...] = bc[CHUNK - 1:CHUNK, :] + mj_last

    def ck_k(hd):
        cs = slice(hd * DH_M, (hd + 1) * DH_M)
        ln = slice(H_M + hd, H_M + hd + 1)
        gt = val["gt"]
        kw = k_ref[:, cs].astype(F32) * gt["wg"][:, ln]
        n_old = nst_ref[hd:hd + 1, :]
        val[f"n{hd}"] = n_old
        nst_ref[hd:hd + 1, :] = gt["decay"][:, ln] * n_old + jnp.sum(kw, axis=0, keepdims=True)
        val[f"kw{hd}"] = kw.astype(BF16)

    def ck_a(hd):
        cs = slice(hd * DH_M, (hd + 1) * DH_M)
        q = q_ref[:, cs]
        val[f"qk{hd}"] = lax.dot_general(q, k_ref[:, cs], (((1,), (1,)), ((), ())),
                                        preferred_element_type=F32)
        val[f"qc{hd}"] = _dot(q, cst_ref[hd].astype(BF16))
        val[f"kv{hd}"] = lax.dot_general(val.pop(f"kw{hd}"), v_ref[:, cs], (((0,), (0,)), ((), ())),
                                        preferred_element_type=F32)

    def ck_b(hd):
        ln = slice(H_M + hd, H_M + hd + 1)
        gt = val["gt"]
        causal = causal_ref[...] > 0.5
        w = jnp.where(causal, jnp.exp(gt["rt"][ln, :] - gt["mj"][:, ln]), 0.0)
        s_qk = val.pop(f"qk{hd}") * w
        val[f"rowsum{hd}"] = jnp.sum(s_qk, axis=-1, keepdims=True)
        val[f"s{hd}"] = s_qk.astype(BF16)

    def ck_c(hd):
        val[f"sv{hd}"] = _dot(val.pop(f"s{hd}"), v_ref[:, hd * DH_M:(hd + 1) * DH_M])

    def ck_d(hd):
        cs = slice(hd * DH_M, (hd + 1) * DH_M)
        ln = slice(H_M + hd, H_M + hd + 1)
        gt = val["gt"]
        e_inter = gt["e_inter"][:, ln]
        num = val.pop(f"sv{hd}") + e_inter * val.pop(f"qc{hd}")
        den = val.pop(f"rowsum{hd}") + \
            e_inter * jnp.sum(q_ref[:, cs].astype(F32) * val.pop(f"n{hd}"), axis=-1, keepdims=True)
        hm_ref[:, cs] = num * (1.0 / jnp.maximum(jnp.abs(den), gt["e_negm"][:, ln]))

    def ck_f(hd):
        ln = slice(H_M + hd, H_M + hd + 1)
        cst_ref[hd] = val["gt"]["decay"][:, ln] * cst_ref[hd] + val.pop(f"kv{hd}")

    for hd in range(H_M):
        for nm, fn in (("ck_k", ck_k), ("ck_a", ck_a), ("ck_b", ck_b), ("ck_c", ck_c), ("ck_d", ck_d),
                       ("ck_f", ck_f)):
            steps[f"{nm}{hd}"] = functools.partial(fn, hd)

    def epilogue(hd):
        cs = slice(hd * DH_M, (hd + 1) * DH_M)
        hg = jax.nn.sigmoid(mo_ref[:, cs]) * hm_ref[:, cs]
        mu = jnp.mean(hg, axis=-1, keepdims=True)
        dlt = hg - mu
        var = jnp.mean(dlt * dlt, axis=-1, keepdims=True)
        hn = dlt * lax.rsqrt(var + EPS) * prow(P_NORMG, cs) + prow(P_SKIP, cs) * c_ref[:, cs]
        ym_ref[:, cs] = (hn * _silu(mz_ref[:, cs])).astype(BF16)

    for hd in range(H_M):
        steps[f"epi{hd}"] = functools.partial(epilogue, hd)

    def br(j):
        val[f"br{j}"] = _dot(yr_ref[...], wt_ref[:, OFF_WBR + j * PAIR:OFF_WBR + (j + 1) * PAIR])

    def bm(j):
        val[f"bm{j}"] = _dot(ym_ref[...], wt_ref[:, OFF_WBM + j * PAIR:OFF_WBM + (j + 1) * PAIR])

    def merge(j):
        cs = slice(j * PAIR, (j + 1) * PAIR)
        y = jax.nn.sigmoid(gr_ref[:, cs]) * val.pop(f"br{j}") + \
            jax.nn.sigmoid(gm_ref[:, cs]) * val.pop(f"bm{j}")
        y_ref[:, cs] = y.astype(BF16)

    for j in range(N_G // 2):
        steps[f"br{j}"] = functools.partial(br, j)
        steps[f"bm{j}"] = functools.partial(bm, j)
        steps[f"merge{j}"] = functools.partial(merge, j)

    @step("unperm")
    def _():
        ytime_ref[...] = _dot(unperm_ref[...], y_ref[...]).astype(BF16)

    @step("out")
    def _():
        val["out"] = _dot(ytime_ref[...], wt_ref[:, OFF_WOUT:OFF_WOUT + D_MODEL])

    @step("post")
    def _():
        out = val.pop("out")
        ms_o = jnp.mean(out * out, axis=-1, keepdims=True)
        o_ref[...] = xprev_ref[...] + out * lax.rsqrt(ms_o + EPS) * p_ref[P_GPOST:P_GPOST + 1, :]

    @step("keep_x")
    def _():
        xprev_ref[...] = x_ref[...]

    done = set()
    for name in ORDER:
        assert all(d in done for d in STEP_DEPS[name]), (name, STEP_DEPS[name])
        steps[name]()
        done.add(name)
    assert done == set(steps), set(steps) - done


def _block_diag(w, group):
    h, d, e = w.shape
    across = jnp.tile(w.reshape(h // group, group, d, e), (1, 1, 1, group))
    on_diagonal = np.arange(group * e)[None, :] // e == np.arange(group)[:, None]
    return jnp.where(on_diagonal[None, :, None, :], across, 0.0).reshape(h // group, group * d, group * e)


def _row_order_constants():
    p = np.arange(TILE_T)
    q = p % CHUNK
    time_of = (p // CHUNK) * CHUNK + (q % SUBLANES) * SEG + q // SUBLANES
    perm = np.zeros((TILE_T, TILE_T), np.float32)
    perm[p, time_of] = 1.0
    t = time_of[:CHUNK]
    causal_p = (t[None, :] <= t[:, None]).astype(np.float32)
    return jnp.asarray(perm, BF16), jnp.asarray(causal_p, F32)


def _cast_kernel(src_ref, dst_ref):
    dst_ref[...] = src_ref[...].astype(dst_ref.dtype)


def _to_bf16(w):
    rows, cols = w.shape
    block = pl.BlockSpec((CAST_ROWS, cols), lambda i: (i, 0))
    return pl.pallas_call(
        _cast_kernel,
        out_shape=jax.ShapeDtypeStruct(w.shape, BF16),
        grid=(rows // CAST_ROWS,),
        in_specs=[block],
        out_specs=block,
        compiler_params=pltpu.CompilerParams(vmem_limit_bytes=VMEM_LIMIT_BYTES),
        name="to_bf16",
    )(w)


def _layer(x, g_pre, w_in, rg_conv_w, rg_conv_b, rg_w_a, rg_b_a, rg_w_x, rg_b_x, rg_lambda,
           ml_conv_w, ml_conv_b, ml_w_q, ml_w_k, ml_w_v, ml_b_i, ml_b_f, ml_norm_g, ml_skip,
           w_branch_r, w_branch_m, w_out, g_post):
    bsz, seq, d = x.shape
    assert d == D_MODEL and seq % TILE_T == 0

    n_if = 2 * H_M
    o_i = 5 * D_MODEL
    def odd_width(*ws):
        width = sum(w.shape[-1] for w in ws)
        tiles = -(-width // LANES)
        zeros = jnp.zeros((ws[0].shape[0], (tiles + 1 - tiles % 2) * LANES - width), ws[0].dtype)
        return jnp.concatenate(list(ws) + [zeros], axis=1).astype(BF16)

    w_main = _to_bf16(w_in)
    assert -(-w_main.shape[-1] // LANES) % 2 == 1
    w_gate = odd_width(w_in[:, o_i + n_if:])
    w_if = odd_width(w_in[:, o_i:o_i + n_if])
    w_tail = odd_width(w_branch_r, w_branch_m, w_out)

    group = COL_G // DH_R
    wax = jnp.concatenate([_block_diag(rg_w_a, group), _block_diag(rg_w_x, group)], axis=2).astype(BF16)
    wqk = jnp.concatenate([ml_w_q, ml_w_k * (1.0 / math.sqrt(DH_M))], axis=2).astype(BF16)

    b_if = jnp.concatenate([ml_b_i, ml_b_f, jnp.zeros((D_MODEL - n_if,), F32)])
    rows = [g_pre[None], rg_conv_w, rg_conv_b[None], rg_b_a[None], rg_b_x[None], rg_lambda[None],
            ml_conv_w, ml_conv_b[None], ml_norm_g[None], ml_skip[None], g_post[None], b_if[None]]
    n_rows = sum(r.shape[0] for r in rows)
    params = jnp.concatenate(rows + [jnp.zeros((P_ROWS - n_rows, D_MODEL), F32)], axis=0)

    perm, causal_p = _row_order_constants()

    nt = seq // TILE_T
    n_tiles = bsz * nt
    resident = pl.BlockSpec(memory_space=pltpu.VMEM)

    def tile_in(i):
        j = jnp.minimum(i, n_tiles - 1)
        return (j // nt, j % nt, 0)

    def tile_out(i):
        j = jnp.maximum(i - 1, 0)
        return (j // nt, j % nt, 0)

    x_tile = pl.BlockSpec((None, TILE_T, D_MODEL), tile_in)
    o_tile = pl.BlockSpec((None, TILE_T, D_MODEL), tile_out)
    slab_f32 = pltpu.VMEM((TILE_T, D_MODEL), F32)
    slab_bf16 = pltpu.VMEM((TILE_T, D_MODEL), BF16)
    conv_tail = pltpu.VMEM(((CONV_W - 1) * SUBLANES, D_MODEL), F32)
    scratch = [
        slab_bf16,
        conv_tail, conv_tail,
        slab_bf16,
        slab_f32,
        slab_bf16, slab_bf16, slab_bf16,
        slab_f32,
        slab_bf16,
        slab_bf16,
        slab_f32, slab_f32, slab_f32, slab_f32,
        pltpu.VMEM((SUBLANES, D_MODEL), F32),
        pltpu.VMEM((H_M, DH_M, DH_M), F32),
        pltpu.VMEM((SUBLANES, DH_M), F32),
        pltpu.VMEM((1, LANES), F32),
        slab_bf16,
        slab_f32,
    ]
    return pl.pallas_call(
        functools.partial(_block_kernel, tiles_per_seq=nt),
        out_shape=jax.ShapeDtypeStruct(x.shape, x.dtype),
        grid=(n_tiles + 1,),
        in_specs=[x_tile] + [resident] * 11,
        out_specs=o_tile,
        scratch_shapes=scratch,
        compiler_params=pltpu.CompilerParams(
            dimension_semantics=("arbitrary",),
            vmem_limit_bytes=VMEM_LIMIT_BYTES),
        name="hybrid_block",
    )(x, w_main, w_gate, w_if, wax, wqk, ml_w_v.astype(BF16),
      w_tail, params,
      perm, perm.T, causal_p)


def kernel(x, g_pre, w_in, rg_conv_w, rg_conv_b, rg_w_a, rg_b_a, rg_w_x, rg_b_x, rg_lambda,
           ml_conv_w, ml_conv_b, ml_w_q, ml_w_k, ml_w_v, ml_b_i, ml_b_f, ml_norm_g, ml_skip,
           w_branch_r, w_branch_m, w_out, g_post):
    h = x
    for l in range(g_pre.shape[0]):
        h = _layer(h, g_pre[l], w_in[l], rg_conv_w[l], rg_conv_b[l], rg_w_a[l], rg_b_a[l],
                   rg_w_x[l], rg_b_x[l], rg_lambda[l], ml_conv_w[l], ml_conv_b[l],
                   ml_w_q[l], ml_w_k[l], ml_w_v[l], ml_b_i[l], ml_b_f[l], ml_norm_g[l],
                   ml_skip[l], w_branch_r[l], w_branch_m[l], w_out[l], g_post[l])
    return h
```

```python
import functools
import math

import jax
import jax.numpy as jnp
import numpy as np
from jax import lax
from jax.experimental import pallas as pl
from jax.experimental.pallas import tpu as pltpu

D_MODEL = 1024
H_R = 16
DH_R = D_MODEL // H_R
RG_C = 8.0
CONV_W = 4
H_M = 4
DH_M = D_MODEL // H_M
EPS = 1e-6

TILE_T = 256
CHUNK = TILE_T
COL_G = 256
N_G = D_MODEL // COL_G
SUBLANES = 8
LANES = 128
SEG = CHUNK // SUBLANES
VMEM_LIMIT_BYTES = 56 * 1024 * 1024

PAIR = 2 * COL_G
OFF_RX, OFF_RZ, OFF_MX, OFF_MZ, OFF_MO = (j * D_MODEL for j in range(5))
OFF_GR, OFF_GM = 0, D_MODEL
OFF_WBR, OFF_WBM, OFF_WOUT = 0, D_MODEL, 2 * D_MODEL

(P_GPRE, P_RCW, P_RCB, P_RBA, P_RBX, P_LAM, P_MCW, P_MCB, P_NORMG, P_SKIP, P_GPOST, P_BIF) = (
    0, 1, 5, 6, 7, 8, 9, 13, 14, 15, 16, 17)
P_ROWS = 24

F32 = jnp.float32
BF16 = jnp.bfloat16


def _dot(a, b):
    return jnp.dot(a, b, preferred_element_type=F32)


def _silu(x):
    return x * jax.nn.sigmoid(x)


def _step_deps():
    d = {"pre": ()}
    for g in range(N_G):
        d[f"fill{g}"] = ("pre",)
        d[f"rg_dot{g}"] = ("pre",)
        d[f"rg_conv{g}"] = (f"rg_dot{g}",)
        d[f"rg_gdot{g}"] = (f"rg_conv{g}",)
        d[f"rg_gates{g}"] = (f"rg_gdot{g}",)
        d[f"rg_scan{g}"] = (f"rg_gates{g}",)
        d[f"rg_out{g}"] = (f"rg_scan{g}",)
        d[f"mm_dot{g}"] = ("pre",)
        d[f"mm_conv{g}"] = (f"mm_dot{g}",)
        d[f"mm_qkv{g}"] = (f"mm_conv{g}",)
        d[f"ck_k{g}"] = (f"mm_qkv{g}", "gv_vec")
        d[f"ck_a{g}"] = (f"ck_k{g}",)
        d[f"ck_b{g}"] = (f"ck_a{g}",)
        d[f"ck_c{g}"] = (f"ck_b{g}",)
        d[f"ck_d{g}"] = (f"ck_c{g}",)
        d[f"ck_f{g}"] = (f"ck_a{g}",)
        d[f"epi{g}"] = (f"ck_d{g}", f"mm_dot{g}", f"mm_conv{g}", f"fill{g}")
    for j in range(N_G // 2):
        d[f"fill{N_G + j}"] = ("pre",)
        d[f"br{j}"] = tuple(f"rg_out{g}" for g in range(N_G))
        d[f"bm{j}"] = tuple(f"epi{g}" for g in range(N_G))
        d[f"merge{j}"] = (f"br{j}", f"bm{j}", f"fill{2 * j}", f"fill{2 * j + 1}", f"fill{N_G + j}")
    d["gv_dot"] = ("pre",)
    d["gv_z"] = ("gv_dot",)
    d["gv_cum"] = ("gv_z",)
    d["gv_vec"] = ("gv_cum",)
    d["out"] = ()
    d["post"] = ("out",)
    d["keep_x"] = ("pre", "post")
    d["unperm"] = tuple(f"merge{j}" for j in range(N_G // 2)) + ("out",)
    return d


STEP_DEPS = _step_deps()


def _issue_order():
    assert (H_M, N_G) == (4, 4)
    head = ["out", "post", "pre", "gv_dot", "keep_x", "mm_dot0", "mm_conv0", "mm_dot2", "mm_dot1",
            "mm_qkv0", "gv_z", "mm_conv1", "mm_dot3", "mm_qkv1", "mm_conv2", "gv_cum", "rg_dot0",
            "mm_qkv2", "mm_conv3", "mm_qkv3", "gv_vec"]
    per_head = [
        ["ck_k0", "ck_a0", "rg_conv0", "ck_b0", "rg_gdot0", "ck_c0", "fill0", "ck_d0", "rg_gates0",
         "rg_dot1", "ck_f0", "rg_scan0", "rg_out0", "epi0"],
        ["ck_k1", "rg_dot2", "rg_conv1", "ck_a1", "ck_b1", "rg_gdot1", "ck_c1", "fill1", "ck_d1",
         "rg_gates1", "ck_f1", "rg_scan1", "rg_out1", "fill2", "epi1"],
        ["ck_k2", "ck_a2", "rg_conv2", "ck_b2", "rg_gdot2", "ck_c2", "fill3", "ck_d2", "rg_gates2",
         "ck_f2", "rg_dot3", "rg_scan2", "rg_out2", "epi2"],
        ["ck_k3", "ck_a3", "rg_conv3", "ck_b3", "fill4", "rg_gdot3", "ck_c3", "fill5", "ck_d3",
         "rg_gates3", "ck_f3", "rg_scan3", "rg_out3", "br0", "br1", "epi3"],
    ]
    tail = ["bm0", "bm1", "merge0", "merge1", "unperm"]
    return head + [name for steps_of_head in per_head for name in steps_of_head] + tail


ORDER = _issue_order()


def _linear_scan(a, b, h0):
    tn, c = a.shape
    a3 = a.reshape(tn // SUBLANES, SUBLANES, c)
    b3 = b.reshape(tn // SUBLANES, SUBLANES, c)
    sub = lax.broadcasted_iota(jnp.int32, (SUBLANES, c), 0)
    hs = []
    carry = h0
    for ck in range(tn // CHUNK):
        base = ck * SEG
        h = b3[base]
        p = a3[base]
        h_loc, p_loc = [h], [p]
        for r in range(1, SEG):
            h = a3[base + r] * h + b3[base + r]
            p = a3[base + r] * p
            h_loc.append(h)
            p_loc.append(p)
        s = 1
        while s < SUBLANES:
            keep = sub >= s
            h = jnp.where(keep, p * pltpu.roll(h, s, 0) + h, h)
            p = jnp.where(keep, p * pltpu.roll(p, s, 0), p)
            s *= 2
        end_state = p * carry + h
        seg_in = jnp.where(sub >= 1, pltpu.roll(end_state, 1, 0), carry)
        for r in range(SEG):
            hs.append(h_loc[r] + p_loc[r] * seg_in)
        carry = end_state[SUBLANES - 1:SUBLANES, :]
    return jnp.stack(hs, axis=0).reshape(tn, c), carry


def _causal_conv(tail_ref, cs, y, w_rows, bias):
    tn, c = y.shape
    n_tail = CONV_W - 1
    y3 = y.reshape(tn // SUBLANES, SUBLANES, c)
    first = lax.broadcasted_iota(jnp.int32, (n_tail, SUBLANES, c), 1) == 0
    prev_last = tail_ref[:, cs].reshape(n_tail, SUBLANES, c)
    outs = []
    for ck in range(tn // CHUNK):
        cur = y3[ck * SEG:(ck + 1) * SEG]
        wrapped = jnp.where(first, pltpu.roll(prev_last, 1, 1), pltpu.roll(cur[SEG - n_tail:], 1, 1))
        acc = bias + w_rows[CONV_W - 1] * cur
        for shift in range(1, CONV_W):
            shifted = jnp.concatenate([wrapped[n_tail - shift:], cur[:SEG - shift]], axis=0)
            acc = acc + w_rows[CONV_W - 1 - shift] * shifted
        outs.append(acc)
        prev_last = cur[SEG - n_tail:]
    tail_ref[:, cs] = prev_last.reshape(n_tail * SUBLANES, c)
    return jnp.concatenate(outs, axis=0).reshape(tn, c)


def _block_kernel(x_ref, w_ref, wg_ref, wif_ref, wax_ref, wqk_ref, wv_ref, wt_ref,
                  p_ref, perm_ref, unperm_ref, causal_ref, o_ref,
                  xn_ref, rxb_ref, mxb_ref, yr_ref, c_ref, q_ref, k_ref, v_ref, hm_ref,
                  ym_ref, y_ref, gr_ref, gm_ref, mo_ref, mz_ref,
                  hr_ref, cst_ref, nst_ref, mst_ref, ytime_ref, xprev_ref, *, tiles_per_seq):
    tn = x_ref.shape[0]

    @pl.when(pl.program_id(0) == 0)
    def _no_previous_tile():
        ytime_ref[...] = jnp.zeros_like(ytime_ref)
        xprev_ref[...] = jnp.zeros_like(xprev_ref)

    @pl.when(pl.program_id(0) % tiles_per_seq == 0)
    def _reset_state():
        rxb_ref[...] = jnp.zeros_like(rxb_ref)
        mxb_ref[...] = jnp.zeros_like(mxb_ref)
        hr_ref[...] = jnp.zeros_like(hr_ref)
        cst_ref[...] = jnp.zeros_like(cst_ref)
        nst_ref[...] = jnp.zeros_like(nst_ref)
        mst_ref[...] = jnp.zeros_like(mst_ref)

    def prow(r, cs):
        return p_ref[r:r + 1, cs]

    val = {}
    steps = {}

    def step(name):
        def register(fn):
            steps[name] = fn
            return fn
        return register

    @step("pre")
    def _():
        x = x_ref[...]
        ms = jnp.mean(x * x, axis=-1, keepdims=True)
        xn_time = (x * lax.rsqrt(ms + EPS) * p_ref[P_GPRE:P_GPRE + 1, :]).astype(BF16)
        xn_ref[...] = _dot(perm_ref[...], xn_time).astype(BF16)

    def fill_mz_gr(g):
        cs = slice(g * COL_G, (g + 1) * COL_G)
        xn = xn_ref[...]
        mz_ref[:, cs] = _dot(xn, w_ref[:, OFF_MZ + g * COL_G:OFF_MZ + (g + 1) * COL_G])
        gr_ref[:, cs] = _dot(xn, wg_ref[:, OFF_GR + g * COL_G:OFF_GR + (g + 1) * COL_G])

    def fill_gm(j):
        gm_ref[:, j * PAIR:(j + 1) * PAIR] = _dot(xn_ref[...], wg_ref[:, OFF_GM + j * PAIR:OFF_GM + (j + 1) * PAIR])

    for g in range(N_G):
        steps[f"fill{g}"] = functools.partial(fill_mz_gr, g)
    for j in range(N_G // 2):
        steps[f"fill{N_G + j}"] = functools.partial(fill_gm, j)

    def rg_dot(g):
        xn = xn_ref[...]
        val[f"rx{g}"] = _dot(xn, w_ref[:, OFF_RX + g * COL_G:OFF_RX + (g + 1) * COL_G])
        val[f"rz{g}"] = _dot(xn, w_ref[:, OFF_RZ + g * COL_G:OFF_RZ + (g + 1) * COL_G])

    def rg_conv(g):
        cs = slice(g * COL_G, (g + 1) * COL_G)
        u = _causal_conv(rxb_ref, cs, val.pop(f"rx{g}"), [prow(P_RCW + j, cs) for j in range(CONV_W)],
                         prow(P_RCB, cs))
        val[f"u{g}"] = u
        val[f"ub{g}"] = u.astype(BF16)

    def rg_gdot(g):
        val[f"gate{g}"] = _dot(val.pop(f"ub{g}"), wax_ref[g])

    def rg_gates(g):
        cs = slice(g * COL_G, (g + 1) * COL_G)
        u = val.pop(f"u{g}")
        gate = val.pop(f"gate{g}")
        r = jax.nn.sigmoid(gate[:, :COL_G] + prow(P_RBA, cs))
        i = jax.nn.sigmoid(gate[:, COL_G:] + prow(P_RBX, cs))
        log_a = (-RG_C * jax.nn.softplus(-prow(P_LAM, cs))) * r
        s_half = jnp.tanh(0.5 * log_a)
        inv = 1.0 / (1.0 - s_half)
        val[f"a{g}"] = (1.0 + s_half) * inv
        val[f"b{g}"] = (2.0 * jnp.sqrt(-s_half) * inv) * (i * u)

    def rg_scan(g):
        cs = slice(g * COL_G, (g + 1) * COL_G)
        h, h_last = _linear_scan(val.pop(f"a{g}"), val.pop(f"b{g}"), hr_ref[0:1, cs])
        hr_ref[0:1, cs] = h_last
        val[f"h{g}"] = h

    def rg_out(g):
        cs = slice(g * COL_G, (g + 1) * COL_G)
        yr_ref[:, cs] = (val.pop(f"h{g}") * _silu(val.pop(f"rz{g}"))).astype(BF16)

    def mm_dot(hd):
        cs = slice(hd * DH_M, (hd + 1) * DH_M)
        xn = xn_ref[...]
        val[f"mx{hd}"] = _dot(xn, w_ref[:, OFF_MX + hd * DH_M:OFF_MX + (hd + 1) * DH_M])
        mo_ref[:, cs] = _dot(xn, w_ref[:, OFF_MO + hd * DH_M:OFF_MO + (hd + 1) * DH_M])

    def mm_conv(hd):
        cs = slice(hd * DH_M, (hd + 1) * DH_M)
        mx = val.pop(f"mx{hd}")
        c = _silu(_causal_conv(mxb_ref, cs, mx, [prow(P_MCW + j, cs) for j in range(CONV_W)],
                               prow(P_MCB, cs)))
        c_ref[:, cs] = c
        val[f"cb{hd}"] = c.astype(BF16)
        val[f"mxb{hd}"] = mx.astype(BF16)

    def mm_qkv(hd):
        cs = slice(hd * DH_M, (hd + 1) * DH_M)
        qk = _dot(val.pop(f"cb{hd}"), wqk_ref[hd])
        q_ref[:, cs] = qk[:, :DH_M].astype(BF16)
        k_ref[:, cs] = qk[:, DH_M:].astype(BF16)
        v_ref[:, cs] = _dot(val.pop(f"mxb{hd}"), wv_ref[hd]).astype(BF16)

    for g in range(N_G):
        for nm, fn in (("rg_dot", rg_dot), ("rg_conv", rg_conv), ("rg_gdot", rg_gdot),
                       ("rg_gates", rg_gates), ("rg_scan", rg_scan), ("rg_out", rg_out),
                       ("mm_dot", mm_dot), ("mm_conv", mm_conv), ("mm_qkv", mm_qkv)):
            steps[f"{nm}{g}"] = functools.partial(fn, g)

    @step("gv_dot")
    def _():
        zt = lax.dot_general(wif_ref[...], xn_ref[...], (((1,), (1,)), ((), ())),
                             preferred_element_type=F32)
        slab = jnp.concatenate([zt, jnp.zeros((LANES - zt.shape[0], zt.shape[1]), F32)], axis=0)
        val["pre_if"] = slab.T

    @step("gv_z")
    def _():
        pre_if = val.pop("pre_if") + p_ref[P_BIF:P_BIF + 1, 0:LANES]
        lane = lax.broadcasted_iota(jnp.int32, pre_if.shape, 1)
        val["z"] = jnp.where(lane < H_M, pre_if, jax.nn.log_sigmoid(pre_if))

    @step("gv_cum")
    def _():
        sub = lax.broadcasted_iota(jnp.int32, (SUBLANES, LANES), 0)
        bc = val["z"].reshape(SEG, SUBLANES, LANES)
        s = 1
        while s < SEG:
            bc = jnp.concatenate([bc[:s], bc[s:] + bc[:-s]], axis=0)
            s *= 2
        seg_sum = bc[SEG - 1]
        s = 1
        while s < SUBLANES:
            seg_sum = jnp.where(sub >= s, seg_sum + pltpu.roll(seg_sum, s, 0), seg_sum)
            s *= 2
        before = jnp.where(sub >= 1, pltpu.roll(seg_sum, 1, 0), 0.0)
        val["bc"] = (bc + before).reshape(CHUNK, LANES)

    @step("gv_vec")
    def _():
        z = val.pop("z")
        bc = val.pop("bc")
        sub = lax.broadcasted_iota(jnp.int32, (SUBLANES, LANES), 0)
        m_row = mst_ref[...]
        r = pltpu.roll(z, H_M, 1) - bc
        cm = r.reshape(SEG, SUBLANES, LANES)
        s = 1
        while s < SEG:
            cm = jnp.concatenate([cm[:s], jnp.maximum(cm[s:], cm[:-s])], axis=0)
            s *= 2
        seg_max = cm[SEG - 1]
        s = 1
        while s < SUBLANES:
            seg_max = jnp.where(sub >= s, jnp.maximum(seg_max, pltpu.roll(seg_max, s, 0)), seg_max)
            s *= 2
        before = jnp.where(sub >= 1, pltpu.roll(seg_max, 1, 0), -jnp.inf)
        cm = jnp.maximum(cm, before).reshape(CHUNK, LANES)
        mj = jnp.maximum(m_row, cm)
        mj_last = mj[CHUNK - 1:CHUNK, :]
        val["gt"] = dict(
            rt=r.T,
            mj=mj,
            e_inter=jnp.exp(m_row - mj),
            e_negm=jnp.exp(-(bc + mj)),
            wg=jnp.exp(r - mj_last),
            decay=jnp.exp(m_row - mj_last))
        mst_ref[...] = bc[CHUNK - 1:CHUNK, :] + mj_last

    def ck_k(hd):
        cs = slice(hd * DH_M, (hd + 1) * DH_M)
        ln = slice(H_M + hd, H_M + hd + 1)
        gt = val["gt"]
        kw = k_ref[:, cs].astype(F32) * gt["wg"][:, ln]
        n_old = nst_ref[hd:hd + 1, :]
        val[f"n{hd}"] = n_old
        nst_ref[hd:hd + 1, :] = gt["decay"][:, ln] * n_old + jnp.sum(kw, axis=0, keepdims=True)
        val[f"kw{hd}"] = kw.astype(BF16)

    def ck_a(hd):
        cs = slice(hd * DH_M, (hd + 1) * DH_M)
        q = q_ref[:, cs]
        val[f"qk{hd}"] = lax.dot_general(q, k_ref[:, cs], (((1,), (1,)), ((), ())),
                                        preferred_element_type=F32)
        val[f"qc{hd}"] = _dot(q, cst_ref[hd].astype(BF16))
        val[f"kv{hd}"] = lax.dot_general(val.pop(f"kw{hd}"), v_ref[:, cs], (((0,), (0,)), ((), ())),
                                        preferred_element_type=F32)

    def ck_b(hd):
        ln = slice(H_M + hd, H_M + hd + 1)
        gt = val["gt"]
        causal = causal_ref[...] > 0.5
        w = jnp.where(causal, jnp.exp(gt["rt"][ln, :] - gt["mj"][:, ln]), 0.0)
        s_qk = val.pop(f"qk{hd}") * w
        val[f"rowsum{hd}"] = jnp.sum(s_qk, axis=-1, keepdims=True)
        val[f"s{hd}"] = s_qk.astype(BF16)

    def ck_c(hd):
        val[f"sv{hd}"] = _dot(val.pop(f"s{hd}"), v_ref[:, hd * DH_M:(hd + 1) * DH_M])

    def ck_d(hd):
        cs = slice(hd * DH_M, (hd + 1) * DH_M)
        ln = slice(H_M + hd, H_M + hd + 1)
        gt = val["gt"]
        e_inter = gt["e_inter"][:, ln]
        num = val.pop(f"sv{hd}") + e_inter * val.pop(f"qc{hd}")
        den = val.pop(f"rowsum{hd}") + \
            e_inter * jnp.sum(q_ref[:, cs].astype(F32) * val.pop(f"n{hd}"), axis=-1, keepdims=True)
        hm_ref[:, cs] = num * (1.0 / jnp.maximum(jnp.abs(den), gt["e_negm"][:, ln]))

    def ck_f(hd):
        ln = slice(H_M + hd, H_M + hd + 1)
        cst_ref[hd] = val["gt"]["decay"][:, ln] * cst_ref[hd] + val.pop(f"kv{hd}")

    for hd in range(H_M):
        for nm, fn in (("ck_k", ck_k), ("ck_a", ck_a), ("ck_b", ck_b), ("ck_c", ck_c), ("ck_d", ck_d),
                       ("ck_f", ck_f)):
            steps[f"{nm}{hd}"] = functools.partial(fn, hd)

    def epilogue(hd):
        cs = slice(hd * DH_M, (hd + 1) * DH_M)
        hg = jax.nn.sigmoid(mo_ref[:, cs]) * hm_ref[:, cs]
        mu = jnp.mean(hg, axis=-1, keepdims=True)
        dlt = hg - mu
        var = jnp.mean(dlt * dlt, axis=-1, keepdims=True)
        hn = dlt * lax.rsqrt(var + EPS) * prow(P_NORMG, cs) + prow(P_SKIP, cs) * c_ref[:, cs]
        ym_ref[:, cs] = (hn * _silu(mz_ref[:, cs])).astype(BF16)

    for hd in range(H_M):
        steps[f"epi{hd}"] = functools.partial(epilogue, hd)

    def br(j):
        val[f"br{j}"] = _dot(yr_ref[...], wt_ref[:, OFF_WBR + j * PAIR:OFF_WBR + (j + 1) * PAIR])

    def bm(j):
        val[f"bm{j}"] = _dot(ym_ref[...], wt_ref[:, OFF_WBM + j * PAIR:OFF_WBM + (j + 1) * PAIR])

    def merge(j):
        cs = slice(j * PAIR, (j + 1) * PAIR)
        y = jax.nn.sigmoid(gr_ref[:, cs]) * val.pop(f"br{j}") + \
            jax.nn.sigmoid(gm_ref[:, cs]) * val.pop(f"bm{j}")
        y_ref[:, cs] = y.astype(BF16)

    for j in range(N_G // 2):
        steps[f"br{j}"] = functools.partial(br, j)
        steps[f"bm{j}"] = functools.partial(bm, j)
        steps[f"merge{j}"] = functools.partial(merge, j)

    @step("unperm")
    def _():
        ytime_ref[...] = _dot(unperm_ref[...], y_ref[...]).astype(BF16)

    @step("out")
    def _():
        val["out"] = _dot(ytime_ref[...], wt_ref[:, OFF_WOUT:OFF_WOUT + D_MODEL])

    @step("post")
    def _():
        out = val.pop("out")
        ms_o = jnp.mean(out * out, axis=-1, keepdims=True)
        o_ref[...] = xprev_ref[...] + out * lax.rsqrt(ms_o + EPS) * p_ref[P_GPOST:P_GPOST + 1, :]

    @step("keep_x")
    def _():
        xprev_ref[...] = x_ref[...]

    done = set()
    for name in ORDER:
        assert all(d in done for d in STEP_DEPS[name]), (name, STEP_DEPS[name])
        steps[name]()
        done.add(name)
    assert done == set(steps), set(steps) - done


def _block_diag(w, group):
    h, d, e = w.shape
    across = jnp.tile(w.reshape(h // group, group, d, e), (1, 1, 1, group))
    on_diagonal = np.arange(group * e)[None, :] // e == np.arange(group)[:, None]
    return jnp.where(on_diagonal[None, :, None, :], across, 0.0).reshape(h // group, group * d, group * e)


def _row_order_constants():
    p = np.arange(TILE_T)
    q = p % CHUNK
    time_of = (p // CHUNK) * CHUNK + (q % SUBLANES) * SEG + q // SUBLANES
    perm = np.zeros((TILE_T, TILE_T), np.float32)
    perm[p, time_of] = 1.0
    t = time_of[:CHUNK]
    causal_p = (t[None, :] <= t[:, None]).astype(np.float32)
    return jnp.asarray(perm, BF16), jnp.asarray(causal_p, F32)


def _layer(x, g_pre, w_in, rg_conv_w, rg_conv_b, rg_w_a, rg_b_a, rg_w_x, rg_b_x, rg_lambda,
           ml_conv_w, ml_conv_b, ml_w_q, ml_w_k, ml_w_v, ml_b_i, ml_b_f, ml_norm_g, ml_skip,
           w_branch_r, w_branch_m, w_out, g_post):
    bsz, seq, d = x.shape
    assert d == D_MODEL and seq % TILE_T == 0

    n_if = 2 * H_M
    o_i = 5 * D_MODEL
    def odd_width(*ws):
        width = sum(w.shape[-1] for w in ws)
        tiles = -(-width // LANES)
        zeros = jnp.zeros((ws[0].shape[0], (tiles + 1 - tiles % 2) * LANES - width), ws[0].dtype)
        return jnp.concatenate(list(ws) + [zeros], axis=1).astype(BF16)

    w_main = w_in.astype(BF16)
    assert -(-w_main.shape[-1] // LANES) % 2 == 1
    w_gate = odd_width(w_in[:, o_i + n_if:])
    w_if = jnp.concatenate([w_in[:, o_i:o_i + n_if].T, jnp.zeros((n_if, D_MODEL), F32)], axis=0).astype(BF16)
    w_tail = odd_width(w_branch_r, w_branch_m, w_out)

    group = COL_G // DH_R
    wax = jnp.concatenate([_block_diag(rg_w_a, group), _block_diag(rg_w_x, group)], axis=2).astype(BF16)
    wqk = jnp.concatenate([ml_w_q, ml_w_k * (1.0 / math.sqrt(DH_M))], axis=2).astype(BF16)

    b_if = jnp.concatenate([ml_b_i, ml_b_f, jnp.zeros((D_MODEL - n_if,), F32)])
    rows = [g_pre[None], rg_conv_w, rg_conv_b[None], rg_b_a[None], rg_b_x[None], rg_lambda[None],
            ml_conv_w, ml_conv_b[None], ml_norm_g[None], ml_skip[None], g_post[None], b_if[None]]
    n_rows = sum(r.shape[0] for r in rows)
    params = jnp.concatenate(rows + [jnp.zeros((P_ROWS - n_rows, D_MODEL), F32)], axis=0)

    perm, causal_p = _row_order_constants()

    nt = seq // TILE_T
    n_tiles = bsz * nt
    resident = pl.BlockSpec(memory_space=pltpu.VMEM)

    def tile_in(i):
        j = jnp.minimum(i, n_tiles - 1)
        return (j // nt, j % nt, 0)

    def tile_out(i):
        j = jnp.maximum(i - 1, 0)
        return (j // nt, j % nt, 0)

    x_tile = pl.BlockSpec((None, TILE_T, D_MODEL), tile_in)
    o_tile = pl.BlockSpec((None, TILE_T, D_MODEL), tile_out)
    slab_f32 = pltpu.VMEM((TILE_T, D_MODEL), F32)
    slab_bf16 = pltpu.VMEM((TILE_T, D_MODEL), BF16)
    conv_tail = pltpu.VMEM(((CONV_W - 1) * SUBLANES, D_MODEL), F32)
    scratch = [
        slab_bf16,
        conv_tail, conv_tail,
        slab_bf16,
        slab_f32,
        slab_bf16, slab_bf16, slab_bf16,
        slab_f32,
        slab_bf16,
        slab_bf16,
        slab_f32, slab_f32, slab_f32, slab_f32,
        pltpu.VMEM((SUBLANES, D_MODEL), F32),
        pltpu.VMEM((H_M, DH_M, DH_M), F32),
        pltpu.VMEM((SUBLANES, DH_M), F32),
        pltpu.VMEM((1, LANES), F32),
        slab_bf16,
        slab_f32,
    ]
    return pl.pallas_call(
        functools.partial(_block_kernel, tiles_per_seq=nt),
        out_shape=jax.ShapeDtypeStruct(x.shape, x.dtype),
        grid=(n_tiles + 1,),
        in_specs=[x_tile] + [resident] * 11,
        out_specs=o_tile,
        scratch_shapes=scratch,
        compiler_params=pltpu.CompilerParams(
            dimension_semantics=("arbitrary",),
            vmem_limit_bytes=VMEM_LIMIT_BYTES),
        name="hybrid_block",
    )(x, w_main, w_gate, w_if, wax, wqk, ml_w_v.astype(BF16),
      w_tail, params,
      perm, perm.T, causal_p)


def kernel(x, g_pre, w_in, rg_conv_w, rg_conv_b, rg_w_a, rg_b_a, rg_w_x, rg_b_x, rg_lambda,
           ml_conv_w, ml_conv_b, ml_w_q, ml_w_k, ml_w_v, ml_b_i, ml_b_f, ml_norm_g, ml_skip,
           w_branch_r, w_branch_m, w_out, g_post):
    h = x
    for l in range(g_pre.shape[0]):
        h = _layer(h, g_pre[l], w_in[l], rg_conv_w[l], rg_conv_b[l], rg_w_a[l], rg_b_a[l],
                   rg_w_x[l], rg_b_x[l], rg_lambda[l], ml_conv_w[l], ml_conv_b[l],
                   ml_w_q[l], ml_w_k[l], ml_w_v[l], ml_b_i[l], ml_b_f[l], ml_norm_g[l],
                   ml_skip[l], w_branch_r[l], w_branch_m[l], w_out[l], g_post[l])
    return h
```

```python
import functools
import math

import jax
import jax.numpy as jnp
import numpy as np
from jax import lax
from jax.experimental import pallas as pl
from jax.experimental.pallas import tpu as pltpu

D_MODEL = 1024
H_R = 16
DH_R = D_MODEL // H_R
RG_C = 8.0
CONV_W = 4
H_M = 4
DH_M = D_MODEL // H_M
EPS = 1e-6

TILE_T = 256
CHUNK = TILE_T
COL_G = 256
N_G = D_MODEL // COL_G
SUBLANES = 8
LANES = 128
SEG = CHUNK // SUBLANES
VMEM_LIMIT_BYTES = 56 * 1024 * 1024

PAIR = 2 * COL_G
OFF_RX, OFF_RZ, OFF_MX, OFF_MZ, OFF_MO = (j * D_MODEL for j in range(5))
OFF_GR, OFF_GM = 0, D_MODEL
OFF_WBR, OFF_WBM, OFF_WOUT = 0, D_MODEL, 2 * D_MODEL

(P_GPRE, P_RCW, P_RCB, P_RBA, P_RBX, P_LAM, P_MCW, P_MCB, P_NORMG, P_SKIP, P_GPOST, P_BIF) = (
    0, 1, 5, 6, 7, 8, 9, 13, 14, 15, 16, 17)
P_ROWS = 24

F32 = jnp.float32
BF16 = jnp.bfloat16


def _dot(a, b):
    return jnp.dot(a, b, preferred_element_type=F32)


def _silu(x):
    return x * jax.nn.sigmoid(x)


def _step_deps():
    d = {"pre": ()}
    for g in range(N_G):
        d[f"fill{g}"] = ("pre",)
        d[f"rg_dot{g}"] = ("pre",)
        d[f"rg_conv{g}"] = (f"rg_dot{g}",)
        d[f"rg_gdot{g}"] = (f"rg_conv{g}",)
        d[f"rg_gates{g}"] = (f"rg_gdot{g}",)
        d[f"rg_scan{g}"] = (f"rg_gates{g}",)
        d[f"rg_out{g}"] = (f"rg_scan{g}",)
        d[f"mm_dot{g}"] = ("pre",)
        d[f"mm_conv{g}"] = (f"mm_dot{g}",)
        d[f"mm_qkv{g}"] = (f"mm_conv{g}",)
        d[f"ck_k{g}"] = (f"mm_qkv{g}", "gv_vec")
        d[f"ck_a{g}"] = (f"ck_k{g}",)
        d[f"ck_b{g}"] = (f"ck_a{g}",)
        d[f"ck_c{g}"] = (f"ck_b{g}",)
        d[f"ck_d{g}"] = (f"ck_c{g}",)
        d[f"ck_f{g}"] = (f"ck_a{g}",)
        d[f"epi{g}"] = (f"ck_d{g}", f"mm_dot{g}", f"mm_conv{g}", f"fill{g}")
    for j in range(N_G // 2):
        d[f"fill{N_G + j}"] = ("pre",)
        d[f"br{j}"] = tuple(f"rg_out{g}" for g in range(N_G))
        d[f"bm{j}"] = tuple(f"epi{g}" for g in range(N_G))
        d[f"merge{j}"] = (f"br{j}", f"bm{j}", f"fill{2 * j}", f"fill{2 * j + 1}", f"fill{N_G + j}")
    d["gv_dot"] = ("pre",)
    d["gv_z"] = ("gv_dot",)
    d["gv_cum"] = ("gv_z",)
    d["gv_vec"] = ("gv_cum",)
    d["out"] = ()
    d["post"] = ("out",)
    d["unperm"] = tuple(f"merge{j}" for j in range(N_G // 2)) + ("out",)
    return d


STEP_DEPS = _step_deps()


def _issue_order():
    assert (H_M, N_G) == (4, 4)
    head = ["out", "post", "pre", "mm_dot0", "gv_dot", "mm_conv0", "mm_dot2", "mm_dot1",
            "mm_qkv0", "gv_z", "mm_conv1", "mm_dot3", "mm_qkv1", "mm_conv2", "gv_cum", "rg_dot0",
            "mm_qkv2", "mm_conv3", "mm_qkv3", "rg_dot1", "gv_vec"]
    per_head = [
        ["ck_k0", "ck_a0", "rg_conv0", "ck_b0", "rg_gdot0", "ck_c0", "fill0", "ck_d0", "rg_gates0",
         "ck_f0", "rg_scan0", "rg_out0", "epi0"],
        ["ck_k1", "rg_dot2", "rg_conv1", "ck_a1", "ck_b1", "rg_gdot1", "ck_c1", "fill1", "ck_d1",
         "rg_gates1", "ck_f1", "rg_scan1", "rg_out1", "fill2", "epi1"],
        ["ck_k2", "ck_a2", "rg_conv2", "ck_b2", "rg_gdot2", "ck_c2", "fill3", "ck_d2", "rg_gates2",
         "ck_f2", "rg_dot3", "rg_scan2", "rg_out2", "epi2"],
        ["ck_k3", "ck_a3", "rg_conv3", "ck_b3", "fill4", "rg_gdot3", "ck_c3", "fill5", "ck_d3",
         "rg_gates3", "ck_f3", "rg_scan3", "rg_out3", "br1", "br0", "epi3"],
    ]
    tail = ["bm0", "bm1", "merge0", "merge1", "unperm"]
    return head + [name for steps_of_head in per_head for name in steps_of_head] + tail


ORDER = _issue_order()


def _linear_scan(a, b, h0):
    tn, c = a.shape
    a3 = a.reshape(tn // SUBLANES, SUBLANES, c)
    b3 = b.reshape(tn // SUBLANES, SUBLANES, c)
    sub = lax.broadcasted_iota(jnp.int32, (SUBLANES, c), 0)
    hs = []
    carry = h0
    for ck in range(tn // CHUNK):
        base = ck * SEG
        h = b3[base]
        p = a3[base]
        h_loc, p_loc = [h], [p]
        for r in range(1, SEG):
            h = a3[base + r] * h + b3[base + r]
            p = a3[base + r] * p
            h_loc.append(h)
            p_loc.append(p)
        s = 1
        while s < SUBLANES:
            keep = sub >= s
            h = jnp.where(keep, p * pltpu.roll(h, s, 0) + h, h)
            p = jnp.where(keep, p * pltpu.roll(p, s, 0), p)
            s *= 2
        end_state = p * carry + h
        seg_in = jnp.where(sub >= 1, pltpu.roll(end_state, 1, 0), carry)
        for r in range(SEG):
            hs.append(h_loc[r] + p_loc[r] * seg_in)
        carry = end_state[SUBLANES - 1:SUBLANES, :]
    return jnp.stack(hs, axis=0).reshape(tn, c), carry


def _causal_conv(tail_ref, cs, y, w_rows, bias):
    tn, c = y.shape
    n_tail = CONV_W - 1
    y3 = y.reshape(tn // SUBLANES, SUBLANES, c)
    first = lax.broadcasted_iota(jnp.int32, (n_tail, SUBLANES, c), 1) == 0
    prev_last = tail_ref[:, cs].reshape(n_tail, SUBLANES, c)
    outs = []
    for ck in range(tn // CHUNK):
        cur = y3[ck * SEG:(ck + 1) * SEG]
        wrapped = jnp.where(first, pltpu.roll(prev_last, 1, 1), pltpu.roll(cur[SEG - n_tail:], 1, 1))
        acc = bias + w_rows[CONV_W - 1] * cur
        for shift in range(1, CONV_W):
            shifted = jnp.concatenate([wrapped[n_tail - shift:], cur[:SEG - shift]], axis=0)
            acc = acc + w_rows[CONV_W - 1 - shift] * shifted
        outs.append(acc)
        prev_last = cur[SEG - n_tail:]
    tail_ref[:, cs] = prev_last.reshape(n_tail * SUBLANES, c)
    return jnp.concatenate(outs, axis=0).reshape(tn, c)


def _block_kernel(x_ref, xprev_ref, w_ref, wg_ref, wif_ref, wax_ref, wqk_ref, wv_ref, wt_ref,
                  p_ref, perm_ref, unperm_ref, causal_ref, o_ref,
                  xn_ref, rxb_ref, mxb_ref, yr_ref, c_ref, q_ref, k_ref, v_ref, hm_ref,
                  ym_ref, y_ref, gr_ref, gm_ref, mo_ref, mz_ref,
                  hr_ref, cst_ref, nst_ref, mst_ref, ytime_ref, *, tiles_per_seq):
    tn = x_ref.shape[0]

    @pl.when(pl.program_id(0) == 0)
    def _no_previous_tile():
        ytime_ref[...] = jnp.zeros_like(ytime_ref)

    @pl.when(pl.program_id(0) % tiles_per_seq == 0)
    def _reset_state():
        rxb_ref[...] = jnp.zeros_like(rxb_ref)
        mxb_ref[...] = jnp.zeros_like(mxb_ref)
        hr_ref[...] = jnp.zeros_like(hr_ref)
        cst_ref[...] = jnp.zeros_like(cst_ref)
        nst_ref[...] = jnp.zeros_like(nst_ref)
        mst_ref[...] = jnp.zeros_like(mst_ref)

    def prow(r, cs):
        return p_ref[r:r + 1, cs]

    val = {}
    steps = {}

    def step(name):
        def register(fn):
            steps[name] = fn
            return fn
        return register

    @step("pre")
    def _():
        x = x_ref[...]
        ms = jnp.mean(x * x, axis=-1, keepdims=True)
        xn_time = (x * lax.rsqrt(ms + EPS) * p_ref[P_GPRE:P_GPRE + 1, :]).astype(BF16)
        xn_ref[...] = _dot(perm_ref[...], xn_time).astype(BF16)

    def fill_mz_gr(g):
        cs = slice(g * COL_G, (g + 1) * COL_G)
        xn = xn_ref[...]
        mz_ref[:, cs] = _dot(xn, w_ref[:, OFF_MZ + g * COL_G:OFF_MZ + (g + 1) * COL_G])
        gr_ref[:, cs] = _dot(xn, wg_ref[:, OFF_GR + g * COL_G:OFF_GR + (g + 1) * COL_G])

    def fill_gm(j):
        gm_ref[:, j * PAIR:(j + 1) * PAIR] = _dot(xn_ref[...], wg_ref[:, OFF_GM + j * PAIR:OFF_GM + (j + 1) * PAIR])

    for g in range(N_G):
        steps[f"fill{g}"] = functools.partial(fill_mz_gr, g)
    for j in range(N_G // 2):
        steps[f"fill{N_G + j}"] = functools.partial(fill_gm, j)

    def rg_dot(g):
        xn = xn_ref[...]
        val[f"rx{g}"] = _dot(xn, w_ref[:, OFF_RX + g * COL_G:OFF_RX + (g + 1) * COL_G])
        val[f"rz{g}"] = _dot(xn, w_ref[:, OFF_RZ + g * COL_G:OFF_RZ + (g + 1) * COL_G])

    def rg_conv(g):
        cs = slice(g * COL_G, (g + 1) * COL_G)
        u = _causal_conv(rxb_ref, cs, val.pop(f"rx{g}"), [prow(P_RCW + j, cs) for j in range(CONV_W)],
                         prow(P_RCB, cs))
        val[f"u{g}"] = u
        val[f"ub{g}"] = u.astype(BF16)

    def rg_gdot(g):
        val[f"gate{g}"] = _dot(val.pop(f"ub{g}"), wax_ref[g])

    def rg_gates(g):
        cs = slice(g * COL_G, (g + 1) * COL_G)
        u = val.pop(f"u{g}")
        gate = val.pop(f"gate{g}")
        r = jax.nn.sigmoid(gate[:, :COL_G] + prow(P_RBA, cs))
        i = jax.nn.sigmoid(gate[:, COL_G:] + prow(P_RBX, cs))
        log_a = (-RG_C * jax.nn.softplus(-prow(P_LAM, cs))) * r
        s_half = jnp.tanh(0.5 * log_a)
        inv = 1.0 / (1.0 - s_half)
        val[f"a{g}"] = (1.0 + s_half) * inv
        val[f"b{g}"] = (2.0 * jnp.sqrt(-s_half) * inv) * (i * u)

    def rg_scan(g):
        cs = slice(g * COL_G, (g + 1) * COL_G)
        h, h_last = _linear_scan(val.pop(f"a{g}"), val.pop(f"b{g}"), hr_ref[0:1, cs])
        hr_ref[0:1, cs] = h_last
        val[f"h{g}"] = h

    def rg_out(g):
        cs = slice(g * COL_G, (g + 1) * COL_G)
        yr_ref[:, cs] = (val.pop(f"h{g}") * _silu(val.pop(f"rz{g}"))).astype(BF16)

    def mm_dot(hd):
        cs = slice(hd * DH_M, (hd + 1) * DH_M)
        xn = xn_ref[...]
        val[f"mx{hd}"] = _dot(xn, w_ref[:, OFF_MX + hd * DH_M:OFF_MX + (hd + 1) * DH_M])
        mo_ref[:, cs] = _dot(xn, w_ref[:, OFF_MO + hd * DH_M:OFF_MO + (hd + 1) * DH_M])

    def mm_conv(hd):
        cs = slice(hd * DH_M, (hd + 1) * DH_M)
        mx = val.pop(f"mx{hd}")
        c = _silu(_causal_conv(mxb_ref, cs, mx, [prow(P_MCW + j, cs) for j in range(CONV_W)],
                               prow(P_MCB, cs)))
        c_ref[:, cs] = c
        val[f"cb{hd}"] = c.astype(BF16)
        val[f"mxb{hd}"] = mx.astype(BF16)

    def mm_qkv(hd):
        cs = slice(hd * DH_M, (hd + 1) * DH_M)
        qk = _dot(val.pop(f"cb{hd}"), wqk_ref[hd])
        q_ref[:, cs] = qk[:, :DH_M].astype(BF16)
        k_ref[:, cs] = qk[:, DH_M:].astype(BF16)
        v_ref[:, cs] = _dot(val.pop(f"mxb{hd}"), wv_ref[hd]).astype(BF16)

    for g in range(N_G):
        for nm, fn in (("rg_dot", rg_dot), ("rg_conv", rg_conv), ("rg_gdot", rg_gdot),
                       ("rg_gates", rg_gates), ("rg_scan", rg_scan), ("rg_out", rg_out),
                       ("mm_dot", mm_dot), ("mm_conv", mm_conv), ("mm_qkv", mm_qkv)):
            steps[f"{nm}{g}"] = functools.partial(fn, g)

    @step("gv_dot")
    def _():
        val["pre_if"] = _dot(xn_ref[...], wif_ref[...])

    @step("gv_z")
    def _():
        pre_if = val.pop("pre_if") + p_ref[P_BIF:P_BIF + 1, 0:LANES]
        lane = lax.broadcasted_iota(jnp.int32, pre_if.shape, 1)
        val["z"] = jnp.where(lane < H_M, pre_if, jax.nn.log_sigmoid(pre_if))

    @step("gv_cum")
    def _():
        sub = lax.broadcasted_iota(jnp.int32, (SUBLANES, LANES), 0)
        bc = val["z"].reshape(SEG, SUBLANES, LANES)
        s = 1
        while s < SEG:
            bc = jnp.concatenate([bc[:s], bc[s:] + bc[:-s]], axis=0)
            s *= 2
        seg_sum = bc[SEG - 1]
        s = 1
        while s < SUBLANES:
            seg_sum = jnp.where(sub >= s, seg_sum + pltpu.roll(seg_sum, s, 0), seg_sum)
            s *= 2
        before = jnp.where(sub >= 1, pltpu.roll(seg_sum, 1, 0), 0.0)
        val["bc"] = (bc + before).reshape(CHUNK, LANES)

    @step("gv_vec")
    def _():
        z = val.pop("z")
        bc = val.pop("bc")
        sub = lax.broadcasted_iota(jnp.int32, (SUBLANES, LANES), 0)
        m_row = mst_ref[...]
        r = pltpu.roll(z, H_M, 1) - bc
        cm = r.reshape(SEG, SUBLANES, LANES)
        s = 1
        while s < SEG:
            cm = jnp.concatenate([cm[:s], jnp.maximum(cm[s:], cm[:-s])], axis=0)
            s *= 2
        seg_max = cm[SEG - 1]
        s = 1
        while s < SUBLANES:
            seg_max = jnp.where(sub >= s, jnp.maximum(seg_max, pltpu.roll(seg_max, s, 0)), seg_max)
            s *= 2
        before = jnp.where(sub >= 1, pltpu.roll(seg_max, 1, 0), -jnp.inf)
        cm = jnp.maximum(cm, before).reshape(CHUNK, LANES)
        mj = jnp.maximum(m_row, cm)
        mj_last = mj[CHUNK - 1:CHUNK, :]
        val["gt"] = dict(
            rt=r.T,
            mj=mj,
            e_inter=jnp.exp(m_row - mj),
            e_negm=jnp.exp(-(bc + mj)),
            wg=jnp.exp(r - mj_last),
            decay=jnp.exp(m_row - mj_last))
        mst_ref[...] = bc[CHUNK - 1:CHUNK, :] + mj_last

    def ck_k(hd):
        cs = slice(hd * DH_M, (hd + 1) * DH_M)
        ln = slice(H_M + hd, H_M + hd + 1)
        gt = val["gt"]
        kw = k_ref[:, cs].astype(F32) * gt["wg"][:, ln]
        n_old = nst_ref[hd:hd + 1, :]
        val[f"n{hd}"] = n_old
        nst_ref[hd:hd + 1, :] = gt["decay"][:, ln] * n_old + jnp.sum(kw, axis=0, keepdims=True)
        val[f"kw{hd}"] = kw.astype(BF16)

    def ck_a(hd):
        cs = slice(hd * DH_M, (hd + 1) * DH_M)
        q = q_ref[:, cs]
        val[f"qk{hd}"] = lax.dot_general(q, k_ref[:, cs], (((1,), (1,)), ((), ())),
                                        preferred_element_type=F32)
        val[f"qc{hd}"] = _dot(q, cst_ref[hd].astype(BF16))
        val[f"kv{hd}"] = lax.dot_general(val.pop(f"kw{hd}"), v_ref[:, cs], (((0,), (0,)), ((), ())),
                                        preferred_element_type=F32)

    def ck_b(hd):
        ln = slice(H_M + hd, H_M + hd + 1)
        gt = val["gt"]
        causal = causal_ref[...] > 0.5
        w = jnp.where(causal, jnp.exp(gt["rt"][ln, :] - gt["mj"][:, ln]), 0.0)
        s_qk = val.pop(f"qk{hd}") * w
        val[f"rowsum{hd}"] = jnp.sum(s_qk, axis=-1, keepdims=True)
        val[f"s{hd}"] = s_qk.astype(BF16)

    def ck_c(hd):
        val[f"sv{hd}"] = _dot(val.pop(f"s{hd}"), v_ref[:, hd * DH_M:(hd + 1) * DH_M])

    def ck_d(hd):
        cs = slice(hd * DH_M, (hd + 1) * DH_M)
        ln = slice(H_M + hd, H_M + hd + 1)
        gt = val["gt"]
        e_inter = gt["e_inter"][:, ln]
        num = val.pop(f"sv{hd}") + e_inter * val.pop(f"qc{hd}")
        den = val.pop(f"rowsum{hd}") + \
            e_inter * jnp.sum(q_ref[:, cs].astype(F32) * val.pop(f"n{hd}"), axis=-1, keepdims=True)
        hm_ref[:, cs] = num * (1.0 / jnp.maximum(jnp.abs(den), gt["e_negm"][:, ln]))

    def ck_f(hd):
        ln = slice(H_M + hd, H_M + hd + 1)
        cst_ref[hd] = val["gt"]["decay"][:, ln] * cst_ref[hd] + val.pop(f"kv{hd}")

    for hd in range(H_M):
        for nm, fn in (("ck_k", ck_k), ("ck_a", ck_a), ("ck_b", ck_b), ("ck_c", ck_c), ("ck_d", ck_d),
                       ("ck_f", ck_f)):
            steps[f"{nm}{hd}"] = functools.partial(fn, hd)

    def epilogue(hd):
        cs = slice(hd * DH_M, (hd + 1) * DH_M)
        hg = jax.nn.sigmoid(mo_ref[:, cs]) * hm_ref[:, cs]
        mu = jnp.mean(hg, axis=-1, keepdims=True)
        dlt = hg - mu
        var = jnp.mean(dlt * dlt, axis=-1, keepdims=True)
        hn = dlt * lax.rsqrt(var + EPS) * prow(P_NORMG, cs) + prow(P_SKIP, cs) * c_ref[:, cs]
        ym_ref[:, cs] = (hn * _silu(mz_ref[:, cs])).astype(BF16)

    for hd in range(H_M):
        steps[f"epi{hd}"] = functools.partial(epilogue, hd)

    def br(j):
        val[f"br{j}"] = _dot(yr_ref[...], wt_ref[:, OFF_WBR + j * PAIR:OFF_WBR + (j + 1) * PAIR])

    def bm(j):
        val[f"bm{j}"] = _dot(ym_ref[...], wt_ref[:, OFF_WBM + j * PAIR:OFF_WBM + (j + 1) * PAIR])

    def merge(j):
        cs = slice(j * PAIR, (j + 1) * PAIR)
        y = jax.nn.sigmoid(gr_ref[:, cs]) * val.pop(f"br{j}") + \
            jax.nn.sigmoid(gm_ref[:, cs]) * val.pop(f"bm{j}")
        y_ref[:, cs] = y.astype(BF16)

    for j in range(N_G // 2):
        steps[f"br{j}"] = functools.partial(br, j)
        steps[f"bm{j}"] = functools.partial(bm, j)
        steps[f"merge{j}"] = functools.partial(merge, j)

    @step("unperm")
    def _():
        ytime_ref[...] = _dot(unperm_ref[...], y_ref[...]).astype(BF16)

    @step("out")
    def _():
        val["out"] = _dot(ytime_ref[...], wt_ref[:, OFF_WOUT:OFF_WOUT + D_MODEL])

    @step("post")
    def _():
        out = val.pop("out")
        ms_o = jnp.mean(out * out, axis=-1, keepdims=True)
        o_ref[...] = xprev_ref[...] + out * lax.rsqrt(ms_o + EPS) * p_ref[P_GPOST:P_GPOST + 1, :]

    done = set()
    for name in ORDER:
        assert all(d in done for d in STEP_DEPS[name]), (name, STEP_DEPS[name])
        steps[name]()
        done.add(name)
    assert done == set(steps), set(steps) - done


def _block_diag(w, group):
    h, d, e = w.shape
    across = jnp.tile(w.reshape(h // group, group, d, e), (1, 1, 1, group))
    on_diagonal = np.arange(group * e)[None, :] // e == np.arange(group)[:, None]
    return jnp.where(on_diagonal[None, :, None, :], across, 0.0).reshape(h // group, group * d, group * e)


def _row_order_constants():
    p = np.arange(TILE_T)
    q = p % CHUNK
    time_of = (p // CHUNK) * CHUNK + (q % SUBLANES) * SEG + q // SUBLANES
    perm = np.zeros((TILE_T, TILE_T), np.float32)
    perm[p, time_of] = 1.0
    t = time_of[:CHUNK]
    causal_p = (t[None, :] <= t[:, None]).astype(np.float32)
    return jnp.asarray(perm, BF16), jnp.asarray(causal_p, F32)


def _layer(x, g_pre, w_in, rg_conv_w, rg_conv_b, rg_w_a, rg_b_a, rg_w_x, rg_b_x, rg_lambda,
           ml_conv_w, ml_conv_b, ml_w_q, ml_w_k, ml_w_v, ml_b_i, ml_b_f, ml_norm_g, ml_skip,
           w_branch_r, w_branch_m, w_out, g_post):
    bsz, seq, d = x.shape
    assert d == D_MODEL and seq % TILE_T == 0

    n_if = 2 * H_M
    o_i = 5 * D_MODEL
    def odd_width(*ws):
        width = sum(w.shape[-1] for w in ws)
        tiles = -(-width // LANES)
        zeros = jnp.zeros((ws[0].shape[0], (tiles + 1 - tiles % 2) * LANES - width), ws[0].dtype)
        return jnp.concatenate(list(ws) + [zeros], axis=1).astype(BF16)

    w_main = w_in.astype(BF16)
    assert -(-w_main.shape[-1] // LANES) % 2 == 1
    w_gate = odd_width(w_in[:, o_i + n_if:])
    w_if = odd_width(w_in[:, o_i:o_i + n_if])
    w_tail = odd_width(w_branch_r, w_branch_m, w_out)

    group = COL_G // DH_R
    wax = jnp.concatenate([_block_diag(rg_w_a, group), _block_diag(rg_w_x, group)], axis=2).astype(BF16)
    wqk = jnp.concatenate([ml_w_q, ml_w_k * (1.0 / math.sqrt(DH_M))], axis=2).astype(BF16)

    b_if = jnp.concatenate([ml_b_i, ml_b_f, jnp.zeros((D_MODEL - n_if,), F32)])
    rows = [g_pre[None], rg_conv_w, rg_conv_b[None], rg_b_a[None], rg_b_x[None], rg_lambda[None],
            ml_conv_w, ml_conv_b[None], ml_norm_g[None], ml_skip[None], g_post[None], b_if[None]]
    n_rows = sum(r.shape[0] for r in rows)
    params = jnp.concatenate(rows + [jnp.zeros((P_ROWS - n_rows, D_MODEL), F32)], axis=0)

    perm, causal_p = _row_order_constants()

    nt = seq // TILE_T
    n_tiles = bsz * nt
    resident = pl.BlockSpec(memory_space=pltpu.VMEM)

    def tile_in(i):
        j = jnp.minimum(i, n_tiles - 1)
        return (j // nt, j % nt, 0)

    def tile_out(i):
        j = jnp.maximum(i - 1, 0)
        return (j // nt, j % nt, 0)

    x_tile = pl.BlockSpec((None, TILE_T, D_MODEL), tile_in)
    o_tile = pl.BlockSpec((None, TILE_T, D_MODEL), tile_out)
    slab_f32 = pltpu.VMEM((TILE_T, D_MODEL), F32)
    slab_bf16 = pltpu.VMEM((TILE_T, D_MODEL), BF16)
    conv_tail = pltpu.VMEM(((CONV_W - 1) * SUBLANES, D_MODEL), F32)
    scratch = [
        slab_bf16,
        conv_tail, conv_tail,
        slab_bf16,
        slab_f32,
        slab_bf16, slab_bf16, slab_bf16,
        slab_f32,
        slab_bf16,
        slab_bf16,
        slab_f32, slab_f32, slab_f32, slab_f32,
        pltpu.VMEM((SUBLANES, D_MODEL), F32),
        pltpu.VMEM((H_M, DH_M, DH_M), F32),
        pltpu.VMEM((SUBLANES, DH_M), F32),
        pltpu.VMEM((1, LANES), F32),
        slab_bf16,
    ]
    return pl.pallas_call(
        functools.partial(_block_kernel, tiles_per_seq=nt),
        out_shape=jax.ShapeDtypeStruct(x.shape, x.dtype),
        grid=(n_tiles + 1,),
        in_specs=[x_tile, o_tile] + [resident] * 11,
        out_specs=o_tile,
        scratch_shapes=scratch,
        compiler_params=pltpu.CompilerParams(
            dimension_semantics=("arbitrary",),
            vmem_limit_bytes=VMEM_LIMIT_BYTES),
        name="hybrid_block",
    )(x, x, w_main, w_gate, w_if, wax, wqk, ml_w_v.astype(BF16),
      w_tail, params,
      perm, perm.T, causal_p)


def kernel(x, g_pre, w_in, rg_conv_w, rg_conv_b, rg_w_a, rg_b_a, rg_w_x, rg_b_x, rg_lambda,
           ml_conv_w, ml_conv_b, ml_w_q, ml_w_k, ml_w_v, ml_b_i, ml_b_f, ml_norm_g, ml_skip,
           w_branch_r, w_branch_m, w_out, g_post):
    h = x
    for l in range(g_pre.shape[0]):
        h = _layer(h, g_pre[l], w_in[l], rg_conv_w[l], rg_conv_b[l], rg_w_a[l], rg_b_a[l],
                   rg_w_x[l], rg_b_x[l], rg_lambda[l], ml_conv_w[l], ml_conv_b[l],
                   ml_w_q[l], ml_w_k[l], ml_w_v[l], ml_b_i[l], ml_b_f[l], ml_norm_g[l],
                   ml_skip[l], w_branch_r[l], w_branch_m[l], w_out[l], g_post[l])
    return h
```
